```python
import numpy as np
import jax
import jax.numpy as jnp
from jax import lax

D_MODEL = 1024
BATCH = 8
SEQ = 2048
DEPTH = 1

SSM_EXPAND = 1
D_INNER = SSM_EXPAND * D_MODEL
SSM_HEAD_DIM = 64
SSM_HEADS = D_INNER // SSM_HEAD_DIM
SSM_GROUPS = 4
D_STATE = 128
CONV_K = 4
CONV_DIM = D_INNER + 2 * SSM_GROUPS * D_STATE
SSD_CHUNK = 128
DT_MIN = 1e-3
DT_MAX = 1e-1
NSA_HEAD_DIM = 64
NSA_HEADS = D_MODEL // NSA_HEAD_DIM
NSA_GROUPS = 4
NSA_REP = NSA_HEADS // NSA_GROUPS
NSA_WIDTH = NSA_HEADS * NSA_HEAD_DIM
NSA_KV_WIDTH = NSA_GROUPS * NSA_HEAD_DIM
CMP_BLOCK = 32
CMP_STRIDE = 16
CMP_HIDDEN = 2 * NSA_HEAD_DIM
SEL_BLOCK = 64
SEL_TOPN = 8
WINDOW = 256
Q_BLOCK = 64
N_EXPERTS = 32
TOP_K = 4
D_EXPERT = D_MODEL
SWIGLU_ALPHA = 1.702
SWIGLU_LIMIT = 7.0
NORM_EPS = 1e-5

IN_WIDTHS = (
    D_INNER,
    CONV_DIM,
    SSM_HEADS,
    NSA_WIDTH,
    NSA_KV_WIDTH, NSA_KV_WIDTH,
    NSA_KV_WIDTH, NSA_KV_WIDTH,
    NSA_KV_WIDTH, NSA_KV_WIDTH,
    3 * NSA_HEADS,
    2 * D_MODEL,
)
IN_DIM = int(sum(IN_WIDTHS))
IN_OFFSETS = [int(o) for o in np.cumsum(IN_WIDTHS)[:-1]]

kernel_name = 'hybrid_ssd_nsa_moe_block'


def rms_norm(x, w):
    xf = x.astype(jnp.float32)
    y = xf * lax.rsqrt(jnp.mean(xf * xf, axis=-1, keepdims=True) + NORM_EPS)
    return (y * w.astype(jnp.float32)).astype(x.dtype)


def masked_softmax(logits, mask):
    l = jnp.where(mask, logits.astype(jnp.float32), -jnp.inf)
    m = jnp.max(l, axis=-1, keepdims=True)
    m = jnp.where(jnp.isfinite(m), m, 0.0)
    e = jnp.where(mask, jnp.exp(l - m), 0.0)
    return e / jnp.maximum(jnp.sum(e, axis=-1, keepdims=True), 1e-30)


def segsum(a):
    t = a.shape[-1]
    cs = jnp.cumsum(a, axis=-1)
    diff = cs[..., :, None] - cs[..., None, :]
    return jnp.where(jnp.tril(jnp.ones((t, t), dtype=bool)), diff, -jnp.inf)


def ssd_chunked_scan(xh, dt, a, b_mat, c_mat):
    bsz, s, nh, hp = xh.shape
    nc = s // SSD_CHUNK

    def chunk(t):
        return t.reshape((bsz, nc, SSD_CHUNK) + t.shape[2:])

    xc = chunk(xh * dt[..., None])
    bc = chunk(b_mat)
    cc = chunk(c_mat)
    a_dt = jnp.moveaxis(chunk(dt * a), -1, 1)
    a_cs = jnp.cumsum(a_dt, axis=-1)
    decay_in = jnp.exp(segsum(a_dt))
    y_diag = jnp.einsum('bclhn,bcshn,bhcls,bcshp->bclhp', cc, bc, decay_in, xc)
    decay_to_end = jnp.exp(a_cs[..., -1:] - a_cs)
    states = jnp.einsum('bclhn,bhcl,bclhp->bchpn', bc, decay_to_end, xc)
    chunk_decay = jnp.exp(a_cs[..., -1])

    def step(carry, inp):
        st, dec = inp
        return carry * dec[..., None, None] + st, carry

    init = jnp.zeros((bsz, nh, hp, D_STATE), xc.dtype)
    _, prev = lax.scan(step, init, (jnp.moveaxis(states, 1, 0), jnp.moveaxis(chunk_decay, 2, 0)))
    prev = jnp.moveaxis(prev, 0, 1)
    y_off = jnp.einsum('bclhn,bchpn,bhcl->bclhp', cc, prev, jnp.exp(a_cs))
    return (y_diag + y_off).reshape(bsz, s, nh, hp)


def mamba2_mixer(z, xbc, dt_raw, conv_w, conv_b, dt_bias, a_log, d_skip, norm_w):
    bsz, s, _ = xbc.shape
    xbc = lax.conv_general_dilated(
        xbc, conv_w[:, None, :], window_strides=(1,), padding=[(CONV_K - 1, 0)],
        dimension_numbers=('NWC', 'WIO', 'NWC'), feature_group_count=CONV_DIM)
    xbc = jax.nn.silu(xbc + conv_b)
    xs, b_mat, c_mat = jnp.split(xbc, [D_INNER, D_INNER + SSM_GROUPS * D_STATE], axis=-1)
    f32 = jnp.float32
    xh = xs.reshape(bsz, s, SSM_HEADS, SSM_HEAD_DIM).astype(f32)
    rep = SSM_HEADS // SSM_GROUPS
    b_mat = jnp.repeat(b_mat.reshape(bsz, s, SSM_GROUPS, D_STATE), rep, axis=2).astype(f32)
    c_mat = jnp.repeat(c_mat.reshape(bsz, s, SSM_GROUPS, D_STATE), rep, axis=2).astype(f32)
    dt = jax.nn.softplus(dt_raw.astype(f32) + dt_bias.astype(f32))
    a = -jnp.exp(a_log.astype(f32))
    y = ssd_chunked_scan(xh, dt, a, b_mat, c_mat) + d_skip.astype(f32)[:, None] * xh
    y = y.reshape(bsz, s, D_INNER) * jax.nn.silu(z.astype(f32))
    yg = y.reshape(bsz, s, SSM_GROUPS, D_INNER // SSM_GROUPS)
    yg = yg * lax.rsqrt(jnp.mean(yg * yg, axis=-1, keepdims=True) + NORM_EPS)
    return (yg.reshape(bsz, s, D_INNER) * norm_w.astype(f32)).astype(z.dtype)


def compress_blocks(kv, pos, w1, w2):
    bsz, s = kv.shape[:2]
    n_cmp = (s - CMP_BLOCK) // CMP_STRIDE + 1
    idx = np.arange(n_cmp)[:, None] * CMP_STRIDE + np.arange(CMP_BLOCK)[None, :]
    blocks = kv[:, idx] + pos[None, None, :, None, :]
    flat = jnp.moveaxis(blocks, 3, 2).reshape(bsz, n_cmp, NSA_GROUPS, CMP_BLOCK * NSA_HEAD_DIM)
    return jax.nn.gelu(flat @ w1) @ w2


def nsa_mixer(q, k_cmp, v_cmp, k_sel, v_sel, k_win, v_win, gate_logits,
              cmp_pos_k, cmp_w1_k, cmp_w2_k, cmp_pos_v, cmp_w1_v, cmp_w2_v):
    bsz, s, _ = q.shape
    G, R, DH = NSA_GROUPS, NSA_REP, NSA_HEAD_DIM
    q = q.reshape(bsz, s, G, R, DH) * (DH ** -0.5)
    kvshape = (bsz, s, G, DH)
    k_cmp, v_cmp, k_sel, v_sel, k_win, v_win = [
        t.reshape(kvshape) for t in (k_cmp, v_cmp, k_sel, v_sel, k_win, v_win)]
    t_pos = jnp.arange(s)

    kc = compress_blocks(k_cmp, cmp_pos_k, cmp_w1_k, cmp_w2_k)
    vc = compress_blocks(v_cmp, cmp_pos_v, cmp_w1_v, cmp_w2_v)
    n_cmp = kc.shape[1]
    cmp_end = jnp.arange(n_cmp) * CMP_STRIDE + CMP_BLOCK - 1
    cmp_mask = cmp_end[None, :] <= t_pos[:, None]
    p_cmp = masked_softmax(jnp.einsum('bsgrd,bcgd->bgrsc', q, kc), cmp_mask)
    o_cmp = jnp.einsum('bgrsc,bcgd->bsgrd', p_cmp.astype(vc.dtype), vc)

    n_sel = s // SEL_BLOCK
    c_start = np.arange(n_cmp) * CMP_STRIDE
    s_start = np.arange(n_sel) * SEL_BLOCK
    overlap = (c_start[:, None] < s_start[None, :] + SEL_BLOCK) & (c_start[:, None] + CMP_BLOCK > s_start[None, :])
    imp = jnp.einsum('bgrsc,cj->bgsj', p_cmp, jnp.asarray(overlap, p_cmp.dtype))
    cur = t_pos // SEL_BLOCK
    j = jnp.arange(n_sel)[None, :]
    forced = (j == 0) | (j == cur[:, None]) | (j == cur[:, None] - 1)
    valid = j <= cur[:, None]
    imp = jnp.where(forced, jnp.inf, jnp.where(valid, imp, -jnp.inf))
    n_top = min(SEL_TOPN, n_sel)
    _, sel_idx = lax.top_k(imp, n_top)

    ks = jnp.transpose(k_sel.reshape(bsz, n_sel, SEL_BLOCK, G, DH), (0, 3, 1, 2, 4))
    vs = jnp.transpose(v_sel.reshape(bsz, n_sel, SEL_BLOCK, G, DH), (0, 3, 1, 2, 4))
    pad = ((0, 0), (WINDOW, 0), (0, 0), (0, 0))
    kwp = jnp.pad(k_win, pad)
    vwp = jnp.pad(v_win, pad)
    bi = jnp.arange(bsz)[:, None, None, None]
    gi = jnp.arange(G)[None, :, None, None]
    sb_off = jnp.arange(SEL_BLOCK)
    w_off = jnp.arange(Q_BLOCK + WINDOW)

    def query_block(i):
        q0 = i * Q_BLOCK
        qb = lax.dynamic_slice_in_dim(q, q0, Q_BLOCK, axis=1)
        tq = q0 + jnp.arange(Q_BLOCK)
        idx = lax.dynamic_slice_in_dim(sel_idx, q0, Q_BLOCK, axis=2)
        kg = ks[bi, gi, idx].reshape(bsz, G, Q_BLOCK, n_top * SEL_BLOCK, DH)
        vg = vs[bi, gi, idx].reshape(bsz, G, Q_BLOCK, n_top * SEL_BLOCK, DH)
        kpos = (idx[..., None] * SEL_BLOCK + sb_off).reshape(bsz, G, Q_BLOCK, n_top * SEL_BLOCK)
        smask = (kpos <= tq[None, None, :, None])[:, :, None]
        p_sel = masked_softmax(jnp.einsum('bqgrd,bgqmd->bgrqm', qb, kg), smask)
        o_sel = jnp.einsum('bgrqm,bgqmd->bqgrd', p_sel.astype(vg.dtype), vg)
        kw = lax.dynamic_slice_in_dim(kwp, q0, Q_BLOCK + WINDOW, axis=1)
        vw = lax.dynamic_slice_in_dim(vwp, q0, Q_BLOCK + WINDOW, axis=1)
        wpos = q0 - WINDOW + w_off
        wmask = (wpos[None, :] <= tq[:, None]) & (wpos[None, :] > tq[:, None] - WINDOW) & (wpos[None, :] >= 0)
        p_win = masked_softmax(jnp.einsum('bqgrd,bkgd->bgrqk', qb, kw), wmask)
        o_win = jnp.einsum('bgrqk,bkgd->bqgrd', p_win.astype(vw.dtype), vw)
        return o_sel, o_win

    o_sel, o_win = lax.map(query_block, jnp.arange(s // Q_BLOCK))
    o_sel = jnp.moveaxis(o_sel, 0, 1).reshape(bsz, s, G, R, DH)
    o_win = jnp.moveaxis(o_win, 0, 1).reshape(bsz, s, G, R, DH)

    g = jax.nn.sigmoid(gate_logits.reshape(bsz, s, G, R, 3))
    o = g[..., 0, None] * o_cmp + g[..., 1, None] * o_sel + g[..., 2, None] * o_win
    return o.reshape(bsz, s, NSA_WIDTH).astype(q.dtype)


def clamped_swiglu(h):
    x_glu = jnp.minimum(h[..., ::2], SWIGLU_LIMIT)
    x_lin = jnp.clip(h[..., 1::2], -SWIGLU_LIMIT, SWIGLU_LIMIT)
    return x_glu * jax.nn.sigmoid(SWIGLU_ALPHA * x_glu) * (x_lin + 1.0)


def moe_ffn(h, router_w, router_b, w_gate_up, b_gate_up, w_down, b_down):
    bsz, s, d = h.shape
    t = h.reshape(bsz * s, d)
    logits = (t @ router_w + router_b).astype(jnp.float32)
    top_v, top_i = lax.top_k(logits, TOP_K)
    top_w = jax.nn.softmax(top_v, axis=-1)
    combine = jnp.einsum('tk,tke->te', top_w, jax.nn.one_hot(top_i, N_EXPERTS, dtype=jnp.float32)).astype(h.dtype)
    out = jnp.zeros_like(t)
    for e in range(N_EXPERTS):
        a = clamped_swiglu(t @ w_gate_up[e] + b_gate_up[e])
        out = out + combine[:, e:e + 1] * (a @ w_down[e] + b_down[e])
    return out.reshape(bsz, s, d)


def setup_inputs(seed: int = 0) -> dict:
    key = jax.random.key(seed)
    ks = jax.random.split(key, 32)
    f32 = jnp.float32
    L = DEPTH

    def nrm(k, shape, scale):
        return jax.random.normal(k, shape, f32) * scale

    dt = jnp.exp(jax.random.uniform(ks[4], (L, SSM_HEADS), f32) * (np.log(DT_MAX) - np.log(DT_MIN)) + np.log(DT_MIN))
    return {
        'x': nrm(ks[0], (BATCH, SEQ, D_MODEL), 1.0),
        'mix_norm_w': 1.0 + nrm(ks[1], (L, D_MODEL), 0.05),
        'w_in': nrm(ks[2], (L, D_MODEL, IN_DIM), D_MODEL ** -0.5),
        'conv_w': nrm(ks[3], (L, CONV_K, CONV_DIM), CONV_K ** -0.5),
        'conv_b': nrm(ks[5], (L, CONV_DIM), 0.02),
        'dt_bias': dt + jnp.log(-jnp.expm1(-dt)),
        'a_log': jnp.log(jax.random.uniform(ks[6], (L, SSM_HEADS), f32, 1.0, 16.0)),
        'd_skip': 1.0 + nrm(ks[7], (L, SSM_HEADS), 0.05),
        'ssm_norm_w': 1.0 + nrm(ks[8], (L, D_INNER), 0.05),
        'cmp_pos_k': nrm(ks[9], (L, CMP_BLOCK, NSA_HEAD_DIM), 0.1),
        'cmp_w1_k': nrm(ks[10], (L, CMP_BLOCK * NSA_HEAD_DIM, CMP_HIDDEN), (CMP_BLOCK * NSA_HEAD_DIM) ** -0.5),
        'cmp_w2_k': nrm(ks[11], (L, CMP_HIDDEN, NSA_HEAD_DIM), CMP_HIDDEN ** -0.5),
        'cmp_pos_v': nrm(ks[12], (L, CMP_BLOCK, NSA_HEAD_DIM), 0.1),
        'cmp_w1_v': nrm(ks[13], (L, CMP_BLOCK * NSA_HEAD_DIM, CMP_HIDDEN), (CMP_BLOCK * NSA_HEAD_DIM) ** -0.5),
        'cmp_w2_v': nrm(ks[14], (L, CMP_HIDDEN, NSA_HEAD_DIM), CMP_HIDDEN ** -0.5),
        'w_branch_ssm': nrm(ks[15], (L, D_INNER, D_MODEL), D_INNER ** -0.5),
        'w_branch_nsa': nrm(ks[16], (L, NSA_WIDTH, D_MODEL), NSA_WIDTH ** -0.5),
        'w_out': nrm(ks[17], (L, D_MODEL, D_MODEL), D_MODEL ** -0.5),
        'ffn_norm_w': 1.0 + nrm(ks[18], (L, D_MODEL), 0.05),
        'router_w': nrm(ks[19], (L, D_MODEL, N_EXPERTS), D_MODEL ** -0.5),
        'router_b': nrm(ks[20], (L, N_EXPERTS), 0.01),
        'w_gate_up': nrm(ks[21], (L, N_EXPERTS, D_MODEL, 2 * D_EXPERT), D_MODEL ** -0.5),
        'b_gate_up': nrm(ks[22], (L, N_EXPERTS, 2 * D_EXPERT), 0.01),
        'w_down': nrm(ks[23], (L, N_EXPERTS, D_EXPERT, D_MODEL), D_EXPERT ** -0.5),
        'b_down': nrm(ks[24], (L, N_EXPERTS, D_MODEL), 0.01),
        'final_norm_w': 1.0 + nrm(ks[25], (D_MODEL,), 0.05),
    }


def reference(x, mix_norm_w, w_in, conv_w, conv_b, dt_bias, a_log, d_skip, ssm_norm_w,
              cmp_pos_k, cmp_w1_k, cmp_w2_k, cmp_pos_v, cmp_w1_v, cmp_w2_v,
              w_branch_ssm, w_branch_nsa, w_out, ffn_norm_w, router_w, router_b,
              w_gate_up, b_gate_up, w_down, b_down, final_norm_w):
    for l in range(DEPTH):
        h = rms_norm(x, mix_norm_w[l])
        proj = h @ w_in[l]
        (z, xbc, dt_raw, q, k_c, v_c, k_s, v_s, k_w, v_w,
         nsa_gate, merge_gate) = jnp.split(proj, IN_OFFSETS, axis=-1)
        y_ssm = mamba2_mixer(z, xbc, dt_raw, conv_w[l], conv_b[l], dt_bias[l], a_log[l],
                             d_skip[l], ssm_norm_w[l])
        y_nsa = nsa_mixer(q, k_c, v_c, k_s, v_s, k_w, v_w, nsa_gate,
                          cmp_pos_k[l], cmp_w1_k[l], cmp_w2_k[l], cmp_pos_v[l], cmp_w1_v[l], cmp_w2_v[l])
        g_ssm, g_nsa = jnp.split(jax.nn.sigmoid(merge_gate), 2, axis=-1)
        merged = g_ssm * (y_ssm @ w_branch_ssm[l]) + g_nsa * (y_nsa @ w_branch_nsa[l])
        x = x + merged @ w_out[l]
        h = rms_norm(x, ffn_norm_w[l])
        x = x + moe_ffn(h, router_w[l], router_b[l], w_gate_up[l], b_gate_up[l], w_down[l], b_down[l])
    return rms_norm(x, final_norm_w)
```

```python
import functools

import numpy as np
import jax
import jax.numpy as jnp
from jax import lax
from jax.experimental import pallas as pl
from jax.experimental.pallas import tpu as pltpu

F32 = jnp.float32
BF16 = jnp.bfloat16
HIGHEST = lax.Precision.HIGHEST

D_MODEL = 1024
D_INNER = 1024
SSM_HEAD_DIM = 64
SSM_HEADS = 16
SSM_GROUPS = 4
D_STATE = 128
CONV_K = 4
XBC_DIM = D_INNER + 2 * SSM_GROUPS * D_STATE
SSD_CHUNK = 128
NSA_HEAD_DIM = 64
NSA_HEADS = 16
NSA_GROUPS = 4
NSA_REP = 4
NSA_KV_WIDTH = 256
CMP_BLOCK = 32
CMP_STRIDE = 16
CMP_HIDDEN = 128
SEL_BLOCK = 64
SEL_TOPN = 8
WINDOW = 256
N_EXPERTS = 32
TOP_K = 4
SWIGLU_ALPHA = 1.702
SWIGLU_LIMIT = 7.0
NORM_EPS = 1e-5

LANES = 128
NEG_BIG = -1e30
VMEM_LIMIT = 56 * 1024 * 1024

F_Z, F_MS, F_XBC, F_MN, F_SMALL = 0, 1024, 2048, 4096, 5120
F_WIDTH = 5376
Q_WIDTH = 1024 + 6 * NSA_KV_WIDTH


def _sigmoid(x):
    return 1.0 / (1.0 + jnp.exp(-x))


def _silu(x):
    return x * _sigmoid(x)


def _softplus(x):
    return jnp.maximum(x, 0.0) + jnp.log1p(jnp.exp(-jnp.abs(x)))


def _params(*sem):
    return pltpu.CompilerParams(dimension_semantics=sem, vmem_limit_bytes=VMEM_LIMIT)


def _norm_matmul_kernel(x_ref, nw_ref, w_ref, o_ref, h_ref):
    @pl.when(pl.program_id(1) == 0)
    def _():
        x = x_ref[...]
        var = jnp.mean(x * x, axis=-1, keepdims=True)
        h_ref[...] = (x * lax.rsqrt(var + NORM_EPS) * nw_ref[...]).astype(BF16)

    o_ref[...] = jnp.dot(h_ref[...], w_ref[...], preferred_element_type=F32).astype(o_ref.dtype)


def _norm_matmul(x2, norm_w, w, out_dtype, tm, tn):
    t, d = x2.shape
    n = w.shape[1]
    return pl.pallas_call(
        _norm_matmul_kernel,
        grid=(t // tm, n // tn),
        in_specs=[
            pl.BlockSpec((tm, d), lambda i, j: (i, 0)),
            pl.BlockSpec((1, d), lambda i, j: (0, 0)),
            pl.BlockSpec((d, tn), lambda i, j: (0, j)),
        ],
        out_specs=pl.BlockSpec((tm, tn), lambda i, j: (i, j)),
        out_shape=jax.ShapeDtypeStruct((t, n), out_dtype),
        scratch_shapes=[pltpu.VMEM((tm, d), BF16)],
        compiler_params=_params("parallel", "arbitrary"),
        name="norm_matmul",
    )(x2, norm_w.reshape(1, d), w)


def _ssd_kernel(z_ref, xbc_ref, sm_ref, dtt_ref, convw_ref, convb_ref, dtb_ref, dtbt_ref,
                alog_ref, alogt_ref, dskip_ref, nw_ref, expand_ref, y_ref,
                xbuf, state, ydiag):
    L = SSD_CHUNK
    c = pl.program_id(1)

    @pl.when(c == 0)
    def _():
        xbuf[0:8, :] = jnp.zeros((8, XBC_DIM), F32)
        state[...] = jnp.zeros(state.shape, F32)

    xbuf[8:8 + L, :] = xbc_ref[...]
    acc = convb_ref[...] + convw_ref[0:1, :] * xbuf[5:5 + L, :]
    for k in range(1, CONV_K):
        acc = acc + convw_ref[k:k + 1, :] * xbuf[5 + k:5 + k + L, :]
    xbuf[0:8, :] = xbuf[L:L + 8, :]
    xbc = _silu(acc)
    xs = xbc[:, :D_INNER]

    lane = lax.broadcasted_iota(jnp.int32, (1, LANES), 1)
    a_row = jnp.where(lane < SSM_HEADS, -jnp.exp(alog_ref[...]), 0.0)
    dt = _softplus(sm_ref[...] + dtb_ref[...])
    a_dt = dt * a_row
    dt_t = _softplus(dtt_ref[0] + dtbt_ref[...])
    a_dt_t = dt_t * (-jnp.exp(alogt_ref[...]))

    row = lax.broadcasted_iota(jnp.int32, (L, L), 0)
    col = lax.broadcasted_iota(jnp.int32, (L, L), 1)
    lower = row >= col
    tri = lower.astype(F32)
    tri_t = (row <= col).astype(F32)
    cs = jnp.dot(tri, a_dt, precision=HIGHEST, preferred_element_type=F32)
    cs_t = jnp.dot(a_dt_t, tri_t, precision=HIGHEST, preferred_element_type=F32)
    cs_last = cs[L - 1:L, :]

    stacked = jnp.concatenate([dt, jnp.exp(cs_last - cs), jnp.exp(cs)], axis=0)
    wide = jnp.dot(stacked, expand_ref[...], precision=HIGHEST, preferred_element_type=F32)
    dt_x = wide[0:L]
    dte_x = wide[L:2 * L]
    ecs_x = wide[2 * L:3 * L]

    xdt = xs * dt_x
    xw = (xdt * dte_x).astype(BF16)
    xdt_b = xdt.astype(BF16)

    hpg = SSM_HEADS // SSM_GROUPS
    gw = hpg * SSM_HEAD_DIM
    y_off_parts = []
    for g in range(SSM_GROUPS):
        b_g = xbc[:, D_INNER + g * D_STATE:D_INNER + (g + 1) * D_STATE].astype(BF16)
        c_g = xbc[:, D_INNER + SSM_GROUPS * D_STATE + g * D_STATE:
                  D_INNER + SSM_GROUPS * D_STATE + (g + 1) * D_STATE].astype(BF16)
        cb = lax.dot_general(c_g, b_g, (((1,), (1,)), ((), ())), preferred_element_type=F32)
        for hh in range(hpg):
            h = g * hpg + hh
            seg = cs[:, h:h + 1] - cs_t[h:h + 1, :]
            decay = jnp.exp(jnp.where(lower, seg, -jnp.inf))
            m = (cb * decay).astype(BF16)
            ydiag[:, h * SSM_HEAD_DIM:(h + 1) * SSM_HEAD_DIM] = jnp.dot(
                m, xdt_b[:, h * SSM_HEAD_DIM:(h + 1) * SSM_HEAD_DIM], preferred_element_type=F32)
        st_prev = state[g]
        y_off_parts.append(jnp.dot(c_g, st_prev.astype(BF16), preferred_element_type=F32))
        st_new = lax.dot_general(b_g, xw[:, g * gw:(g + 1) * gw], (((0,), (0,)), ((), ())),
                                 preferred_element_type=F32)
        state[g] = st_prev * ecs_x[L - 1:L, g * gw:(g + 1) * gw] + st_new
    y_off = jnp.concatenate(y_off_parts, axis=1) * ecs_x

    y = (ydiag[...] + y_off + dskip_ref[...] * xs) * _silu(z_ref[...])
    for g in range(SSM_GROUPS):
        yg = y[:, g * gw:(g + 1) * gw]
        yg = yg * lax.rsqrt(jnp.mean(yg * yg, axis=-1, keepdims=True) + NORM_EPS)
        y_ref[:, g * gw:(g + 1) * gw] = (yg * nw_ref[:, g * gw:(g + 1) * gw]).astype(y_ref.dtype)


def _ssd(proj_f, dt_t, conv_w, conv_b, dt_bias, a_log, d_skip, norm_w, bsz, seq):
    L = SSD_CHUNK
    nc = seq // L
    pad = LANES - SSM_HEADS
    dtb = jnp.pad(dt_bias, (0, pad)).reshape(1, LANES)
    alog = jnp.pad(a_log, (0, pad)).reshape(1, LANES)
    dskip_x = jnp.repeat(d_skip, SSM_HEAD_DIM).reshape(1, D_INNER)
    expand = (np.arange(LANES)[:, None] == (np.arange(D_INNER)[None, :] // SSM_HEAD_DIM)).astype(np.float32)
    const = lambda shape: pl.BlockSpec(shape, lambda b, c: (0,) * len(shape))
    return pl.pallas_call(
        _ssd_kernel,
        grid=(bsz, nc),
        in_specs=[
            pl.BlockSpec((L, D_INNER), lambda b, c: (b * nc + c, F_Z // D_INNER)),
            pl.BlockSpec((L, XBC_DIM), lambda b, c: (b * nc + c, F_XBC // XBC_DIM)),
            pl.BlockSpec((L, LANES), lambda b, c: (b * nc + c, F_SMALL // LANES)),
            pl.BlockSpec((1, SSM_HEADS, L), lambda b, c: (b, 0, c)),
            const((CONV_K, XBC_DIM)), const((1, XBC_DIM)),
            const((1, LANES)), const((SSM_HEADS, 1)),
            const((1, LANES)), const((SSM_HEADS, 1)),
            const((1, D_INNER)), const((1, D_INNER)),
            const((LANES, D_INNER)),
        ],
        out_specs=pl.BlockSpec((L, D_INNER), lambda b, c: (b * nc + c, 0)),
        out_shape=jax.ShapeDtypeStruct((bsz * seq, D_INNER), BF16),
        scratch_shapes=[
            pltpu.VMEM((L + 8, XBC_DIM), F32),
            pltpu.VMEM((SSM_GROUPS, D_STATE, (SSM_HEADS // SSM_GROUPS) * SSM_HEAD_DIM), F32),
            pltpu.VMEM((L, D_INNER), F32),
        ],
        compiler_params=_params("parallel", "arbitrary"),
        name="ssd",
    )(proj_f, proj_f, proj_f, dt_t, conv_w, conv_b.reshape(1, XBC_DIM),
      dtb, dt_bias.reshape(SSM_HEADS, 1), alog, a_log.reshape(SSM_HEADS, 1),
      dskip_x, norm_w.reshape(1, D_INNER), jnp.asarray(expand))


def _gelu_tanh(x):
    return 0.5 * x * (1.0 + jnp.tanh(np.sqrt(2.0 / np.pi) * (x + 0.044715 * (x * x * x))))


def _compress_kernel(cur_ref, nxt_ref, pos_ref, w1_ref, w2_ref, o_ref):
    half = CMP_STRIDE * NSA_HEAD_DIM
    lo = (cur_ref[0, 0].astype(F32) + pos_ref[0:1, :]).astype(BF16)
    hi = (nxt_ref[0, 0].astype(F32) + pos_ref[1:2, :]).astype(BF16)
    hid = (jnp.dot(lo, w1_ref[0:half, :], preferred_element_type=F32)
           + jnp.dot(hi, w1_ref[half:2 * half, :], preferred_element_type=F32))
    o_ref[0, 0] = jnp.dot(_gelu_tanh(hid).astype(BF16), w2_ref[...],
                          preferred_element_type=F32).astype(o_ref.dtype)


def _compress(kv, pos, w1, w2, bsz, seq):
    nch = seq // CMP_STRIDE
    half = CMP_STRIDE * NSA_HEAD_DIM
    ch = kv.reshape(bsz, nch, CMP_STRIDE, NSA_GROUPS, NSA_HEAD_DIM)
    ch = jnp.transpose(ch, (0, 3, 1, 2, 4)).reshape(bsz, NSA_GROUPS, nch, half)
    nxt = jnp.concatenate([ch[:, :, 1:], jnp.zeros_like(ch[:, :, :1])], axis=2)
    blk = pl.BlockSpec((1, 1, nch, half), lambda b, g: (b, g, 0, 0))
    return pl.pallas_call(
        _compress_kernel,
        grid=(bsz, NSA_GROUPS),
        in_specs=[
            blk, blk,
            pl.BlockSpec((2, half), lambda b, g: (0, 0)),
            pl.BlockSpec((2 * half, CMP_HIDDEN), lambda b, g: (0, 0)),
            pl.BlockSpec((CMP_HIDDEN, NSA_HEAD_DIM), lambda b, g: (0, 0)),
        ],
        out_specs=pl.BlockSpec((1, 1, nch, NSA_HEAD_DIM), lambda b, g: (b, g, 0, 0)),
        out_shape=jax.ShapeDtypeStruct((bsz, NSA_GROUPS, nch, NSA_HEAD_DIM), BF16),
        compiler_params=_params("parallel", "parallel"),
        name="compress",
    )(ch, nxt, pos.reshape(2, half), w1.astype(BF16), w2.astype(BF16))


NSA_TQ = 128
NSA_KC = 128


def _nsa_kernel(q_ref, kc_ref, vc_ref, ks_ref, vs_ref, kw_ref, vw_ref, gate_ref,
                overlap_ref, expand_ref, o_ref, selx_ref):
    R, TQ, KC, DH = NSA_REP, NSA_TQ, NSA_KC, NSA_HEAD_DIM
    i = pl.program_id(2)
    q0 = i * TQ
    qs = q_ref[0, 0].reshape(R * TQ, DH)
    scale = DH ** -0.5
    tpos = q0 + lax.broadcasted_iota(jnp.int32, (1, TQ, 1), 1)
    nt = (((1,), (1,)), ((), ()))

    sc = (lax.dot_general(qs, kc_ref[0, 0], nt, preferred_element_type=F32) * scale).reshape(R, TQ, KC)
    cmp_end = lax.broadcasted_iota(jnp.int32, (1, 1, KC), 2) * CMP_STRIDE + (CMP_BLOCK - 1)
    cmask = cmp_end <= tpos
    lg = jnp.where(cmask, sc, -jnp.inf)
    mx = jnp.max(lg, axis=-1, keepdims=True)
    mx = jnp.where(mx > -jnp.inf, mx, 0.0)
    e = jnp.where(cmask, jnp.exp(lg - mx), 0.0)
    p_cmp = e / jnp.maximum(jnp.sum(e, axis=-1, keepdims=True), 1e-30)
    o_cmp = jnp.dot(p_cmp.reshape(R * TQ, KC).astype(BF16), vc_ref[0, 0], preferred_element_type=F32)

    p4 = p_cmp[0] + p_cmp[1] + p_cmp[2] + p_cmp[3]
    imp = jnp.dot(p4, overlap_ref[...], precision=HIGHEST, preferred_element_type=F32)
    nsel = imp.shape[1]
    tq2 = q0 + lax.broadcasted_iota(jnp.int32, (TQ, 1), 0)
    cur = tq2 // SEL_BLOCK
    jj = lax.broadcasted_iota(jnp.int32, (1, nsel), 1)
    forced = (jj == 0) | (jj == cur) | (jj == cur - 1)
    val = jnp.where(forced, jnp.inf, jnp.where(jj <= cur, imp, -jnp.inf))
    rank = jnp.zeros((TQ, nsel), jnp.int32)
    for b in range(nsel):
        cb = val[:, b:b + 1]
        beats = (cb > val) | ((cb == val) & (jj > b))
        rank = rank + beats.astype(jnp.int32)
    sel = (rank < SEL_TOPN).astype(BF16)
    selx = jnp.dot(sel, expand_ref[...], preferred_element_type=F32)
    for kb in range(selx_ref.shape[0]):
        selx_ref[kb] = selx[:, kb * KC:(kb + 1) * KC]

    def flash_step(carry, k, v, msk):
        m_i, l_i, acc = carry
        s = (lax.dot_general(qs, k, nt, preferred_element_type=F32) * scale).reshape(R, TQ, KC)
        s = jnp.where(msk, s, NEG_BIG)
        m_new = jnp.maximum(m_i, jnp.max(s, axis=-1, keepdims=True))
        alpha = jnp.exp(m_i - m_new)
        p = jnp.where(msk, jnp.exp(s - m_new), 0.0)
        l_new = alpha * l_i + jnp.sum(p, axis=-1, keepdims=True)
        pv = jnp.dot(p.reshape(R * TQ, KC).astype(BF16), v, preferred_element_type=F32)
        acc = acc * alpha.reshape(R * TQ, 1) + pv
        return m_new, l_new, acc

    def init():
        return (jnp.full((R, TQ, 1), NEG_BIG, F32), jnp.zeros((R, TQ, 1), F32),
                jnp.zeros((R * TQ, DH), F32))

    def finish(carry):
        _, l_i, acc = carry
        return acc / jnp.maximum(l_i, 1e-30).reshape(R * TQ, 1)

    kiota = lax.broadcasted_iota(jnp.int32, (1, 1, KC), 2)

    def sel_body(kb, carry):
        k0 = pl.multiple_of(kb * KC, KC)
        kpos = k0 + kiota
        msk = (selx_ref[kb] > 0.5)[None] & (kpos <= tpos)
        return flash_step(carry, ks_ref[0, 0, pl.ds(k0, KC), :], vs_ref[0, 0, pl.ds(k0, KC), :], msk)

    o_sel = finish(lax.fori_loop(0, i + 1, sel_body, init()))

    carry = init()
    for d in range(WINDOW // KC + 1):
        kidx = i - (WINDOW // KC) + d
        k0 = pl.multiple_of(jnp.maximum(kidx, 0) * KC, KC)
        kpos = k0 + kiota + jnp.where(kidx >= 0, 0, 2 * ks_ref.shape[2])
        msk = (kpos <= tpos) & (kpos > tpos - WINDOW)
        carry = flash_step(carry, kw_ref[0, 0, pl.ds(k0, KC), :], vw_ref[0, 0, pl.ds(k0, KC), :], msk)
    o_win = finish(carry)

    g = _sigmoid(gate_ref[0, 0].reshape(R * TQ, 3))
    o = g[:, 0:1] * o_cmp + g[:, 1:2] * o_sel + g[:, 2:3] * o_win
    o_ref[0, 0] = o.reshape(R, TQ, DH).astype(o_ref.dtype)


def _nsa(proj_q, gates, kc, vc, bsz, seq):
    G, R, DH, TQ = NSA_GROUPS, NSA_REP, NSA_HEAD_DIM, NSA_TQ
    n_cmp_pad = seq // CMP_STRIDE
    n_sel = seq // SEL_BLOCK
    q = jnp.transpose(proj_q[:, :1024].reshape(bsz, seq, G, R, DH), (0, 2, 3, 1, 4))

    def kv(idx):
        lo = 1024 + idx * NSA_KV_WIDTH
        return jnp.transpose(proj_q[:, lo:lo + NSA_KV_WIDTH].reshape(bsz, seq, G, DH), (0, 2, 1, 3))

    ks, vs, kw, vw = kv(2), kv(3), kv(4), kv(5)
    gt = jnp.transpose(gates.reshape(bsz, seq, G, R, 3), (0, 2, 3, 1, 4))

    c_start = np.arange(n_cmp_pad) * CMP_STRIDE
    s_start = np.arange(n_sel) * SEL_BLOCK
    overlap = ((c_start[:, None] < s_start[None, :] + SEL_BLOCK)
               & (c_start[:, None] + CMP_BLOCK > s_start[None, :])).astype(np.float32)
    overlap[(seq - CMP_BLOCK) // CMP_STRIDE + 1:] = 0.0
    expand = (np.arange(n_sel)[:, None] == (np.arange(seq)[None, :] // SEL_BLOCK)).astype(np.float32)

    qblk = pl.BlockSpec((1, 1, R, TQ, DH), lambda b, g, i: (b, g, 0, i, 0))
    cblk = pl.BlockSpec((1, 1, n_cmp_pad, DH), lambda b, g, i: (b, g, 0, 0))
    sblk = pl.BlockSpec((1, 1, seq, DH), lambda b, g, i: (b, g, 0, 0))
    out = pl.pallas_call(
        _nsa_kernel,
        grid=(bsz, G, seq // TQ),
        in_specs=[
            qblk, cblk, cblk, sblk, sblk, sblk, sblk,
            pl.BlockSpec((1, 1, R, TQ, 3), lambda b, g, i: (b, g, 0, i, 0)),
            pl.BlockSpec((n_cmp_pad, n_sel), lambda b, g, i: (0, 0)),
            pl.BlockSpec((n_sel, seq), lambda b, g, i: (0, 0)),
        ],
        out_specs=qblk,
        out_shape=jax.ShapeDtypeStruct((bsz, G, R, seq, DH), BF16),
        scratch_shapes=[pltpu.VMEM((seq // NSA_KC, TQ, NSA_KC), F32)],
        compiler_params=_params("parallel", "parallel", "arbitrary"),
        name="nsa_attention",
    )(q, kc, vc, ks, vs, kw, vw, gt, jnp.asarray(overlap), jnp.asarray(expand, dtype=BF16))
    return jnp.transpose(out, (0, 3, 1, 2, 4)).reshape(bsz * seq, G * R * DH)


def _merge_kernel(ys_ref, yn_ref, gs_ref, gn_ref, x_ref, wbs_ref, wbn_ref, wo_ref, fw_ref,
                  rw_ref, rb_ref, x1_ref, h_ref, ti_ref, tw_ref):
    a = jnp.dot(ys_ref[...], wbs_ref[...], preferred_element_type=F32)
    b = jnp.dot(yn_ref[...], wbn_ref[...], preferred_element_type=F32)
    merged = _sigmoid(gs_ref[...]) * a + _sigmoid(gn_ref[...]) * b
    x1 = x_ref[...] + jnp.dot(merged.astype(BF16), wo_ref[...], preferred_element_type=F32)
    x1_ref[...] = x1
    var = jnp.mean(x1 * x1, axis=-1, keepdims=True)
    h = x1 * lax.rsqrt(var + NORM_EPS) * fw_ref[...]
    h_ref[...] = h.astype(h_ref.dtype)

    logits = jnp.dot(h, rw_ref[...], precision=HIGHEST, preferred_element_type=F32) + rb_ref[...]
    ne = logits.shape[1]
    jj = lax.broadcasted_iota(jnp.int32, (1, ne), 1)
    rank = jnp.zeros(logits.shape, jnp.int32)
    for e in range(ne):
        ce = logits[:, e:e + 1]
        beats = (ce > logits) | ((ce == logits) & (jj > e))
        rank = rank + beats.astype(jnp.int32)
    sel = rank < TOP_K
    mx = jnp.max(logits, axis=-1, keepdims=True)
    p = jnp.where(sel, jnp.exp(logits - mx), 0.0)
    p = p / jnp.sum(p, axis=-1, keepdims=True)
    for k in range(TOP_K):
        hit = rank == k
        ti_ref[:, k:k + 1] = jnp.sum(jnp.where(hit, jj, 0), axis=-1, keepdims=True)
        tw_ref[:, k:k + 1] = jnp.sum(jnp.where(hit, p, 0.0), axis=-1, keepdims=True)


def _merge_route(y_ssm, y_nsa, proj_f, x2, wbs, wbn, wo, ffn_w, router_w, router_b, tm):
    t, d = x2.shape
    row = lambda col: pl.BlockSpec((tm, d), lambda i: (i, col))
    const = lambda shape: pl.BlockSpec(shape, lambda i: (0,) * len(shape))
    return pl.pallas_call(
        _merge_kernel,
        grid=(t // tm,),
        in_specs=[
            row(0), row(0), row(F_MS // d), row(F_MN // d), row(0),
            const((d, d)), const((d, d)), const((d, d)), const((1, d)),
            const((d, N_EXPERTS)), const((1, N_EXPERTS)),
        ],
        out_specs=[row(0), row(0),
                   pl.BlockSpec((tm, TOP_K), lambda i: (i, 0)),
                   pl.BlockSpec((tm, TOP_K), lambda i: (i, 0))],
        out_shape=[jax.ShapeDtypeStruct((t, d), F32), jax.ShapeDtypeStruct((t, d), BF16),
                   jax.ShapeDtypeStruct((t, TOP_K), jnp.int32), jax.ShapeDtypeStruct((t, TOP_K), F32)],
        compiler_params=_params("parallel"),
        name="merge_route",
    )(y_ssm, y_nsa, proj_f, proj_f, x2, wbs.astype(BF16), wbn.astype(BF16), wo.astype(BF16),
      ffn_w.reshape(1, d), router_w, router_b.reshape(1, N_EXPERTS))


MOE_TM = 256


def _moe_kernel(te_ref, nu_ref, xs_ref, wgu_ref, bgu_ref, wd_ref, bd_ref, rw_ref, o_ref):
    i = pl.program_id(0)

    @pl.when(i < nu_ref[0])
    def _():
        d = wd_ref.shape[1]
        h1 = jnp.dot(xs_ref[...], wgu_ref[0], preferred_element_type=F32) + bgu_ref[0]
        glu = jnp.minimum(h1[:, :d], SWIGLU_LIMIT)
        lin = jnp.clip(h1[:, d:], -SWIGLU_LIMIT, SWIGLU_LIMIT)
        act = glu * _sigmoid(SWIGLU_ALPHA * glu) * (lin + 1.0)
        y = jnp.dot(act.astype(BF16), wd_ref[0], preferred_element_type=F32) + bd_ref[0]
        o_ref[...] = y * rw_ref[...]

    @pl.when(i >= nu_ref[0])
    def _():
        o_ref[...] = jnp.zeros(o_ref.shape, o_ref.dtype)


def _moe_experts(xs, row_w, tile_expert, n_used, wgu, bgu, wd, bd):
    p, d = xs.shape
    tm = MOE_TM
    grid_spec = pltpu.PrefetchScalarGridSpec(
        num_scalar_prefetch=2,
        grid=(p // tm,),
        in_specs=[
            pl.BlockSpec((tm, d), lambda i, te, nu: (i, 0)),
            pl.BlockSpec((1, d, 2 * d), lambda i, te, nu: (te[i], 0, 0)),
            pl.BlockSpec((1, 1, 2 * d), lambda i, te, nu: (te[i], 0, 0)),
            pl.BlockSpec((1, d, d), lambda i, te, nu: (te[i], 0, 0)),
            pl.BlockSpec((1, 1, d), lambda i, te, nu: (te[i], 0, 0)),
            pl.BlockSpec((tm, 1), lambda i, te, nu: (i, 0)),
        ],
        out_specs=pl.BlockSpec((tm, d), lambda i, te, nu: (i, 0)),
    )
    return pl.pallas_call(
        _moe_kernel,
        grid_spec=grid_spec,
        out_shape=jax.ShapeDtypeStruct((p, d), F32),
        compiler_params=_params("arbitrary"),
        name="moe_experts",
    )(tile_expert, n_used, xs, wgu, bgu, wd, bd, row_w)


def _final_kernel(x1_ref, yk_ref, nw_ref, o_ref):
    x = x1_ref[...] + ((yk_ref[0] + yk_ref[1]) + (yk_ref[2] + yk_ref[3]))
    var = jnp.mean(x * x, axis=-1, keepdims=True)
    o_ref[...] = x * lax.rsqrt(var + NORM_EPS) * nw_ref[...]


def _final_norm(x1, yk, norm_w, tm):
    t, d = x1.shape
    return pl.pallas_call(
        _final_kernel,
        grid=(t // tm,),
        in_specs=[
            pl.BlockSpec((tm, d), lambda i: (i, 0)),
            pl.BlockSpec((TOP_K, tm, d), lambda i: (0, i, 0)),
            pl.BlockSpec((1, d), lambda i: (0, 0)),
        ],
        out_specs=pl.BlockSpec((tm, d), lambda i: (i, 0)),
        out_shape=jax.ShapeDtypeStruct((t, d), F32),
        compiler_params=_params("parallel"),
        name="final_norm",
    )(x1, yk, norm_w.reshape(1, d))


def _dispatch_tables(top_i, top_w, n_rows):
    t = top_i.shape[0]
    n = t * TOP_K
    tm = MOE_TM
    e_flat = top_i.reshape(n)
    order = jnp.argsort(e_flat, stable=True)
    e_sorted = e_flat[order]
    counts = jnp.zeros((N_EXPERTS,), jnp.int32).at[e_flat].add(1)
    padded = ((counts + tm - 1) // tm) * tm
    pend = jnp.cumsum(padded)
    pstart = pend - padded
    start = jnp.cumsum(counts) - counts
    dest = pstart[e_sorted] + (jnp.arange(n, dtype=jnp.int32) - start[e_sorted])
    row_token = jnp.zeros((n_rows,), jnp.int32).at[dest].set(order // TOP_K)
    row_w = jnp.zeros((n_rows,), F32).at[dest].set(top_w.reshape(n)[order])
    pair_pos = jnp.zeros((n,), jnp.int32).at[order].set(dest)
    tile_start = jnp.arange(n_rows // tm, dtype=jnp.int32) * tm
    tile_expert = jnp.minimum(jnp.searchsorted(pend, tile_start, side="right"), N_EXPERTS - 1).astype(jnp.int32)
    n_used = (pend[-1] // tm).astype(jnp.int32).reshape(1)
    return row_token, row_w, pair_pos.reshape(t, TOP_K), tile_expert, n_used


def _in_proj_weights(w_in):
    o = np.cumsum([0, D_INNER, XBC_DIM, SSM_HEADS, 1024, 256, 256, 256, 256, 256, 256, 3 * NSA_HEADS, 2 * D_MODEL])
    z, xbc, dt, q, kv, gate, mg = (w_in[:, o[0]:o[1]], w_in[:, o[1]:o[2]], w_in[:, o[2]:o[3]],
                                   w_in[:, o[3]:o[4]], w_in[:, o[4]:o[10]], w_in[:, o[10]:o[11]],
                                   w_in[:, o[11]:o[12]])
    small_pad = jnp.zeros((D_MODEL, F_WIDTH - F_SMALL - SSM_HEADS - 3 * NSA_HEADS), w_in.dtype)
    w_f = jnp.concatenate([z, mg[:, :D_MODEL], xbc, mg[:, D_MODEL:], dt, gate, small_pad], axis=1)
    w_q = jnp.concatenate([q, kv], axis=1)
    return w_f.astype(BF16), w_q.astype(BF16)


def kernel(x, mix_norm_w, w_in, conv_w, conv_b, dt_bias, a_log, d_skip, ssm_norm_w, cmp_pos_k, cmp_w1_k, cmp_w2_k, cmp_pos_v, cmp_w1_v, cmp_w2_v, w_branch_ssm, w_branch_nsa, w_out, ffn_norm_w, router_w, router_b, w_gate_up, b_gate_up, w_down, b_down, final_norm_w):
    bsz, seq, d = x.shape
    t = bsz * seq
    x2 = x.reshape(t, d)
    depth = w_in.shape[0]
    for l in range(depth):
        w_f, w_q = _in_proj_weights(w_in[l])
        proj_f = _norm_matmul(x2, mix_norm_w[l], w_f, F32, 1024, 768)
        proj_q = _norm_matmul(x2, mix_norm_w[l], w_q, BF16, 1024, 512)

        small = proj_f[:, F_SMALL:F_SMALL + LANES]
        dt_t = jnp.transpose(small[:, :SSM_HEADS].reshape(bsz, seq, SSM_HEADS), (0, 2, 1))
        y_ssm = _ssd(proj_f, dt_t, conv_w[l], conv_b[l], dt_bias[l], a_log[l], d_skip[l], ssm_norm_w[l],
                     bsz, seq)

        kc = _compress(proj_q[:, 1024:1280], cmp_pos_k[l], cmp_w1_k[l], cmp_w2_k[l], bsz, seq)
        vc = _compress(proj_q[:, 1280:1536], cmp_pos_v[l], cmp_w1_v[l], cmp_w2_v[l], bsz, seq)
        gates = small[:, SSM_HEADS:SSM_HEADS + 3 * NSA_HEADS]
        y_nsa = _nsa(proj_q, gates, kc, vc, bsz, seq)

        x1, h, top_i, top_w = _merge_route(y_ssm, y_nsa, proj_f, x2, w_branch_ssm[l], w_branch_nsa[l],
                                           w_out[l], ffn_norm_w[l], router_w[l], router_b[l], 512)

        n_rows = t * TOP_K + N_EXPERTS * MOE_TM
        row_token, row_w, pair_pos, tile_expert, n_used = _dispatch_tables(top_i, top_w, n_rows)
        xs = jnp.take(h, row_token, axis=0)
        wgu = jnp.concatenate([w_gate_up[l][:, :, 0::2], w_gate_up[l][:, :, 1::2]], axis=-1).astype(BF16)
        bgu = jnp.concatenate([b_gate_up[l][:, 0::2], b_gate_up[l][:, 1::2]], axis=-1)[:, None, :]
        ys = _moe_experts(xs, row_w.reshape(n_rows, 1), tile_expert, n_used, wgu, bgu,
                          w_down[l].astype(BF16), b_down[l][:, None, :])
        yk = jnp.take(ys, pair_pos.T, axis=0)
        if l + 1 < depth:
            x2 = x1 + ((yk[0] + yk[1]) + (yk[2] + yk[3]))
    out = _final_norm(x1, yk, final_norm_w, 512)
    return out.reshape(bsz, seq, d)
```

```python
import numpy as np
import jax
import jax.numpy as jnp
from jax import lax
from jax.experimental import pallas as pl
from jax.experimental.pallas import tpu as pltpu

F32 = jnp.float32
BF16 = jnp.bfloat16
HIGHEST = lax.Precision.HIGHEST

D_MODEL = 1024
D_INNER = 1024
SSM_HEAD_DIM = 64
SSM_HEADS = 16
SSM_GROUPS = 4
D_STATE = 128
CONV_K = 4
XBC_DIM = D_INNER + 2 * SSM_GROUPS * D_STATE
SSD_CHUNK = 128
NSA_HEAD_DIM = 64
NSA_HEADS = 16
NSA_GROUPS = 4
NSA_REP = 4
NSA_KV_WIDTH = 256
CMP_BLOCK = 32
CMP_STRIDE = 16
CMP_HIDDEN = 128
SEL_BLOCK = 64
SEL_TOPN = 8
WINDOW = 256
N_EXPERTS = 32
TOP_K = 4
SWIGLU_ALPHA = 1.702
SWIGLU_LIMIT = 7.0
NORM_EPS = 1e-5

LANES = 128
VMEM_LIMIT = 56 * 1024 * 1024

F_Z, F_MS, F_XBC, F_MN, F_SMALL = 0, 1024, 2048, 4096, 5120
F_WIDTH = 5376
Q_WIDTH = 1024 + 6 * NSA_KV_WIDTH


def _sigmoid(x):
    return 1.0 / (1.0 + jnp.exp(-x))


def _silu(x):
    return x * _sigmoid(x)


def _softplus(x):
    return jnp.maximum(x, 0.0) + jnp.log1p(jnp.exp(-jnp.abs(x)))


def _params(*sem):
    return pltpu.CompilerParams(dimension_semantics=sem, vmem_limit_bytes=VMEM_LIMIT)


def _norm_matmul_kernel(x_ref, nw_ref, w_ref, o_ref, h_ref):
    @pl.when(pl.program_id(1) == 0)
    def _():
        x = x_ref[...]
        var = jnp.mean(x * x, axis=-1, keepdims=True)
        h_ref[...] = (x * lax.rsqrt(var + NORM_EPS) * nw_ref[...]).astype(BF16)

    o_ref[...] = jnp.dot(h_ref[...], w_ref[...], preferred_element_type=F32).astype(o_ref.dtype)


def _norm_matmul(x2, norm_w, w, out_dtype, tm, tn):
    t, d = x2.shape
    n = w.shape[1]
    return pl.pallas_call(
        _norm_matmul_kernel,
        grid=(t // tm, n // tn),
        in_specs=[
            pl.BlockSpec((tm, d), lambda i, j: (i, 0)),
            pl.BlockSpec((1, d), lambda i, j: (0, 0)),
            pl.BlockSpec((d, tn), lambda i, j: (0, j)),
        ],
        out_specs=pl.BlockSpec((tm, tn), lambda i, j: (i, j)),
        out_shape=jax.ShapeDtypeStruct((t, n), out_dtype),
        scratch_shapes=[pltpu.VMEM((tm, d), BF16)],
        compiler_params=_params("parallel", "arbitrary"),
        name="norm_matmul",
    )(x2, norm_w.reshape(1, d), w)


def _ssd_kernel(z_ref, xbc_ref, sm_ref, dtt_ref, convw_ref, convb_ref, dtb_ref, dtbt_ref,
                alog_ref, alogt_ref, dskip_ref, nw_ref, expand_ref, y_ref,
                xbuf, state, ydiag):
    L = SSD_CHUNK
    c = pl.program_id(1)

    @pl.when(c == 0)
    def _():
        xbuf[0:8, :] = jnp.zeros((8, XBC_DIM), F32)
        state[...] = jnp.zeros(state.shape, F32)

    xbuf[8:8 + L, :] = xbc_ref[...]
    acc = convb_ref[...] + convw_ref[0:1, :] * xbuf[5:5 + L, :]
    for k in range(1, CONV_K):
        acc = acc + convw_ref[k:k + 1, :] * xbuf[5 + k:5 + k + L, :]
    xbuf[0:8, :] = xbuf[L:L + 8, :]
    xbc = _silu(acc)
    xs = xbc[:, :D_INNER]

    lane = lax.broadcasted_iota(jnp.int32, (1, LANES), 1)
    a_row = jnp.where(lane < SSM_HEADS, -jnp.exp(alog_ref[...]), 0.0)
    dt = _softplus(sm_ref[...] + dtb_ref[...])
    a_dt = dt * a_row
    dt_t = _softplus(dtt_ref[0] + dtbt_ref[...])
    a_dt_t = dt_t * (-jnp.exp(alogt_ref[...]))

    row = lax.broadcasted_iota(jnp.int32, (L, L), 0)
    col = lax.broadcasted_iota(jnp.int32, (L, L), 1)
    lower = row >= col
    tri = lower.astype(F32)
    tri_t = (row <= col).astype(F32)
    cs = jnp.dot(tri, a_dt, precision=HIGHEST, preferred_element_type=F32)
    cs_t = jnp.dot(a_dt_t, tri_t, precision=HIGHEST, preferred_element_type=F32)
    cs_last = cs[L - 1:L, :]

    stacked = jnp.concatenate([dt, jnp.exp(cs_last - cs), jnp.exp(cs)], axis=0)
    wide = jnp.dot(stacked, expand_ref[...], precision=HIGHEST, preferred_element_type=F32)
    dt_x = wide[0:L]
    dte_x = wide[L:2 * L]
    ecs_x = wide[2 * L:3 * L]

    xdt = xs * dt_x
    xw = (xdt * dte_x).astype(BF16)
    xdt_b = xdt.astype(BF16)

    hpg = SSM_HEADS // SSM_GROUPS
    gw = hpg * SSM_HEAD_DIM
    y_off_parts = []
    for g in range(SSM_GROUPS):
        b_g = xbc[:, D_INNER + g * D_STATE:D_INNER + (g + 1) * D_STATE].astype(BF16)
        c_g = xbc[:, D_INNER + SSM_GROUPS * D_STATE + g * D_STATE:
                  D_INNER + SSM_GROUPS * D_STATE + (g + 1) * D_STATE].astype(BF16)
        cb = lax.dot_general(c_g, b_g, (((1,), (1,)), ((), ())), preferred_element_type=F32)
        for hh in range(hpg):
            h = g * hpg + hh
            seg = cs[:, h:h + 1] - cs_t[h:h + 1, :]
            decay = jnp.exp(jnp.where(lower, seg, -jnp.inf))
            m = (cb * decay).astype(BF16)
            ydiag[:, h * SSM_HEAD_DIM:(h + 1) * SSM_HEAD_DIM] = jnp.dot(
                m, xdt_b[:, h * SSM_HEAD_DIM:(h + 1) * SSM_HEAD_DIM], preferred_element_type=F32)
        st_prev = state[g]
        y_off_parts.append(jnp.dot(c_g, st_prev.astype(BF16), preferred_element_type=F32))
        st_new = lax.dot_general(b_g, xw[:, g * gw:(g + 1) * gw], (((0,), (0,)), ((), ())),
                                 preferred_element_type=F32)
        state[g] = st_prev * ecs_x[L - 1:L, g * gw:(g + 1) * gw] + st_new
    y_off = jnp.concatenate(y_off_parts, axis=1) * ecs_x

    y = (ydiag[...] + y_off + dskip_ref[...] * xs) * _silu(z_ref[...])
    for g in range(SSM_GROUPS):
        yg = y[:, g * gw:(g + 1) * gw]
        yg = yg * lax.rsqrt(jnp.mean(yg * yg, axis=-1, keepdims=True) + NORM_EPS)
        y_ref[:, g * gw:(g + 1) * gw] = (yg * nw_ref[:, g * gw:(g + 1) * gw]).astype(y_ref.dtype)


def _ssd(proj_f, dt_t, conv_w, conv_b, dt_bias, a_log, d_skip, norm_w, bsz, seq):
    L = SSD_CHUNK
    nc = seq // L
    pad = LANES - SSM_HEADS
    dtb = jnp.pad(dt_bias, (0, pad)).reshape(1, LANES)
    alog = jnp.pad(a_log, (0, pad)).reshape(1, LANES)
    dskip_x = jnp.repeat(d_skip, SSM_HEAD_DIM).reshape(1, D_INNER)
    expand = (np.arange(LANES)[:, None] == (np.arange(D_INNER)[None, :] // SSM_HEAD_DIM)).astype(np.float32)
    const = lambda shape: pl.BlockSpec(shape, lambda b, c: (0,) * len(shape))
    return pl.pallas_call(
        _ssd_kernel,
        grid=(bsz, nc),
        in_specs=[
            pl.BlockSpec((L, D_INNER), lambda b, c: (b * nc + c, F_Z // D_INNER)),
            pl.BlockSpec((L, XBC_DIM), lambda b, c: (b * nc + c, F_XBC // XBC_DIM)),
            pl.BlockSpec((L, LANES), lambda b, c: (b * nc + c, F_SMALL // LANES)),
            pl.BlockSpec((1, SSM_HEADS, L), lambda b, c: (b, 0, c)),
            const((CONV_K, XBC_DIM)), const((1, XBC_DIM)),
            const((1, LANES)), const((SSM_HEADS, 1)),
            const((1, LANES)), const((SSM_HEADS, 1)),
            const((1, D_INNER)), const((1, D_INNER)),
            const((LANES, D_INNER)),
        ],
        out_specs=pl.BlockSpec((L, D_INNER), lambda b, c: (b * nc + c, 0)),
        out_shape=jax.ShapeDtypeStruct((bsz * seq, D_INNER), BF16),
        scratch_shapes=[
            pltpu.VMEM((L + 8, XBC_DIM), F32),
            pltpu.VMEM((SSM_GROUPS, D_STATE, (SSM_HEADS // SSM_GROUPS) * SSM_HEAD_DIM), F32),
            pltpu.VMEM((L, D_INNER), F32),
        ],
        compiler_params=_params("parallel", "arbitrary"),
        name="ssd",
    )(proj_f, proj_f, proj_f, dt_t, conv_w, conv_b.reshape(1, XBC_DIM),
      dtb, dt_bias.reshape(SSM_HEADS, 1), alog, a_log.reshape(SSM_HEADS, 1),
      dskip_x, norm_w.reshape(1, D_INNER), jnp.asarray(expand))


def _gelu_tanh(x):
    return 0.5 * x * (1.0 + jnp.tanh(np.sqrt(2.0 / np.pi) * (x + 0.044715 * (x * x * x))))


def _compress_kernel(cur_ref, nxt_ref, pos_ref, w1_ref, w2_ref, o_ref):
    half = CMP_STRIDE * NSA_HEAD_DIM
    lo = (cur_ref[0, 0].astype(F32) + pos_ref[0:1, :]).astype(BF16)
    hi = (nxt_ref[0, 0].astype(F32) + pos_ref[1:2, :]).astype(BF16)
    hid = (jnp.dot(lo, w1_ref[0:half, :], preferred_element_type=F32)
           + jnp.dot(hi, w1_ref[half:2 * half, :], preferred_element_type=F32))
    o_ref[0, 0] = jnp.dot(_gelu_tanh(hid).astype(BF16), w2_ref[...],
                          preferred_element_type=F32).astype(o_ref.dtype)


def _compress(kv, pos, w1, w2, bsz, seq):
    nch = seq // CMP_STRIDE
    half = CMP_STRIDE * NSA_HEAD_DIM
    ch = kv.reshape(bsz, nch, CMP_STRIDE, NSA_GROUPS, NSA_HEAD_DIM)
    ch = jnp.transpose(ch, (0, 3, 1, 2, 4)).reshape(bsz, NSA_GROUPS, nch, half)
    nxt = jnp.concatenate([ch[:, :, 1:], jnp.zeros_like(ch[:, :, :1])], axis=2)
    blk = pl.BlockSpec((1, 1, nch, half), lambda b, g: (b, g, 0, 0))
    return pl.pallas_call(
        _compress_kernel,
        grid=(bsz, NSA_GROUPS),
        in_specs=[
            blk, blk,
            pl.BlockSpec((2, half), lambda b, g: (0, 0)),
            pl.BlockSpec((2 * half, CMP_HIDDEN), lambda b, g: (0, 0)),
            pl.BlockSpec((CMP_HIDDEN, NSA_HEAD_DIM), lambda b, g: (0, 0)),
        ],
        out_specs=pl.BlockSpec((1, 1, nch, NSA_HEAD_DIM), lambda b, g: (b, g, 0, 0)),
        out_shape=jax.ShapeDtypeStruct((bsz, NSA_GROUPS, nch, NSA_HEAD_DIM), BF16),
        compiler_params=_params("parallel", "parallel"),
        name="compress",
    )(ch, nxt, pos.reshape(2, half), w1.astype(BF16), w2.astype(BF16))


NSA_TQ = 256
NSA_KC = 512
MASK_VALUE = -1e30
SOFTMAX_M0 = -1e20


def _nsa_kernel(q_ref, kc_ref, vc_ref, ks_ref, vs_ref, kw_ref, vw_ref, gate_ref,
                overlap_ref, expand_ref, o_ref, selx_ref):
    R, TQ, KC, DH = NSA_REP, NSA_TQ, NSA_KC, NSA_HEAD_DIM
    i = pl.program_id(2)
    q0 = i * TQ
    nt = (((1,), (1,)), ((), ()))
    tcol = q0 + lax.broadcasted_iota(jnp.int32, (TQ, 1), 0)
    qblk = q_ref[...]
    scale = DH ** -0.5
    qh = [(qblk[:, r * DH:(r + 1) * DH].astype(F32) * scale).astype(BF16) for r in range(R)]
    v_width = vs_ref.shape[3]

    def flash_step(m_i, acc, q, k, v_ext, msk):
        s = jnp.where(msk, lax.dot_general(q, k, nt, preferred_element_type=F32), MASK_VALUE)
        m_new = jnp.maximum(m_i, jnp.max(s, axis=-1, keepdims=True))
        p = jnp.exp(s - m_new)
        acc = acc * jnp.exp(m_i - m_new) + jnp.dot(p.astype(BF16), v_ext, preferred_element_type=F32)
        return m_new, acc

    def init():
        return tuple((jnp.full((TQ, 1), SOFTMAX_M0, F32), jnp.zeros((TQ, v_width), F32)) for _ in range(R))

    def finish(carry):
        return [acc[:, :DH] / jnp.maximum(acc[:, DH:DH + 1], 1e-30) for _, acc in carry]

    w0 = pl.multiple_of(jnp.maximum(q0 - WINDOW, 0), TQ)
    kpos_w = w0 + lax.broadcasted_iota(jnp.int32, (1, WINDOW + TQ), 1)
    msk_w = (kpos_w <= tcol) & (kpos_w > tcol - WINDOW)
    k_w = kw_ref[0, 0, pl.ds(w0, WINDOW + TQ), :]
    v_w = vw_ref[0, 0, pl.ds(w0, WINDOW + TQ), :]
    carry_w = init()
    o_win = finish(tuple(flash_step(carry_w[r][0], carry_w[r][1], qh[r], k_w, v_w, msk_w) for r in range(R)))

    kc = kc_ref[0, 0]
    vc = vc_ref[0, 0]
    ncmp = kc.shape[0]
    cmp_end = lax.broadcasted_iota(jnp.int32, (1, ncmp), 1) * CMP_STRIDE + (CMP_BLOCK - 1)
    cmask = cmp_end <= tcol
    o_cmp = []
    p4 = None
    for r in range(R):
        lg = jnp.where(cmask, lax.dot_general(qh[r], kc, nt, preferred_element_type=F32), -jnp.inf)
        mx = jnp.max(lg, axis=-1, keepdims=True)
        mx = jnp.where(mx > -jnp.inf, mx, 0.0)
        e = jnp.where(cmask, jnp.exp(lg - mx), 0.0)
        p = e / jnp.maximum(jnp.sum(e, axis=-1, keepdims=True), 1e-30)
        o_cmp.append(jnp.dot(p.astype(BF16), vc, preferred_element_type=F32))
        p4 = p if p4 is None else p4 + p

    imp = lax.dot_general(overlap_ref[...], p4, nt, precision=HIGHEST, preferred_element_type=F32)
    nsel = imp.shape[0]
    trow = q0 + lax.broadcasted_iota(jnp.int32, (1, TQ), 1)
    cur = trow // SEL_BLOCK
    jj = lax.broadcasted_iota(jnp.int32, (nsel, 1), 0)
    forced = (jj == 0) | (jj == cur) | (jj == cur - 1)
    val = jnp.where(forced, jnp.inf, jnp.where(jj <= cur, imp, -jnp.inf))
    rank = jnp.zeros((nsel, TQ), jnp.int32)
    for b in range(nsel):
        vb = val[b:b + 1, :]
        beats = (vb > val) | ((vb == val) & (jj > b))
        rank = rank + beats.astype(jnp.int32)
    sel = (rank < SEL_TOPN).astype(BF16)
    selx = lax.dot_general(sel, expand_ref[...], (((0,), (0,)), ((), ())),
                           preferred_element_type=F32)
    for kb in range(selx_ref.shape[0]):
        selx_ref[kb] = selx[:, kb * KC:(kb + 1) * KC]

    def sel_body(kb, carry):
        k0 = pl.multiple_of(kb * KC, KC)
        kpos = k0 + lax.broadcasted_iota(jnp.int32, (1, KC), 1)
        msk = (selx_ref[kb] > 0.5) & (kpos <= tcol)
        k = ks_ref[0, 0, pl.ds(k0, KC), :]
        v = vs_ref[0, 0, pl.ds(k0, KC), :]
        return tuple(flash_step(carry[r][0], carry[r][1], qh[r], k, v, msk) for r in range(R))

    o_sel = finish(lax.fori_loop(0, (q0 + TQ + KC - 1) // KC, sel_body, init()))

    g = _sigmoid(gate_ref[0, 0])
    outs = [g[:, 3 * r:3 * r + 1] * o_cmp[r] + g[:, 3 * r + 1:3 * r + 2] * o_sel[r]
            + g[:, 3 * r + 2:3 * r + 3] * o_win[r] for r in range(R)]
    o_ref[...] = jnp.concatenate(outs, axis=1).astype(o_ref.dtype)


def _nsa(proj_q, gates, kc, vc, bsz, seq):
    G, R, DH, TQ = NSA_GROUPS, NSA_REP, NSA_HEAD_DIM, NSA_TQ
    n_cmp_pad = seq // CMP_STRIDE
    n_sel = seq // SEL_BLOCK
    nq = seq // TQ

    def kv(idx, ones_col):
        lo = 1024 + idx * NSA_KV_WIDTH
        a = jnp.transpose(proj_q[:, lo:lo + NSA_KV_WIDTH].reshape(bsz, seq, G, DH), (0, 2, 1, 3))
        if ones_col:
            ext = jnp.zeros((bsz, G, seq, LANES - DH), BF16).at[..., 0].set(1.0)
            a = jnp.concatenate([a, ext], axis=-1)
        return a

    ks, vs, kw, vw = kv(2, False), kv(3, True), kv(4, False), kv(5, True)
    gt = jnp.transpose(gates.reshape(bsz, seq, G, R * 3), (0, 2, 1, 3))

    c_start = np.arange(n_cmp_pad) * CMP_STRIDE
    s_start = np.arange(n_sel) * SEL_BLOCK
    overlap = ((c_start[:, None] < s_start[None, :] + SEL_BLOCK)
               & (c_start[:, None] + CMP_BLOCK > s_start[None, :])).astype(np.float32)
    overlap[(seq - CMP_BLOCK) // CMP_STRIDE + 1:] = 0.0
    expand = (np.arange(n_sel)[:, None] == (np.arange(seq)[None, :] // SEL_BLOCK)).astype(np.float32)

    qblk = pl.BlockSpec((TQ, R * DH), lambda b, g, i: (b * nq + i, g))
    cblk = pl.BlockSpec((1, 1, n_cmp_pad, DH), lambda b, g, i: (b, g, 0, 0))
    kblk = pl.BlockSpec((1, 1, seq, DH), lambda b, g, i: (b, g, 0, 0))
    vblk = pl.BlockSpec((1, 1, seq, LANES), lambda b, g, i: (b, g, 0, 0))
    return pl.pallas_call(
        _nsa_kernel,
        grid=(bsz, G, nq),
        in_specs=[
            qblk, cblk, cblk, kblk, vblk, kblk, vblk,
            pl.BlockSpec((1, 1, TQ, R * 3), lambda b, g, i: (b, g, i, 0)),
            pl.BlockSpec((n_sel, n_cmp_pad), lambda b, g, i: (0, 0)),
            pl.BlockSpec((n_sel, seq), lambda b, g, i: (0, 0)),
        ],
        out_specs=qblk,
        out_shape=jax.ShapeDtypeStruct((bsz * seq, G * R * DH), BF16),
        scratch_shapes=[pltpu.VMEM((seq // NSA_KC, TQ, NSA_KC), F32)],
        compiler_params=_params("parallel", "parallel", "arbitrary"),
        name="nsa_attention",
    )(proj_q, kc, vc, ks, vs, kw, vw, gt, jnp.asarray(overlap.T), jnp.asarray(expand, dtype=BF16))


def _merge_kernel(ys_ref, yn_ref, gs_ref, gn_ref, x_ref, wbs_ref, wbn_ref, wo_ref, fw_ref,
                  rw_ref, rb_ref, tri_ref, x1_ref, h_ref, ti_ref, tw_ref, slot_ref, cnt_ref, base_ref):
    @pl.when(pl.program_id(0) == 0)
    def _():
        base_ref[...] = jnp.zeros(base_ref.shape, F32)

    a = jnp.dot(ys_ref[...], wbs_ref[...], preferred_element_type=F32)
    b = jnp.dot(yn_ref[...], wbn_ref[...], preferred_element_type=F32)
    merged = _sigmoid(gs_ref[...]) * a + _sigmoid(gn_ref[...]) * b
    x1 = x_ref[...] + jnp.dot(merged.astype(BF16), wo_ref[...], preferred_element_type=F32)
    x1_ref[...] = x1
    var = jnp.mean(x1 * x1, axis=-1, keepdims=True)
    h = x1 * lax.rsqrt(var + NORM_EPS) * fw_ref[...]
    h_ref[...] = h.astype(h_ref.dtype)

    logits = jnp.dot(h, rw_ref[...], precision=HIGHEST, preferred_element_type=F32) + rb_ref[...]
    ne = logits.shape[1]
    jj = lax.broadcasted_iota(jnp.int32, (1, ne), 1)
    rank = jnp.zeros(logits.shape, jnp.int32)
    for e in range(ne):
        ce = logits[:, e:e + 1]
        beats = (ce > logits) | ((ce == logits) & (jj > e))
        rank = rank + beats.astype(jnp.int32)
    sel = rank < TOP_K
    mx = jnp.max(logits, axis=-1, keepdims=True)
    p = jnp.where(sel, jnp.exp(logits - mx), 0.0)
    p = p / jnp.sum(p, axis=-1, keepdims=True)

    self01 = sel.astype(F32)
    before = jnp.dot(tri_ref[...], self01.astype(BF16), preferred_element_type=F32) + base_ref[...]
    for k in range(TOP_K):
        hit = rank == k
        ti_ref[:, k:k + 1] = jnp.sum(jnp.where(hit, jj, 0), axis=-1, keepdims=True)
        tw_ref[:, k:k + 1] = jnp.sum(jnp.where(hit, p, 0.0), axis=-1, keepdims=True)
        slot_ref[:, k:k + 1] = jnp.sum(jnp.where(hit, before, 0.0), axis=-1, keepdims=True).astype(jnp.int32)
    base_ref[...] = base_ref[...] + jnp.sum(self01, axis=0, keepdims=True)
    cnt_ref[...] = base_ref[...]


def _merge_route(y_ssm, y_nsa, proj_f, x2, wbs, wbn, wo, ffn_w, router_w, router_b, tm):
    t, d = x2.shape
    row = lambda col: pl.BlockSpec((tm, d), lambda i: (i, col))
    const = lambda shape: pl.BlockSpec(shape, lambda i: (0,) * len(shape))
    kblk = pl.BlockSpec((tm, TOP_K), lambda i: (i, 0))
    tri = np.tril(np.ones((tm, tm), np.float32), -1)
    return pl.pallas_call(
        _merge_kernel,
        grid=(t // tm,),
        in_specs=[
            row(0), row(0), row(F_MS // d), row(F_MN // d), row(0),
            const((d, d)), const((d, d)), const((d, d)), const((1, d)),
            const((d, N_EXPERTS)), const((1, N_EXPERTS)), const((tm, tm)),
        ],
        out_specs=[row(0), row(0), kblk, kblk, kblk, const((1, N_EXPERTS))],
        out_shape=[jax.ShapeDtypeStruct((t, d), F32), jax.ShapeDtypeStruct((t, d), BF16),
                   jax.ShapeDtypeStruct((t, TOP_K), jnp.int32), jax.ShapeDtypeStruct((t, TOP_K), F32),
                   jax.ShapeDtypeStruct((t, TOP_K), jnp.int32), jax.ShapeDtypeStruct((1, N_EXPERTS), F32)],
        scratch_shapes=[pltpu.VMEM((1, N_EXPERTS), F32)],
        compiler_params=_params("arbitrary"),
        name="merge_route",
    )(y_ssm, y_nsa, proj_f, proj_f, x2, wbs.astype(BF16), wbn.astype(BF16), wo.astype(BF16),
      ffn_w.reshape(1, d), router_w, router_b.reshape(1, N_EXPERTS), jnp.asarray(tri, dtype=BF16))


GU_CHUNK = 2 * LANES


def _deinterleave_kernel(w_ref, perm_ref, o_ref):
    o_ref[0] = jnp.dot(w_ref[0].astype(BF16), perm_ref[...], preferred_element_type=F32).astype(o_ref.dtype)


def _deinterleave(w):
    e, d, n = w.shape
    perm = np.zeros((GU_CHUNK, GU_CHUNK), np.float32)
    perm[2 * np.arange(LANES), np.arange(LANES)] = 1.0
    perm[2 * np.arange(LANES) + 1, LANES + np.arange(LANES)] = 1.0
    blk = pl.BlockSpec((1, d, GU_CHUNK), lambda i, j: (i, 0, j))
    return pl.pallas_call(
        _deinterleave_kernel,
        grid=(e, n // GU_CHUNK),
        in_specs=[blk, pl.BlockSpec((GU_CHUNK, GU_CHUNK), lambda i, j: (0, 0))],
        out_specs=blk,
        out_shape=jax.ShapeDtypeStruct((e, d, n), BF16),
        compiler_params=_params("parallel", "parallel"),
        name="deinterleave",
    )(w, jnp.asarray(perm, dtype=BF16))


MOE_TM = 256


def _moe_kernel(te_ref, nu_ref, xs_ref, wgu_ref, bgu_ref, wd_ref, bd_ref, o_ref):
    i = pl.program_id(0)

    @pl.when(i < nu_ref[0])
    def _():
        h1 = jnp.dot(xs_ref[...], wgu_ref[0], preferred_element_type=F32) + bgu_ref[0]
        nchunk = h1.shape[1] // GU_CHUNK
        glu = jnp.concatenate([h1[:, c * GU_CHUNK:c * GU_CHUNK + LANES] for c in range(nchunk)], axis=1)
        lin = jnp.concatenate([h1[:, c * GU_CHUNK + LANES:(c + 1) * GU_CHUNK] for c in range(nchunk)], axis=1)
        glu = jnp.minimum(glu, SWIGLU_LIMIT)
        lin = jnp.clip(lin, -SWIGLU_LIMIT, SWIGLU_LIMIT)
        act = glu * _sigmoid(SWIGLU_ALPHA * glu) * (lin + 1.0)
        o_ref[...] = jnp.dot(act.astype(BF16), wd_ref[0], preferred_element_type=F32) + bd_ref[0]

    @pl.when(i >= nu_ref[0])
    def _():
        o_ref[...] = jnp.zeros(o_ref.shape, o_ref.dtype)


def _moe_experts(xs, tile_expert, n_used, wgu, bgu, wd, bd):
    p, d = xs.shape
    tm = MOE_TM
    grid_spec = pltpu.PrefetchScalarGridSpec(
        num_scalar_prefetch=2,
        grid=(p // tm,),
        in_specs=[
            pl.BlockSpec((tm, d), lambda i, te, nu: (i, 0)),
            pl.BlockSpec((1, d, 2 * d), lambda i, te, nu: (te[i], 0, 0)),
            pl.BlockSpec((1, 1, 2 * d), lambda i, te, nu: (te[i], 0, 0)),
            pl.BlockSpec((1, d, d), lambda i, te, nu: (te[i], 0, 0)),
            pl.BlockSpec((1, 1, d), lambda i, te, nu: (te[i], 0, 0)),
        ],
        out_specs=pl.BlockSpec((tm, d), lambda i, te, nu: (i, 0)),
    )
    return pl.pallas_call(
        _moe_kernel,
        grid_spec=grid_spec,
        out_shape=jax.ShapeDtypeStruct((p, d), F32),
        compiler_params=_params("arbitrary"),
        name="moe_experts",
    )(tile_expert, n_used, xs, wgu, bgu, wd, bd)


def _final_kernel(x1_ref, yk_ref, tw_ref, nw_ref, o_ref):
    tw = tw_ref[...]
    moe = ((tw[:, 0:1] * yk_ref[0] + tw[:, 1:2] * yk_ref[1])
           + (tw[:, 2:3] * yk_ref[2] + tw[:, 3:4] * yk_ref[3]))
    x = x1_ref[...] + moe
    var = jnp.mean(x * x, axis=-1, keepdims=True)
    o_ref[...] = x * lax.rsqrt(var + NORM_EPS) * nw_ref[...]


def _final_norm(x1, yk, top_w, norm_w, tm):
    t, d = x1.shape
    return pl.pallas_call(
        _final_kernel,
        grid=(t // tm,),
        in_specs=[
            pl.BlockSpec((tm, d), lambda i: (i, 0)),
            pl.BlockSpec((TOP_K, tm, d), lambda i: (0, i, 0)),
            pl.BlockSpec((tm, TOP_K), lambda i: (i, 0)),
            pl.BlockSpec((1, d), lambda i: (0, 0)),
        ],
        out_specs=pl.BlockSpec((tm, d), lambda i: (i, 0)),
        out_shape=jax.ShapeDtypeStruct((t, d), F32),
        compiler_params=_params("parallel"),
        name="final_norm",
    )(x1, yk, top_w, norm_w.reshape(1, d))


def _dispatch_tables(top_i, slot, counts, n_rows):
    t = top_i.shape[0]
    tm = MOE_TM
    padded = ((counts + tm - 1) // tm) * tm
    pend = jnp.cumsum(padded)
    pstart = pend - padded
    onehot = top_i[:, :, None] == jnp.arange(N_EXPERTS, dtype=jnp.int32)
    dest = slot + jnp.sum(jnp.where(onehot, pstart, 0), axis=-1)
    pair_token = jnp.arange(t * TOP_K, dtype=jnp.int32) // TOP_K
    row_token = jnp.zeros((n_rows,), jnp.int32).at[dest.reshape(-1)].set(pair_token)
    tile_start = jnp.arange(n_rows // tm, dtype=jnp.int32) * tm
    tile_expert = jnp.minimum(jnp.sum(tile_start[:, None] >= pend[None, :], axis=-1), N_EXPERTS - 1)
    n_used = (pend[-1] // tm).reshape(1)
    return row_token, dest, tile_expert.astype(jnp.int32), n_used.astype(jnp.int32)


def _in_proj_weights(w_in):
    o = np.cumsum([0, D_INNER, XBC_DIM, SSM_HEADS, 1024, 256, 256, 256, 256, 256, 256, 3 * NSA_HEADS, 2 * D_MODEL])
    z, xbc, dt, q, kv, gate, mg = (w_in[:, o[0]:o[1]], w_in[:, o[1]:o[2]], w_in[:, o[2]:o[3]],
                                   w_in[:, o[3]:o[4]], w_in[:, o[4]:o[10]], w_in[:, o[10]:o[11]],
                                   w_in[:, o[11]:o[12]])
    small_pad = jnp.zeros((D_MODEL, F_WIDTH - F_SMALL - SSM_HEADS - 3 * NSA_HEADS), w_in.dtype)
    w_f = jnp.concatenate([z, mg[:, :D_MODEL], xbc, mg[:, D_MODEL:], dt, gate, small_pad], axis=1)
    w_q = jnp.concatenate([q, kv], axis=1)
    return w_f.astype(BF16), w_q.astype(BF16)


def kernel(x, mix_norm_w, w_in, conv_w, conv_b, dt_bias, a_log, d_skip, ssm_norm_w, cmp_pos_k, cmp_w1_k, cmp_w2_k, cmp_pos_v, cmp_w1_v, cmp_w2_v, w_branch_ssm, w_branch_nsa, w_out, ffn_norm_w, router_w, router_b, w_gate_up, b_gate_up, w_down, b_down, final_norm_w):
    bsz, seq, d = x.shape
    t = bsz * seq
    x2 = x.reshape(t, d)
    depth = w_in.shape[0]
    for l in range(depth):
        w_f, w_q = _in_proj_weights(w_in[l])
        proj_f = _norm_matmul(x2, mix_norm_w[l], w_f, F32, 1024, 768)
        proj_q = _norm_matmul(x2, mix_norm_w[l], w_q, BF16, 1024, 512)

        small = proj_f[:, F_SMALL:F_SMALL + LANES]
        dt_t = jnp.transpose(small[:, :SSM_HEADS].reshape(bsz, seq, SSM_HEADS), (0, 2, 1))
        y_ssm = _ssd(proj_f, dt_t, conv_w[l], conv_b[l], dt_bias[l], a_log[l], d_skip[l], ssm_norm_w[l],
                     bsz, seq)

        kc = _compress(proj_q[:, 1024:1280], cmp_pos_k[l], cmp_w1_k[l], cmp_w2_k[l], bsz, seq)
        vc = _compress(proj_q[:, 1280:1536], cmp_pos_v[l], cmp_w1_v[l], cmp_w2_v[l], bsz, seq)
        gates = small[:, SSM_HEADS:SSM_HEADS + 3 * NSA_HEADS]
        y_nsa = _nsa(proj_q, gates, kc, vc, bsz, seq)

        x1, h, top_i, top_w, slot, counts = _merge_route(
            y_ssm, y_nsa, proj_f, x2, w_branch_ssm[l], w_branch_nsa[l], w_out[l], ffn_norm_w[l],
            router_w[l], router_b[l], 512)

        n_rows = t * TOP_K + N_EXPERTS * MOE_TM
        row_token, dest, tile_expert, n_used = _dispatch_tables(top_i, slot, counts[0].astype(jnp.int32), n_rows)
        xs = jnp.take(h, row_token, axis=0)
        wgu = _deinterleave(w_gate_up[l])
        nchunk = 2 * d // GU_CHUNK
        bgu = jnp.transpose(b_gate_up[l].reshape(N_EXPERTS, nchunk, LANES, 2), (0, 1, 3, 2))
        ys = _moe_experts(xs, tile_expert, n_used, wgu, bgu.reshape(N_EXPERTS, 1, 2 * d),
                          w_down[l].astype(BF16), b_down[l][:, None, :])
        yk = jnp.take(ys, dest.T, axis=0)
        if l + 1 < depth:
            x2 = x1 + jnp.einsum("tk,ktd->td", top_w, yk)
    out = _final_norm(x1, yk, top_w, final_norm_w, 512)
    return out.reshape(bsz, seq, d)
```

```python
import functools

import numpy as np
import jax
import jax.numpy as jnp
from jax import lax
from jax.experimental import pallas as pl
from jax.experimental.pallas import tpu as pltpu

F32 = jnp.float32
BF16 = jnp.bfloat16
HIGHEST = lax.Precision.HIGHEST

D_MODEL = 1024
D_INNER = 1024
SSM_HEAD_DIM = 64
SSM_HEADS = 16
SSM_GROUPS = 4
D_STATE = 128
CONV_K = 4
XBC_DIM = D_INNER + 2 * SSM_GROUPS * D_STATE
SSD_CHUNK = 128
NSA_HEAD_DIM = 64
NSA_HEADS = 16
NSA_GROUPS = 4
NSA_REP = 4
NSA_KV_WIDTH = 256
CMP_BLOCK = 32
CMP_STRIDE = 16
CMP_HIDDEN = 128
SEL_BLOCK = 64
SEL_TOPN = 8
WINDOW = 256
N_EXPERTS = 32
TOP_K = 4
SWIGLU_ALPHA = 1.702
SWIGLU_LIMIT = 7.0
NORM_EPS = 1e-5

LANES = 128
VMEM_LIMIT = 56 * 1024 * 1024

F_Z, F_MS, F_XBC, F_MN, F_SMALL = 0, 1024, 2048, 4096, 5120
F_WIDTH = 5376
Q_WIDTH = 1024 + 6 * NSA_KV_WIDTH


def _sigmoid(x):
    return 1.0 / (1.0 + jnp.exp(-x))


def _silu(x):
    return x * _sigmoid(x)


def _softplus(x):
    return jnp.maximum(x, 0.0) + jnp.log1p(jnp.exp(-jnp.abs(x)))


def _params(*sem):
    return pltpu.CompilerParams(dimension_semantics=sem, vmem_limit_bytes=VMEM_LIMIT)


def _norm_matmul_kernel(x_ref, nw_ref, w_ref, o_ref, h_ref):
    @pl.when(pl.program_id(1) == 0)
    def _():
        x = x_ref[...]
        var = jnp.mean(x * x, axis=-1, keepdims=True)
        h_ref[...] = (x * lax.rsqrt(var + NORM_EPS) * nw_ref[...]).astype(BF16)

    o_ref[...] = jnp.dot(h_ref[...], w_ref[...], preferred_element_type=F32).astype(o_ref.dtype)


def _norm_matmul(x2, norm_w, w, out_dtype, tm, tn):
    t, d = x2.shape
    n = w.shape[1]
    return pl.pallas_call(
        _norm_matmul_kernel,
        grid=(t // tm, n // tn),
        in_specs=[
            pl.BlockSpec((tm, d), lambda i, j: (i, 0)),
            pl.BlockSpec((1, d), lambda i, j: (0, 0)),
            pl.BlockSpec((d, tn), lambda i, j: (0, j)),
        ],
        out_specs=pl.BlockSpec((tm, tn), lambda i, j: (i, j)),
        out_shape=jax.ShapeDtypeStruct((t, n), out_dtype),
        scratch_shapes=[pltpu.VMEM((tm, d), BF16)],
        compiler_params=_params("parallel", "arbitrary"),
        name="norm_matmul",
    )(x2, norm_w.reshape(1, d), w)


def _ssd_kernel(z_ref, xbc_ref, sm_ref, dtt_ref, convw_ref, convb_ref, dtb_ref, dtbt_ref,
                alog_ref, alogt_ref, dskip_ref, nw_ref, expand_ref, y_ref,
                xbuf, state, ydiag):
    L = SSD_CHUNK
    c = pl.program_id(1)

    @pl.when(c == 0)
    def _():
        xbuf[0:8, :] = jnp.zeros((8, XBC_DIM), F32)
        state[...] = jnp.zeros(state.shape, F32)

    xbuf[8:8 + L, :] = xbc_ref[...]
    acc = convb_ref[...] + convw_ref[0:1, :] * xbuf[5:5 + L, :]
    for k in range(1, CONV_K):
        acc = acc + convw_ref[k:k + 1, :] * xbuf[5 + k:5 + k + L, :]
    xbuf[0:8, :] = xbuf[L:L + 8, :]
    xbc = _silu(acc)
    xs = xbc[:, :D_INNER]

    lane = lax.broadcasted_iota(jnp.int32, (1, LANES), 1)
    a_row = jnp.where(lane < SSM_HEADS, -jnp.exp(alog_ref[...]), 0.0)
    dt = _softplus(sm_ref[...] + dtb_ref[...])
    a_dt = dt * a_row
    dt_t = _softplus(dtt_ref[0] + dtbt_ref[...])
    a_dt_t = dt_t * (-jnp.exp(alogt_ref[...]))

    row = lax.broadcasted_iota(jnp.int32, (L, L), 0)
    col = lax.broadcasted_iota(jnp.int32, (L, L), 1)
    lower = row >= col
    tri = lower.astype(F32)
    tri_t = (row <= col).astype(F32)
    cs = jnp.dot(tri, a_dt, precision=HIGHEST, preferred_element_type=F32)
    cs_t = jnp.dot(a_dt_t, tri_t, precision=HIGHEST, preferred_element_type=F32)
    cs_last = cs[L - 1:L, :]

    stacked = jnp.concatenate([dt, jnp.exp(cs_last - cs), jnp.exp(cs)], axis=0)
    wide = jnp.dot(stacked, expand_ref[...], precision=HIGHEST, preferred_element_type=F32)
    dt_x = wide[0:L]
    dte_x = wide[L:2 * L]
    ecs_x = wide[2 * L:3 * L]

    xdt = xs * dt_x
    xw = (xdt * dte_x).astype(BF16)
    xdt_b = xdt.astype(BF16)

    hpg = SSM_HEADS // SSM_GROUPS
    gw = hpg * SSM_HEAD_DIM
    y_off_parts = []
    for g in range(SSM_GROUPS):
        b_g = xbc[:, D_INNER + g * D_STATE:D_INNER + (g + 1) * D_STATE].astype(BF16)
        c_g = xbc[:, D_INNER + SSM_GROUPS * D_STATE + g * D_STATE:
                  D_INNER + SSM_GROUPS * D_STATE + (g + 1) * D_STATE].astype(BF16)
        cb = lax.dot_general(c_g, b_g, (((1,), (1,)), ((), ())), preferred_element_type=F32)
        for hh in range(hpg):
            h = g * hpg + hh
            seg = cs[:, h:h + 1] - cs_t[h:h + 1, :]
            decay = jnp.exp(jnp.where(lower, seg, -jnp.inf))
            m = (cb * decay).astype(BF16)
            ydiag[:, h * SSM_HEAD_DIM:(h + 1) * SSM_HEAD_DIM] = jnp.dot(
                m, xdt_b[:, h * SSM_HEAD_DIM:(h + 1) * SSM_HEAD_DIM], preferred_element_type=F32)
        st_prev = state[g]
        y_off_parts.append(jnp.dot(c_g, st_prev.astype(BF16), preferred_element_type=F32))
        st_new = lax.dot_general(b_g, xw[:, g * gw:(g + 1) * gw], (((0,), (0,)), ((), ())),
                                 preferred_element_type=F32)
        state[g] = st_prev * ecs_x[L - 1:L, g * gw:(g + 1) * gw] + st_new
    y_off = jnp.concatenate(y_off_parts, axis=1) * ecs_x

    y = (ydiag[...] + y_off + dskip_ref[...] * xs) * _silu(z_ref[...])
    for g in range(SSM_GROUPS):
        yg = y[:, g * gw:(g + 1) * gw]
        yg = yg * lax.rsqrt(jnp.mean(yg * yg, axis=-1, keepdims=True) + NORM_EPS)
        y_ref[:, g * gw:(g + 1) * gw] = (yg * nw_ref[:, g * gw:(g + 1) * gw]).astype(y_ref.dtype)


def _ssd(proj_f, dt_t, conv_w, conv_b, dt_bias, a_log, d_skip, norm_w, bsz, seq):
    L = SSD_CHUNK
    nc = seq // L
    pad = LANES - SSM_HEADS
    dtb = jnp.pad(dt_bias, (0, pad)).reshape(1, LANES)
    alog = jnp.pad(a_log, (0, pad)).reshape(1, LANES)
    dskip_x = jnp.repeat(d_skip, SSM_HEAD_DIM).reshape(1, D_INNER)
    expand = (np.arange(LANES)[:, None] == (np.arange(D_INNER)[None, :] // SSM_HEAD_DIM)).astype(np.float32)
    const = lambda shape: pl.BlockSpec(shape, lambda b, c: (0,) * len(shape))
    return pl.pallas_call(
        _ssd_kernel,
        grid=(bsz, nc),
        in_specs=[
            pl.BlockSpec((L, D_INNER), lambda b, c: (b * nc + c, F_Z // D_INNER)),
            pl.BlockSpec((L, XBC_DIM), lambda b, c: (b * nc + c, F_XBC // XBC_DIM)),
            pl.BlockSpec((L, LANES), lambda b, c: (b * nc + c, F_SMALL // LANES)),
            pl.BlockSpec((1, SSM_HEADS, L), lambda b, c: (b, 0, c)),
            const((CONV_K, XBC_DIM)), const((1, XBC_DIM)),
            const((1, LANES)), const((SSM_HEADS, 1)),
            const((1, LANES)), const((SSM_HEADS, 1)),
            const((1, D_INNER)), const((1, D_INNER)),
            const((LANES, D_INNER)),
        ],
        out_specs=pl.BlockSpec((L, D_INNER), lambda b, c: (b * nc + c, 0)),
        out_shape=jax.ShapeDtypeStruct((bsz * seq, D_INNER), BF16),
        scratch_shapes=[
            pltpu.VMEM((L + 8, XBC_DIM), F32),
            pltpu.VMEM((SSM_GROUPS, D_STATE, (SSM_HEADS // SSM_GROUPS) * SSM_HEAD_DIM), F32),
            pltpu.VMEM((L, D_INNER), F32),
        ],
        compiler_params=_params("parallel", "arbitrary"),
        name="ssd",
    )(proj_f, proj_f, proj_f, dt_t, conv_w, conv_b.reshape(1, XBC_DIM),
      dtb, dt_bias.reshape(SSM_HEADS, 1), alog, a_log.reshape(SSM_HEADS, 1),
      dskip_x, norm_w.reshape(1, D_INNER), jnp.asarray(expand))


def _gelu_tanh(x):
    return 0.5 * x * (1.0 + jnp.tanh(np.sqrt(2.0 / np.pi) * (x + 0.044715 * (x * x * x))))


def _compress_kernel(cur_ref, nxt_ref, pos_ref, w1_ref, w2_ref, o_ref):
    half = CMP_STRIDE * NSA_HEAD_DIM
    lo = (cur_ref[0, 0].astype(F32) + pos_ref[0:1, :]).astype(BF16)
    hi = (nxt_ref[0, 0].astype(F32) + pos_ref[1:2, :]).astype(BF16)
    hid = (jnp.dot(lo, w1_ref[0:half, :], preferred_element_type=F32)
           + jnp.dot(hi, w1_ref[half:2 * half, :], preferred_element_type=F32))
    o_ref[0, 0] = jnp.dot(_gelu_tanh(hid).astype(BF16), w2_ref[...],
                          preferred_element_type=F32).astype(o_ref.dtype)


def _compress(kv, pos, w1, w2, bsz, seq):
    nch = seq // CMP_STRIDE
    half = CMP_STRIDE * NSA_HEAD_DIM
    ch = kv.reshape(bsz, nch, CMP_STRIDE, NSA_GROUPS, NSA_HEAD_DIM)
    ch = jnp.transpose(ch, (0, 3, 1, 2, 4)).reshape(bsz, NSA_GROUPS, nch, half)
    nxt = jnp.concatenate([ch[:, :, 1:], jnp.zeros_like(ch[:, :, :1])], axis=2)
    blk = pl.BlockSpec((1, 1, nch, half), lambda b, g: (b, g, 0, 0))
    return pl.pallas_call(
        _compress_kernel,
        grid=(bsz, NSA_GROUPS),
        in_specs=[
            blk, blk,
            pl.BlockSpec((2, half), lambda b, g: (0, 0)),
            pl.BlockSpec((2 * half, CMP_HIDDEN), lambda b, g: (0, 0)),
            pl.BlockSpec((CMP_HIDDEN, NSA_HEAD_DIM), lambda b, g: (0, 0)),
        ],
        out_specs=pl.BlockSpec((1, 1, nch, NSA_HEAD_DIM), lambda b, g: (b, g, 0, 0)),
        out_shape=jax.ShapeDtypeStruct((bsz, NSA_GROUPS, nch, NSA_HEAD_DIM), BF16),
        compiler_params=_params("parallel", "parallel"),
        name="compress",
    )(ch, nxt, pos.reshape(2, half), w1.astype(BF16), w2.astype(BF16))


NSA_TQ = 256
NSA_KC = 512
MASK_VALUE = -1e30
SOFTMAX_M0 = -1e20


def _nsa_kernel(q_ref, kc_ref, vc_ref, ks_ref, vs_ref, kw_ref, vw_ref, gate_ref,
                overlap_ref, expand_ref, o_ref, selx_ref):
    R, TQ, KC, DH = NSA_REP, NSA_TQ, NSA_KC, NSA_HEAD_DIM
    i = pl.program_id(2)
    q0 = i * TQ
    nt = (((1,), (1,)), ((), ()))
    tcol = q0 + lax.broadcasted_iota(jnp.int32, (TQ, 1), 0)
    qblk = q_ref[...]
    scale = DH ** -0.5
    qh = [(qblk[:, r * DH:(r + 1) * DH].astype(F32) * scale).astype(BF16) for r in range(R)]
    v_width = vs_ref.shape[3]

    def flash_step(m_i, acc, q, k, v_ext, msk):
        s = jnp.where(msk, lax.dot_general(q, k, nt, preferred_element_type=F32), MASK_VALUE)
        m_new = jnp.maximum(m_i, jnp.max(s, axis=-1, keepdims=True))
        p = jnp.exp(s - m_new)
        acc = acc * jnp.exp(m_i - m_new) + jnp.dot(p.astype(BF16), v_ext, preferred_element_type=F32)
        return m_new, acc

    def init():
        return tuple((jnp.full((TQ, 1), SOFTMAX_M0, F32), jnp.zeros((TQ, v_width), F32)) for _ in range(R))

    def finish(carry):
        return [acc[:, :DH] / jnp.maximum(acc[:, DH:DH + 1], 1e-30) for _, acc in carry]

    w0 = pl.multiple_of(jnp.maximum(q0 - WINDOW, 0), TQ)
    kpos_w = w0 + lax.broadcasted_iota(jnp.int32, (1, WINDOW + TQ), 1)
    msk_w = (kpos_w <= tcol) & (kpos_w > tcol - WINDOW)
    k_w = kw_ref[0, 0, pl.ds(w0, WINDOW + TQ), :]
    v_w = vw_ref[0, 0, pl.ds(w0, WINDOW + TQ), :]
    carry_w = init()
    o_win = finish(tuple(flash_step(carry_w[r][0], carry_w[r][1], qh[r], k_w, v_w, msk_w) for r in range(R)))

    kc = kc_ref[0, 0]
    vc = vc_ref[0, 0]
    ncmp = kc.shape[0]
    cmp_end = lax.broadcasted_iota(jnp.int32, (1, ncmp), 1) * CMP_STRIDE + (CMP_BLOCK - 1)
    cmask = cmp_end <= tcol
    o_cmp = []
    p4 = None
    for r in range(R):
        lg = jnp.where(cmask, lax.dot_general(qh[r], kc, nt, preferred_element_type=F32), -jnp.inf)
        mx = jnp.max(lg, axis=-1, keepdims=True)
        mx = jnp.where(mx > -jnp.inf, mx, 0.0)
        e = jnp.where(cmask, jnp.exp(lg - mx), 0.0)
        p = e / jnp.maximum(jnp.sum(e, axis=-1, keepdims=True), 1e-30)
        o_cmp.append(jnp.dot(p.astype(BF16), vc, preferred_element_type=F32))
        p4 = p if p4 is None else p4 + p

    imp = lax.dot_general(overlap_ref[...], p4, nt, precision=HIGHEST, preferred_element_type=F32)
    nsel = imp.shape[0]
    trow = q0 + lax.broadcasted_iota(jnp.int32, (1, TQ), 1)
    cur = trow // SEL_BLOCK
    jj = lax.broadcasted_iota(jnp.int32, (nsel, 1), 0)
    forced = (jj == 0) | (jj == cur) | (jj == cur - 1)
    val = jnp.where(forced, jnp.inf, jnp.where(jj <= cur, imp, -jnp.inf))
    rank = jnp.zeros((nsel, TQ), jnp.int32)
    for b in range(nsel):
        vb = val[b:b + 1, :]
        beats = (vb > val) | ((vb == val) & (jj > b))
        rank = rank + beats.astype(jnp.int32)
    sel = (rank < SEL_TOPN).astype(BF16)
    selx = lax.dot_general(sel, expand_ref[...], (((0,), (0,)), ((), ())),
                           preferred_element_type=F32)
    for kb in range(selx_ref.shape[0]):
        selx_ref[kb] = selx[:, kb * KC:(kb + 1) * KC]

    def sel_body(kb, carry):
        k0 = pl.multiple_of(kb * KC, KC)
        kpos = k0 + lax.broadcasted_iota(jnp.int32, (1, KC), 1)
        msk = (selx_ref[kb] > 0.5) & (kpos <= tcol)
        k = ks_ref[0, 0, pl.ds(k0, KC), :]
        v = vs_ref[0, 0, pl.ds(k0, KC), :]
        return tuple(flash_step(carry[r][0], carry[r][1], qh[r], k, v, msk) for r in range(R))

    o_sel = finish(lax.fori_loop(0, (q0 + TQ + KC - 1) // KC, sel_body, init()))

    g = _sigmoid(gate_ref[0, 0])
    outs = [g[:, 3 * r:3 * r + 1] * o_cmp[r] + g[:, 3 * r + 1:3 * r + 2] * o_sel[r]
            + g[:, 3 * r + 2:3 * r + 3] * o_win[r] for r in range(R)]
    o_ref[...] = jnp.concatenate(outs, axis=1).astype(o_ref.dtype)


def _nsa(proj_q, gates, kc, vc, bsz, seq):
    G, R, DH, TQ = NSA_GROUPS, NSA_REP, NSA_HEAD_DIM, NSA_TQ
    n_cmp_pad = seq // CMP_STRIDE
    n_sel = seq // SEL_BLOCK
    nq = seq // TQ

    def kv(idx, ones_col):
        lo = 1024 + idx * NSA_KV_WIDTH
        a = jnp.transpose(proj_q[:, lo:lo + NSA_KV_WIDTH].reshape(bsz, seq, G, DH), (0, 2, 1, 3))
        if ones_col:
            ext = jnp.zeros((bsz, G, seq, LANES - DH), BF16).at[..., 0].set(1.0)
            a = jnp.concatenate([a, ext], axis=-1)
        return a

    ks, vs, kw, vw = kv(2, False), kv(3, True), kv(4, False), kv(5, True)
    gt = jnp.transpose(gates.reshape(bsz, seq, G, R * 3), (0, 2, 1, 3))

    c_start = np.arange(n_cmp_pad) * CMP_STRIDE
    s_start = np.arange(n_sel) * SEL_BLOCK
    overlap = ((c_start[:, None] < s_start[None, :] + SEL_BLOCK)
               & (c_start[:, None] + CMP_BLOCK > s_start[None, :])).astype(np.float32)
    overlap[(seq - CMP_BLOCK) // CMP_STRIDE + 1:] = 0.0
    expand = (np.arange(n_sel)[:, None] == (np.arange(seq)[None, :] // SEL_BLOCK)).astype(np.float32)

    qblk = pl.BlockSpec((TQ, R * DH), lambda b, g, i: (b * nq + i, g))
    cblk = pl.BlockSpec((1, 1, n_cmp_pad, DH), lambda b, g, i: (b, g, 0, 0))
    kblk = pl.BlockSpec((1, 1, seq, DH), lambda b, g, i: (b, g, 0, 0))
    vblk = pl.BlockSpec((1, 1, seq, LANES), lambda b, g, i: (b, g, 0, 0))
    return pl.pallas_call(
        _nsa_kernel,
        grid=(bsz, G, nq),
        in_specs=[
            qblk, cblk, cblk, kblk, vblk, kblk, vblk,
            pl.BlockSpec((1, 1, TQ, R * 3), lambda b, g, i: (b, g, i, 0)),
            pl.BlockSpec((n_sel, n_cmp_pad), lambda b, g, i: (0, 0)),
            pl.BlockSpec((n_sel, seq), lambda b, g, i: (0, 0)),
        ],
        out_specs=qblk,
        out_shape=jax.ShapeDtypeStruct((bsz * seq, G * R * DH), BF16),
        scratch_shapes=[pltpu.VMEM((seq // NSA_KC, TQ, NSA_KC), F32)],
        compiler_params=_params("parallel", "parallel", "arbitrary"),
        name="nsa_attention",
    )(proj_q, kc, vc, ks, vs, kw, vw, gt, jnp.asarray(overlap.T), jnp.asarray(expand, dtype=BF16))


ROW_TILES = D_MODEL // LANES


def _store_token_major(ref, val):
    n = val.shape[0]
    for c in range(ROW_TILES):
        ref[pl.ds(c, n, stride=ROW_TILES), :] = val[:, c * LANES:(c + 1) * LANES]


def _load_token_major(ref, n):
    return jnp.concatenate([ref[pl.ds(c, n, stride=ROW_TILES), :] for c in range(ROW_TILES)], axis=1)


def _merge_kernel(ys_ref, yn_ref, gs_ref, gn_ref, x_ref, wbs_ref, wbn_ref, wo_ref, fw_ref,
                  rw_ref, rb_ref, tri_ref, x1_ref, h_ref, ti_ref, tw_ref, slot_ref, cnt_ref, base_ref):
    @pl.when(pl.program_id(0) == 0)
    def _():
        base_ref[...] = jnp.zeros(base_ref.shape, F32)

    a = jnp.dot(ys_ref[...], wbs_ref[...], preferred_element_type=F32)
    b = jnp.dot(yn_ref[...], wbn_ref[...], preferred_element_type=F32)
    merged = _sigmoid(gs_ref[...]) * a + _sigmoid(gn_ref[...]) * b
    x1 = x_ref[...] + jnp.dot(merged.astype(BF16), wo_ref[...], preferred_element_type=F32)
    x1_ref[...] = x1
    var = jnp.mean(x1 * x1, axis=-1, keepdims=True)
    h = x1 * lax.rsqrt(var + NORM_EPS) * fw_ref[...]
    _store_token_major(h_ref, h)

    logits = jnp.dot(h, rw_ref[...], precision=HIGHEST, preferred_element_type=F32) + rb_ref[...]
    ne = logits.shape[1]
    jj = lax.broadcasted_iota(jnp.int32, (1, ne), 1)
    rank = jnp.zeros(logits.shape, jnp.int32)
    for e in range(ne):
        ce = logits[:, e:e + 1]
        beats = (ce > logits) | ((ce == logits) & (jj > e))
        rank = rank + beats.astype(jnp.int32)
    sel = rank < TOP_K
    mx = jnp.max(logits, axis=-1, keepdims=True)
    p = jnp.where(sel, jnp.exp(logits - mx), 0.0)
    p = p / jnp.sum(p, axis=-1, keepdims=True)

    self01 = sel.astype(F32)
    before = jnp.dot(tri_ref[...], self01.astype(BF16), preferred_element_type=F32) + base_ref[...]
    for k in range(TOP_K):
        hit = rank == k
        ti_ref[:, k:k + 1] = jnp.sum(jnp.where(hit, jj, 0), axis=-1, keepdims=True)
        tw_ref[:, k:k + 1] = jnp.sum(jnp.where(hit, p, 0.0), axis=-1, keepdims=True)
        slot_ref[:, k:k + 1] = jnp.sum(jnp.where(hit, before, 0.0), axis=-1, keepdims=True).astype(jnp.int32)
    base_ref[...] = base_ref[...] + jnp.sum(self01, axis=0, keepdims=True)
    cnt_ref[...] = base_ref[...]


def _merge_route(y_ssm, y_nsa, proj_f, x2, wbs, wbn, wo, ffn_w, router_w, router_b, tm):
    t, d = x2.shape
    row = lambda col: pl.BlockSpec((tm, d), lambda i: (i, col))
    const = lambda shape: pl.BlockSpec(shape, lambda i: (0,) * len(shape))
    kblk = pl.BlockSpec((tm, TOP_K), lambda i: (i, 0))
    tri = np.tril(np.ones((tm, tm), np.float32), -1)
    return pl.pallas_call(
        _merge_kernel,
        grid=(t // tm,),
        in_specs=[
            row(0), row(0), row(F_MS // d), row(F_MN // d), row(0),
            const((d, d)), const((d, d)), const((d, d)), const((1, d)),
            const((d, N_EXPERTS)), const((1, N_EXPERTS)), const((tm, tm)),
        ],
        out_specs=[row(0), pl.BlockSpec((tm * ROW_TILES, LANES), lambda i: (i, 0)),
                   kblk, kblk, kblk, const((1, N_EXPERTS))],
        out_shape=[jax.ShapeDtypeStruct((t, d), F32), jax.ShapeDtypeStruct((t * ROW_TILES, LANES), F32),
                   jax.ShapeDtypeStruct((t, TOP_K), jnp.int32), jax.ShapeDtypeStruct((t, TOP_K), F32),
                   jax.ShapeDtypeStruct((t, TOP_K), jnp.int32), jax.ShapeDtypeStruct((1, N_EXPERTS), F32)],
        scratch_shapes=[pltpu.VMEM((1, N_EXPERTS), F32)],
        compiler_params=_params("arbitrary"),
        name="merge_route",
    )(y_ssm, y_nsa, proj_f, proj_f, x2, wbs.astype(BF16), wbn.astype(BF16), wo.astype(BF16),
      ffn_w.reshape(1, d), router_w, router_b.reshape(1, N_EXPERTS), jnp.asarray(tri, dtype=BF16))


GU_CHUNK = 2 * LANES


def _deinterleave_kernel(w_ref, perm_ref, o_ref):
    o_ref[0] = jnp.dot(w_ref[0].astype(BF16), perm_ref[...], preferred_element_type=F32).astype(o_ref.dtype)


def _deinterleave(w):
    e, d, n = w.shape
    perm = np.zeros((GU_CHUNK, GU_CHUNK), np.float32)
    perm[2 * np.arange(LANES), np.arange(LANES)] = 1.0
    perm[2 * np.arange(LANES) + 1, LANES + np.arange(LANES)] = 1.0
    blk = pl.BlockSpec((1, d, GU_CHUNK), lambda i, j: (i, 0, j))
    return pl.pallas_call(
        _deinterleave_kernel,
        grid=(e, n // GU_CHUNK),
        in_specs=[blk, pl.BlockSpec((GU_CHUNK, GU_CHUNK), lambda i, j: (0, 0))],
        out_specs=blk,
        out_shape=jax.ShapeDtypeStruct((e, d, n), BF16),
        compiler_params=_params("parallel", "parallel"),
        name="deinterleave",
    )(w, jnp.asarray(perm, dtype=BF16))


MOE_TM = 256


def _moe_kernel(te_ref, nu_ref, tok0_ref, tokn_ref, dst_ref, h_hbm, wgu_ref, bgu_ref, wd_ref, bd_ref,
                y_hbm, xbuf, ybuf, gsem, ssem, *, spare_row):
    tm, rt = MOE_TM, ROW_TILES
    i = pl.program_id(0)
    nu = nu_ref[0]
    slot = i % 2
    other = 1 - slot

    def gather_copy(src_row, r, sl):
        return pltpu.make_async_copy(h_hbm.at[pl.ds(src_row, rt), :], xbuf.at[sl, pl.ds(r * rt, rt), :],
                                     gsem.at[sl])

    def scatter_copy(dst_row, r, sl):
        return pltpu.make_async_copy(ybuf.at[sl, pl.ds(r * rt, rt), :], y_hbm.at[pl.ds(dst_row, rt), :],
                                     ssem.at[sl])

    def start_gather(tok_ref, sl):
        for r in range(tm):
            gather_copy(pl.multiple_of(tok_ref[0, 0, r], rt), r, sl).start()

    def start_scatter(sl):
        for r in range(tm):
            scatter_copy(pl.multiple_of(dst_ref[0, 0, r], rt), r, sl).start()

    def wait_gather(sl):
        pltpu.make_async_copy(h_hbm.at[pl.ds(0, tm * rt), :], xbuf.at[sl], gsem.at[sl]).wait()

    def wait_scatter(sl):
        pltpu.make_async_copy(ybuf.at[sl], y_hbm.at[pl.ds(0, tm * rt), :], ssem.at[sl]).wait()

    @pl.when(i == 0)
    def _():
        ybuf[...] = jnp.zeros(ybuf.shape, ybuf.dtype)
        start_gather(tok0_ref, 0)
        for r in range(tm):
            scatter_copy((spare_row + r) * rt, r, 0).start()

    @pl.when(i < nu)
    def _():
        wait_gather(slot)
        start_gather(tokn_ref, other)
        wait_scatter(slot)
        start_scatter(other)
        x =_load_token_major(xbuf.at[slot], tm).astype(BF16)
        h1 = jnp.dot(x, wgu_ref[0], preferred_element_type=F32) + bgu_ref[0]
        nchunk = h1.shape[1] // GU_CHUNK
        glu = jnp.concatenate([h1[:, c * GU_CHUNK:c * GU_CHUNK + LANES] for c in range(nchunk)], axis=1)
        lin = jnp.concatenate([h1[:, c * GU_CHUNK + LANES:(c + 1) * GU_CHUNK] for c in range(nchunk)], axis=1)
        glu = jnp.minimum(glu, SWIGLU_LIMIT)
        lin = jnp.clip(lin, -SWIGLU_LIMIT, SWIGLU_LIMIT)
        act = glu * _sigmoid(SWIGLU_ALPHA * glu) * (lin + 1.0)
        y = jnp.dot(act.astype(BF16), wd_ref[0], preferred_element_type=F32) + bd_ref[0]
        _store_token_major(ybuf.at[slot], y)

    @pl.when(i == nu)
    def _():
        wait_gather(slot)
        wait_scatter(slot)
        start_scatter(other)
        wait_scatter(other)


def _moe_experts(h_rows, tok_tiles, dst_tiles, tile_expert, n_used, wgu, bgu, wd, bd, n_out_rows, spare_row):
    d = wd.shape[1]
    tm, rt = MOE_TM, ROW_TILES
    n_tiles = tok_tiles.shape[0]
    smem = lambda f: pl.BlockSpec((1, 1, tm), f, memory_space=pltpu.SMEM)
    grid_spec = pltpu.PrefetchScalarGridSpec(
        num_scalar_prefetch=2,
        grid=(n_tiles,),
        in_specs=[
            smem(lambda i, te, nu: (0, 0, 0)),
            smem(lambda i, te, nu: (jnp.minimum(i + 1, n_tiles - 1), 0, 0)),
            smem(lambda i, te, nu: (i, 0, 0)),
            pl.BlockSpec(memory_space=pl.ANY),
            pl.BlockSpec((1, d, 2 * d), lambda i, te, nu: (te[i], 0, 0)),
            pl.BlockSpec((1, 1, 2 * d), lambda i, te, nu: (te[i], 0, 0)),
            pl.BlockSpec((1, d, d), lambda i, te, nu: (te[i], 0, 0)),
            pl.BlockSpec((1, 1, d), lambda i, te, nu: (te[i], 0, 0)),
        ],
        out_specs=pl.BlockSpec(memory_space=pl.ANY),
        scratch_shapes=[
            pltpu.VMEM((2, tm * rt, LANES), F32),
            pltpu.VMEM((2, tm * rt, LANES), F32),
            pltpu.SemaphoreType.DMA((2,)),
            pltpu.SemaphoreType.DMA((2,)),
        ],
    )
    return pl.pallas_call(
        functools.partial(_moe_kernel, spare_row=spare_row),
        grid_spec=grid_spec,
        out_shape=jax.ShapeDtypeStruct((n_out_rows * rt, LANES), F32),
        compiler_params=_params("arbitrary"),
        name="moe_experts",
    )(tile_expert, n_used, tok_tiles, tok_tiles, dst_tiles, h_rows, wgu, bgu, wd, bd)


def _final_kernel(x1_ref, y0_ref, y1_ref, y2_ref, y3_ref, tw_ref, nw_ref, o_ref):
    tm = x1_ref.shape[0]
    tw = tw_ref[...]
    yk = [_load_token_major(r, tm) for r in (y0_ref, y1_ref, y2_ref, y3_ref)]
    moe = ((tw[:, 0:1] * yk[0] + tw[:, 1:2] * yk[1]) + (tw[:, 2:3] * yk[2] + tw[:, 3:4] * yk[3]))
    x = x1_ref[...] + moe
    var = jnp.mean(x * x, axis=-1, keepdims=True)
    o_ref[...] = x * lax.rsqrt(var + NORM_EPS) * nw_ref[...]


def _final_norm(x1, y_rows, top_w, norm_w, tm):
    t, d = x1.shape
    nt = t // tm
    yblk = lambda k: pl.BlockSpec((tm * ROW_TILES, LANES), lambda i: (k * nt + i, 0))
    return pl.pallas_call(
        _final_kernel,
        grid=(nt,),
        in_specs=[
            pl.BlockSpec((tm, d), lambda i: (i, 0)),
            yblk(0), yblk(1), yblk(2), yblk(3),
            pl.BlockSpec((tm, TOP_K), lambda i: (i, 0)),
            pl.BlockSpec((1, d), lambda i: (0, 0)),
        ],
        out_specs=pl.BlockSpec((tm, d), lambda i: (i, 0)),
        out_shape=jax.ShapeDtypeStruct((t, d), F32),
        compiler_params=_params("parallel"),
        name="final_norm",
    )(x1, y_rows, y_rows, y_rows, y_rows, top_w, norm_w.reshape(1, d))


def _dispatch_tables(top_i, slot, counts, n_rows):
    t = top_i.shape[0]
    n = t * TOP_K
    tm, rt = MOE_TM, ROW_TILES
    padded = ((counts + tm - 1) // tm) * tm
    pend = jnp.cumsum(padded)
    pstart = pend - padded
    onehot = top_i[:, :, None] == jnp.arange(N_EXPERTS, dtype=jnp.int32)
    dest = slot + jnp.sum(jnp.where(onehot, pstart, 0), axis=-1)
    row_pair = jnp.full((n_rows,), -1, jnp.int32).at[dest.reshape(-1)].set(jnp.arange(n, dtype=jnp.int32))
    live = row_pair >= 0
    tok_tiles = (jnp.where(live, row_pair // TOP_K, 0) * rt).reshape(n_rows // tm, 1, tm)
    rows = jnp.arange(n_rows, dtype=jnp.int32)
    out_row = jnp.where(live, (row_pair % TOP_K) * t + row_pair // TOP_K, n + tm + rows % tm)
    spare = n + tm + jnp.arange(tm, dtype=jnp.int32)
    dst_tiles = (jnp.concatenate([spare, out_row]) * rt).reshape(n_rows // tm + 1, 1, tm)
    tile_start = jnp.arange(n_rows // tm, dtype=jnp.int32) * tm
    tile_expert = jnp.minimum(jnp.sum(tile_start[:, None] >= pend[None, :], axis=-1), N_EXPERTS - 1)
    n_used = (pend[-1] // tm).reshape(1)
    return tok_tiles, dst_tiles, tile_expert.astype(jnp.int32), n_used.astype(jnp.int32)


def _in_proj_weights(w_in):
    o = np.cumsum([0, D_INNER, XBC_DIM, SSM_HEADS, 1024, 256, 256, 256, 256, 256, 256, 3 * NSA_HEADS, 2 * D_MODEL])
    z, xbc, dt, q, kv, gate, mg = (w_in[:, o[0]:o[1]], w_in[:, o[1]:o[2]], w_in[:, o[2]:o[3]],
                                   w_in[:, o[3]:o[4]], w_in[:, o[4]:o[10]], w_in[:, o[10]:o[11]],
                                   w_in[:, o[11]:o[12]])
    small_pad = jnp.zeros((D_MODEL, F_WIDTH - F_SMALL - SSM_HEADS - 3 * NSA_HEADS), w_in.dtype)
    w_f = jnp.concatenate([z, mg[:, :D_MODEL], xbc, mg[:, D_MODEL:], dt, gate, small_pad], axis=1)
    w_q = jnp.concatenate([q, kv], axis=1)
    return w_f.astype(BF16), w_q.astype(BF16)


def kernel(x, mix_norm_w, w_in, conv_w, conv_b, dt_bias, a_log, d_skip, ssm_norm_w, cmp_pos_k, cmp_w1_k, cmp_w2_k, cmp_pos_v, cmp_w1_v, cmp_w2_v, w_branch_ssm, w_branch_nsa, w_out, ffn_norm_w, router_w, router_b, w_gate_up, b_gate_up, w_down, b_down, final_norm_w):
    bsz, seq, d = x.shape
    t = bsz * seq
    x2 = x.reshape(t, d)
    depth = w_in.shape[0]
    assert depth == 1, "single-layer block"
    for l in range(depth):
        w_f, w_q = _in_proj_weights(w_in[l])
        proj_f = _norm_matmul(x2, mix_norm_w[l], w_f, F32, 1024, 768)
        proj_q = _norm_matmul(x2, mix_norm_w[l], w_q, BF16, 1024, 512)

        small = proj_f[:, F_SMALL:F_SMALL + LANES]
        dt_t = jnp.transpose(small[:, :SSM_HEADS].reshape(bsz, seq, SSM_HEADS), (0, 2, 1))
        y_ssm = _ssd(proj_f, dt_t, conv_w[l], conv_b[l], dt_bias[l], a_log[l], d_skip[l], ssm_norm_w[l],
                     bsz, seq)

        kc = _compress(proj_q[:, 1024:1280], cmp_pos_k[l], cmp_w1_k[l], cmp_w2_k[l], bsz, seq)
        vc = _compress(proj_q[:, 1280:1536], cmp_pos_v[l], cmp_w1_v[l], cmp_w2_v[l], bsz, seq)
        gates = small[:, SSM_HEADS:SSM_HEADS + 3 * NSA_HEADS]
        y_nsa = _nsa(proj_q, gates, kc, vc, bsz, seq)

        x1, h, top_i, top_w, slot, counts = _merge_route(
            y_ssm, y_nsa, proj_f, x2, w_branch_ssm[l], w_branch_nsa[l], w_out[l], ffn_norm_w[l],
            router_w[l], router_b[l], 512)

        n_rows = t * TOP_K + N_EXPERTS * MOE_TM
        tok_tiles, dst_tiles, tile_expert, n_used = _dispatch_tables(
            top_i, slot, counts[0].astype(jnp.int32), n_rows)
        wgu = _deinterleave(w_gate_up[l])
        nchunk = 2 * d // GU_CHUNK
        bgu = jnp.transpose(b_gate_up[l].reshape(N_EXPERTS, nchunk, LANES, 2), (0, 1, 3, 2))
        y_rows = _moe_experts(h, tok_tiles, dst_tiles, tile_expert, n_used, wgu,
                              bgu.reshape(N_EXPERTS, 1, 2 * d), w_down[l].astype(BF16), b_down[l][:, None, :],
                              n_out_rows=t * TOP_K + 2 * MOE_TM, spare_row=t * TOP_K)
    out = _final_norm(x1, y_rows, top_w, final_norm_w, 512)
    return out.reshape(bsz, seq, d)
```

```python
import functools

import numpy as np
import jax
import jax.numpy as jnp
from jax import lax
from jax.experimental import pallas as pl
from jax.experimental.pallas import tpu as pltpu

F32 = jnp.float32
BF16 = jnp.bfloat16
HIGHEST = lax.Precision.HIGHEST

D_MODEL = 1024
D_INNER = 1024
SSM_HEAD_DIM = 64
SSM_HEADS = 16
SSM_GROUPS = 4
D_STATE = 128
CONV_K = 4
XBC_DIM = D_INNER + 2 * SSM_GROUPS * D_STATE
SSD_CHUNK = 128
NSA_HEAD_DIM = 64
NSA_HEADS = 16
NSA_GROUPS = 4
NSA_REP = 4
NSA_KV_WIDTH = 256
CMP_BLOCK = 32
CMP_STRIDE = 16
CMP_HIDDEN = 128
SEL_BLOCK = 64
SEL_TOPN = 8
WINDOW = 256
N_EXPERTS = 32
TOP_K = 4
SWIGLU_ALPHA = 1.702
SWIGLU_LIMIT = 7.0
NORM_EPS = 1e-5

LANES = 128
VMEM_LIMIT = 56 * 1024 * 1024

F_Z, F_MS, F_XBC, F_MN, F_SMALL = 0, 1024, 2048, 4096, 5120
F_WIDTH = 5376
Q_WIDTH = 1024 + 6 * NSA_KV_WIDTH


def _sigmoid(x):
    return 1.0 / (1.0 + jnp.exp(-x))


def _silu(x):
    return x * _sigmoid(x)


def _softplus(x):
    return jnp.maximum(x, 0.0) + jnp.log1p(jnp.exp(-jnp.abs(x)))


def _params(*sem):
    return pltpu.CompilerParams(dimension_semantics=sem, vmem_limit_bytes=VMEM_LIMIT)


def _norm_matmul_kernel(x_ref, nw_ref, w_ref, o_ref, h_ref):
    @pl.when(pl.program_id(1) == 0)
    def _():
        x = x_ref[...]
        var = jnp.mean(x * x, axis=-1, keepdims=True)
        h_ref[...] = (x * lax.rsqrt(var + NORM_EPS) * nw_ref[...]).astype(BF16)

    o_ref[...] = jnp.dot(h_ref[...], w_ref[...], preferred_element_type=F32).astype(o_ref.dtype)


def _norm_matmul(x2, norm_w, w, out_dtype, tm, tn):
    t, d = x2.shape
    n = w.shape[1]
    return pl.pallas_call(
        _norm_matmul_kernel,
        grid=(t // tm, n // tn),
        in_specs=[
            pl.BlockSpec((tm, d), lambda i, j: (i, 0)),
            pl.BlockSpec((1, d), lambda i, j: (0, 0)),
            pl.BlockSpec((d, tn), lambda i, j: (0, j)),
        ],
        out_specs=pl.BlockSpec((tm, tn), lambda i, j: (i, j)),
        out_shape=jax.ShapeDtypeStruct((t, n), out_dtype),
        scratch_shapes=[pltpu.VMEM((tm, d), BF16)],
        compiler_params=_params("parallel", "arbitrary"),
        name="norm_matmul",
    )(x2, norm_w.reshape(1, d), w)


def _ssd_kernel(z_ref, xbc_ref, sm_ref, dtt_ref, convw_ref, convb_ref, dtb_ref, dtbt_ref,
                alog_ref, alogt_ref, dskip_ref, nw_ref, expand_ref, y_ref,
                xbuf, state, ydiag):
    L = SSD_CHUNK
    c = pl.program_id(1)

    @pl.when(c == 0)
    def _():
        xbuf[0:8, :] = jnp.zeros((8, XBC_DIM), F32)
        state[...] = jnp.zeros(state.shape, F32)

    xbuf[8:8 + L, :] = xbc_ref[...]
    acc = convb_ref[...] + convw_ref[0:1, :] * xbuf[5:5 + L, :]
    for k in range(1, CONV_K):
        acc = acc + convw_ref[k:k + 1, :] * xbuf[5 + k:5 + k + L, :]
    xbuf[0:8, :] = xbuf[L:L + 8, :]
    xbc = _silu(acc)
    xs = xbc[:, :D_INNER]

    lane = lax.broadcasted_iota(jnp.int32, (1, LANES), 1)
    a_row = jnp.where(lane < SSM_HEADS, -jnp.exp(alog_ref[...]), 0.0)
    dt = _softplus(sm_ref[...] + dtb_ref[...])
    a_dt = dt * a_row
    dt_t = _softplus(dtt_ref[0] + dtbt_ref[...])
    a_dt_t = dt_t * (-jnp.exp(alogt_ref[...]))

    row = lax.broadcasted_iota(jnp.int32, (L, L), 0)
    col = lax.broadcasted_iota(jnp.int32, (L, L), 1)
    lower = row >= col
    tri = lower.astype(F32)
    tri_t = (row <= col).astype(F32)
    cs = jnp.dot(tri, a_dt, precision=HIGHEST, preferred_element_type=F32)
    cs_t = jnp.dot(a_dt_t, tri_t, precision=HIGHEST, preferred_element_type=F32)
    cs_last = cs[L - 1:L, :]

    stacked = jnp.concatenate([dt, jnp.exp(cs_last - cs), jnp.exp(cs)], axis=0)
    wide = jnp.dot(stacked, expand_ref[...], precision=HIGHEST, preferred_element_type=F32)
    dt_x = wide[0:L]
    dte_x = wide[L:2 * L]
    ecs_x = wide[2 * L:3 * L]

    xdt = xs * dt_x
    xw = (xdt * dte_x).astype(BF16)
    xdt_b = xdt.astype(BF16)

    hpg = SSM_HEADS // SSM_GROUPS
    gw = hpg * SSM_HEAD_DIM
    y_off_parts = []
    for g in range(SSM_GROUPS):
        b_g = xbc[:, D_INNER + g * D_STATE:D_INNER + (g + 1) * D_STATE].astype(BF16)
        c_g = xbc[:, D_INNER + SSM_GROUPS * D_STATE + g * D_STATE:
                  D_INNER + SSM_GROUPS * D_STATE + (g + 1) * D_STATE].astype(BF16)
        cb = lax.dot_general(c_g, b_g, (((1,), (1,)), ((), ())), preferred_element_type=F32)
        for hh in range(hpg):
            h = g * hpg + hh
            seg = cs[:, h:h + 1] - cs_t[h:h + 1, :]
            decay = jnp.exp(jnp.where(lower, seg, -jnp.inf))
            m = (cb * decay).astype(BF16)
            ydiag[:, h * SSM_HEAD_DIM:(h + 1) * SSM_HEAD_DIM] = jnp.dot(
                m, xdt_b[:, h * SSM_HEAD_DIM:(h + 1) * SSM_HEAD_DIM], preferred_element_type=F32)
        st_prev = state[g]
        y_off_parts.append(jnp.dot(c_g, st_prev.astype(BF16), preferred_element_type=F32))
        st_new = lax.dot_general(b_g, xw[:, g * gw:(g + 1) * gw], (((0,), (0,)), ((), ())),
                                 preferred_element_type=F32)
        state[g] = st_prev * ecs_x[L - 1:L, g * gw:(g + 1) * gw] + st_new
    y_off = jnp.concatenate(y_off_parts, axis=1) * ecs_x

    y = (ydiag[...] + y_off + dskip_ref[...] * xs) * _silu(z_ref[...])
    for g in range(SSM_GROUPS):
        yg = y[:, g * gw:(g + 1) * gw]
        yg = yg * lax.rsqrt(jnp.mean(yg * yg, axis=-1, keepdims=True) + NORM_EPS)
        y_ref[:, g * gw:(g + 1) * gw] = (yg * nw_ref[:, g * gw:(g + 1) * gw]).astype(y_ref.dtype)


def _ssd(proj_f, dt_t, conv_w, conv_b, dt_bias, a_log, d_skip, norm_w, bsz, seq):
    L = SSD_CHUNK
    nc = seq // L
    pad = LANES - SSM_HEADS
    dtb = jnp.pad(dt_bias, (0, pad)).reshape(1, LANES)
    alog = jnp.pad(a_log, (0, pad)).reshape(1, LANES)
    dskip_x = jnp.repeat(d_skip, SSM_HEAD_DIM).reshape(1, D_INNER)
    expand = (np.arange(LANES)[:, None] == (np.arange(D_INNER)[None, :] // SSM_HEAD_DIM)).astype(np.float32)
    const = lambda shape: pl.BlockSpec(shape, lambda b, c: (0,) * len(shape))
    return pl.pallas_call(
        _ssd_kernel,
        grid=(bsz, nc),
        in_specs=[
            pl.BlockSpec((L, D_INNER), lambda b, c: (b * nc + c, F_Z // D_INNER)),
            pl.BlockSpec((L, XBC_DIM), lambda b, c: (b * nc + c, F_XBC // XBC_DIM)),
            pl.BlockSpec((L, LANES), lambda b, c: (b * nc + c, F_SMALL // LANES)),
            pl.BlockSpec((1, SSM_HEADS, L), lambda b, c: (b, 0, c)),
            const((CONV_K, XBC_DIM)), const((1, XBC_DIM)),
            const((1, LANES)), const((SSM_HEADS, 1)),
            const((1, LANES)), const((SSM_HEADS, 1)),
            const((1, D_INNER)), const((1, D_INNER)),
            const((LANES, D_INNER)),
        ],
        out_specs=pl.BlockSpec((L, D_INNER), lambda b, c: (b * nc + c, 0)),
        out_shape=jax.ShapeDtypeStruct((bsz * seq, D_INNER), BF16),
        scratch_shapes=[
            pltpu.VMEM((L + 8, XBC_DIM), F32),
            pltpu.VMEM((SSM_GROUPS, D_STATE, (SSM_HEADS // SSM_GROUPS) * SSM_HEAD_DIM), F32),
            pltpu.VMEM((L, D_INNER), F32),
        ],
        compiler_params=_params("parallel", "arbitrary"),
        name="ssd",
    )(proj_f, proj_f, proj_f, dt_t, conv_w, conv_b.reshape(1, XBC_DIM),
      dtb, dt_bias.reshape(SSM_HEADS, 1), alog, a_log.reshape(SSM_HEADS, 1),
      dskip_x, norm_w.reshape(1, D_INNER), jnp.asarray(expand))


def _gelu_tanh(x):
    return 0.5 * x * (1.0 + jnp.tanh(np.sqrt(2.0 / np.pi) * (x + 0.044715 * (x * x * x))))


def _compress_kernel(cur_ref, nxt_ref, pos_ref, w1_ref, w2_ref, o_ref):
    half = CMP_STRIDE * NSA_HEAD_DIM
    lo = (cur_ref[0, 0].astype(F32) + pos_ref[0:1, :]).astype(BF16)
    hi = (nxt_ref[0, 0].astype(F32) + pos_ref[1:2, :]).astype(BF16)
    hid = (jnp.dot(lo, w1_ref[0:half, :], preferred_element_type=F32)
           + jnp.dot(hi, w1_ref[half:2 * half, :], preferred_element_type=F32))
    o_ref[0, 0] = jnp.dot(_gelu_tanh(hid).astype(BF16), w2_ref[...],
                          preferred_element_type=F32).astype(o_ref.dtype)


def _compress(kv, pos, w1, w2, bsz, seq):
    nch = seq // CMP_STRIDE
    half = CMP_STRIDE * NSA_HEAD_DIM
    ch = kv.reshape(bsz, nch, CMP_STRIDE, NSA_GROUPS, NSA_HEAD_DIM)
    ch = jnp.transpose(ch, (0, 3, 1, 2, 4)).reshape(bsz, NSA_GROUPS, nch, half)
    nxt = jnp.concatenate([ch[:, :, 1:], jnp.zeros_like(ch[:, :, :1])], axis=2)
    blk = pl.BlockSpec((1, 1, nch, half), lambda b, g: (b, g, 0, 0))
    return pl.pallas_call(
        _compress_kernel,
        grid=(bsz, NSA_GROUPS),
        in_specs=[
            blk, blk,
            pl.BlockSpec((2, half), lambda b, g: (0, 0)),
            pl.BlockSpec((2 * half, CMP_HIDDEN), lambda b, g: (0, 0)),
            pl.BlockSpec((CMP_HIDDEN, NSA_HEAD_DIM), lambda b, g: (0, 0)),
        ],
        out_specs=pl.BlockSpec((1, 1, nch, NSA_HEAD_DIM), lambda b, g: (b, g, 0, 0)),
        out_shape=jax.ShapeDtypeStruct((bsz, NSA_GROUPS, nch, NSA_HEAD_DIM), BF16),
        compiler_params=_params("parallel", "parallel"),
        name="compress",
    )(ch, nxt, pos.reshape(2, half), w1.astype(BF16), w2.astype(BF16))


NSA_TQ = 512
NSA_KC = 512
MASK_VALUE = -1e30
SOFTMAX_M0 = -1e20


def _nsa_kernel(q_ref, kc_ref, vc_ref, ks_ref, vs_ref, kw_ref, vw_ref, gate_ref,
                overlap_ref, o_ref):
    R, TQ, KC, DH = NSA_REP, NSA_TQ, NSA_KC, NSA_HEAD_DIM
    i = pl.program_id(2)
    q0 = i * TQ
    nt = (((1,), (1,)), ((), ()))
    tcol = q0 + lax.broadcasted_iota(jnp.int32, (TQ, 1), 0)
    qblk = q_ref[...]
    scale = DH ** -0.5
    qf = [qblk[:, r * DH:(r + 1) * DH].astype(F32) * scale for r in range(R)]
    qh = [q.astype(BF16) for q in qf]
    v_width = vs_ref.shape[3]

    def flash_step(m_i, acc, q, k, v_ext, msk=None):
        s = lax.dot_general(q, k, nt, preferred_element_type=F32)
        if msk is not None:
            s = jnp.where(msk, s, MASK_VALUE)
        m_new = jnp.maximum(m_i, jnp.max(s, axis=-1, keepdims=True))
        p = jnp.exp((s - m_new).astype(BF16))
        acc = acc * jnp.exp(m_i - m_new) + jnp.dot(p, v_ext, preferred_element_type=F32)
        return m_new, acc

    def init():
        return tuple((jnp.full((TQ, 1), SOFTMAX_M0, F32), jnp.zeros((TQ, v_width), F32)) for _ in range(R))

    def finish(carry):
        return [acc[:, :DH] / jnp.maximum(acc[:, DH:DH + 1], 1e-30) for _, acc in carry]

    kc = kc_ref[0, 0]
    vc = vc_ref[0, 0]
    ncmp = kc.shape[0]
    cmp_end = lax.broadcasted_iota(jnp.int32, (1, ncmp), 1) * CMP_STRIDE + (CMP_BLOCK - 1)
    cmask = cmp_end <= tcol
    o_cmp = []
    p4 = None
    for r in range(R):
        lg = jnp.where(cmask, lax.dot_general(qh[r], kc, nt, preferred_element_type=F32), -jnp.inf)
        mx = jnp.max(lg, axis=-1, keepdims=True)
        mx = jnp.where(mx > -jnp.inf, mx, 0.0)
        e = jnp.where(cmask, jnp.exp(lg - mx), 0.0)
        p = e / jnp.maximum(jnp.sum(e, axis=-1, keepdims=True), 1e-30)
        o_cmp.append(jnp.dot(p.astype(BF16), vc, preferred_element_type=F32))
        p4 = p if p4 is None else p4 + p

    imp = lax.dot_general(overlap_ref[...], p4, nt, precision=HIGHEST, preferred_element_type=F32)
    nsel = imp.shape[0]
    trow = q0 + lax.broadcasted_iota(jnp.int32, (1, TQ), 1)
    cur = trow // SEL_BLOCK
    jj = lax.broadcasted_iota(jnp.int32, (nsel, 1), 0)
    forced = (jj == 0) | (jj == cur) | (jj == cur - 1)
    val = jnp.where(forced, jnp.inf, jnp.where(jj <= cur, imp, -jnp.inf))
    rank = jnp.zeros((nsel, TQ), jnp.int32)
    for b in range(nsel):
        vb = val[b:b + 1, :]
        beats = (vb > val) | ((vb == val) & (jj > b))
        rank = rank + beats.astype(jnp.int32)
    bias = jnp.transpose(jnp.where(rank < SEL_TOPN, 0.0, MASK_VALUE))
    pad = jnp.zeros((TQ, ks_ref.shape[3] - DH - nsel), F32)
    qa = [jnp.concatenate([qf[r], bias, pad], axis=1).astype(BF16) for r in range(R)]

    def sel_step(kb, carry, causal):
        k0 = pl.multiple_of(kb * KC, KC)
        msk = (k0 + lax.broadcasted_iota(jnp.int32, (1, KC), 1)) <= tcol if causal else None
        k = ks_ref[0, 0, pl.ds(k0, KC), :]
        v = vs_ref[0, 0, pl.ds(k0, KC), :]
        return tuple(flash_step(carry[r][0], carry[r][1], qa[r], k, v, msk) for r in range(R))

    n_full = q0 // KC
    carry = lax.fori_loop(0, n_full, lambda kb, c: sel_step(kb, c, False), init())
    o_sel = finish(sel_step(n_full, carry, True))

    w0 = pl.multiple_of(jnp.maximum(q0 - WINDOW, 0), WINDOW)
    kpos_w = w0 + lax.broadcasted_iota(jnp.int32, (1, WINDOW + TQ), 1)
    msk_w = (kpos_w <= tcol) & (kpos_w > tcol - WINDOW)
    k_w = kw_ref[0, 0, pl.ds(w0, WINDOW + TQ), :]
    v_w = vw_ref[0, 0, pl.ds(w0, WINDOW + TQ), :]
    carry_w = init()
    o_win = finish(tuple(flash_step(carry_w[r][0], carry_w[r][1], qh[r], k_w, v_w, msk_w) for r in range(R)))

    g = _sigmoid(gate_ref[0, 0])
    outs = [g[:, 3 * r:3 * r + 1] * o_cmp[r] + g[:, 3 * r + 1:3 * r + 2] * o_sel[r]
            + g[:, 3 * r + 2:3 * r + 3] * o_win[r] for r in range(R)]
    o_ref[...] = jnp.concatenate(outs, axis=1).astype(o_ref.dtype)


def _nsa(proj_q, gates, kc, vc, bsz, seq):
    G, R, DH, TQ = NSA_GROUPS, NSA_REP, NSA_HEAD_DIM, NSA_TQ
    n_cmp_pad = seq // CMP_STRIDE
    n_sel = seq // SEL_BLOCK
    nq = seq // TQ

    assert NSA_KC % TQ == 0 and seq % NSA_KC == 0 and TQ % WINDOW == 0

    def kv(idx, ext=None):
        lo = 1024 + idx * NSA_KV_WIDTH
        a = jnp.transpose(proj_q[:, lo:lo + NSA_KV_WIDTH].reshape(bsz, seq, G, DH), (0, 2, 1, 3))
        if ext is not None:
            a = jnp.concatenate([a, jnp.broadcast_to(jnp.asarray(ext, BF16), (bsz, G) + ext.shape)], axis=-1)
        return a

    ones_col = np.zeros((seq, LANES - DH), np.float32)
    ones_col[:, 0] = 1.0
    block_onehot = np.zeros((seq, LANES - DH), np.float32)
    block_onehot[np.arange(seq), np.arange(seq) // SEL_BLOCK] = 1.0
    ks, vs, kw, vw = kv(2, block_onehot), kv(3, ones_col), kv(4), kv(5, ones_col)
    gt = jnp.transpose(gates.reshape(bsz, seq, G, R * 3), (0, 2, 1, 3))

    c_start = np.arange(n_cmp_pad) * CMP_STRIDE
    s_start = np.arange(n_sel) * SEL_BLOCK
    overlap = ((c_start[:, None] < s_start[None, :] + SEL_BLOCK)
               & (c_start[:, None] + CMP_BLOCK > s_start[None, :])).astype(np.float32)
    overlap[(seq - CMP_BLOCK) // CMP_STRIDE + 1:] = 0.0
    assert n_sel <= LANES - DH

    qblk = pl.BlockSpec((TQ, R * DH), lambda b, g, i: (b * nq + i, g))
    cblk = pl.BlockSpec((1, 1, n_cmp_pad, DH), lambda b, g, i: (b, g, 0, 0))
    kblk = pl.BlockSpec((1, 1, seq, DH), lambda b, g, i: (b, g, 0, 0))
    vblk = pl.BlockSpec((1, 1, seq, LANES), lambda b, g, i: (b, g, 0, 0))
    return pl.pallas_call(
        _nsa_kernel,
        grid=(bsz, G, nq),
        in_specs=[
            qblk, cblk, cblk, vblk, vblk, kblk, vblk,
            pl.BlockSpec((1, 1, TQ, R * 3), lambda b, g, i: (b, g, i, 0)),
            pl.BlockSpec((n_sel, n_cmp_pad), lambda b, g, i: (0, 0)),
        ],
        out_specs=qblk,
        out_shape=jax.ShapeDtypeStruct((bsz * seq, G * R * DH), BF16),
        compiler_params=_params("parallel", "parallel", "arbitrary"),
        name="nsa_attention",
    )(proj_q, kc, vc, ks, vs, kw, vw, gt, jnp.asarray(overlap.T))


ROW_TILES = D_MODEL // LANES


def _store_token_major(ref, val):
    n = val.shape[0]
    for c in range(ROW_TILES):
        ref[pl.ds(c, n, stride=ROW_TILES), :] = val[:, c * LANES:(c + 1) * LANES]


def _load_token_major(ref, n):
    return jnp.concatenate([ref[pl.ds(c, n, stride=ROW_TILES), :] for c in range(ROW_TILES)], axis=1)


def _merge_kernel(ys_ref, yn_ref, gs_ref, gn_ref, x_ref, wbs_ref, wbn_ref, wo_ref, fw_ref,
                  rw_ref, rb_ref, tri_ref, x1_ref, h_ref, ti_ref, tw_ref, slot_ref, cnt_ref, base_ref):
    @pl.when(pl.program_id(0) == 0)
    def _():
        base_ref[...] = jnp.zeros(base_ref.shape, F32)

    a = jnp.dot(ys_ref[...], wbs_ref[...], preferred_element_type=F32)
    b = jnp.dot(yn_ref[...], wbn_ref[...], preferred_element_type=F32)
    merged = _sigmoid(gs_ref[...]) * a + _sigmoid(gn_ref[...]) * b
    x1 = x_ref[...] + jnp.dot(merged.astype(BF16), wo_ref[...], preferred_element_type=F32)
    x1_ref[...] = x1
    var = jnp.mean(x1 * x1, axis=-1, keepdims=True)
    h = x1 * lax.rsqrt(var + NORM_EPS) * fw_ref[...]
    _store_token_major(h_ref, h)

    logits = jnp.dot(h, rw_ref[...], precision=HIGHEST, preferred_element_type=F32) + rb_ref[...]
    ne = logits.shape[1]
    jj = lax.broadcasted_iota(jnp.int32, (1, ne), 1)
    rank = jnp.zeros(logits.shape, jnp.int32)
    for e in range(ne):
        ce = logits[:, e:e + 1]
        beats = (ce > logits) | ((ce == logits) & (jj > e))
        rank = rank + beats.astype(jnp.int32)
    sel = rank < TOP_K
    mx = jnp.max(logits, axis=-1, keepdims=True)
    p = jnp.where(sel, jnp.exp(logits - mx), 0.0)
    p = p / jnp.sum(p, axis=-1, keepdims=True)

    self01 = sel.astype(F32)
    before = jnp.dot(tri_ref[...], self01.astype(BF16), preferred_element_type=F32) + base_ref[...]
    for k in range(TOP_K):
        hit = rank == k
        ti_ref[:, k:k + 1] = jnp.sum(jnp.where(hit, jj, 0), axis=-1, keepdims=True)
        tw_ref[:, k:k + 1] = jnp.sum(jnp.where(hit, p, 0.0), axis=-1, keepdims=True)
        slot_ref[:, k:k + 1] = jnp.sum(jnp.where(hit, before, 0.0), axis=-1, keepdims=True).astype(jnp.int32)
    base_ref[...] = base_ref[...] + jnp.sum(self01, axis=0, keepdims=True)
    cnt_ref[...] = base_ref[...]


def _merge_route(y_ssm, y_nsa, proj_f, x2, wbs, wbn, wo, ffn_w, router_w, router_b, tm):
    t, d = x2.shape
    row = lambda col: pl.BlockSpec((tm, d), lambda i: (i, col))
    const = lambda shape: pl.BlockSpec(shape, lambda i: (0,) * len(shape))
    kblk = pl.BlockSpec((tm, TOP_K), lambda i: (i, 0))
    tri = np.tril(np.ones((tm, tm), np.float32), -1)
    return pl.pallas_call(
        _merge_kernel,
        grid=(t // tm,),
        in_specs=[
            row(0), row(0), row(F_MS // d), row(F_MN // d), row(0),
            const((d, d)), const((d, d)), const((d, d)), const((1, d)),
            const((d, N_EXPERTS)), const((1, N_EXPERTS)), const((tm, tm)),
        ],
        out_specs=[row(0), pl.BlockSpec((tm * ROW_TILES, LANES), lambda i: (i, 0)),
                   kblk, kblk, kblk, const((1, N_EXPERTS))],
        out_shape=[jax.ShapeDtypeStruct((t, d), F32), jax.ShapeDtypeStruct((t * ROW_TILES, LANES), F32),
                   jax.ShapeDtypeStruct((t, TOP_K), jnp.int32), jax.ShapeDtypeStruct((t, TOP_K), F32),
                   jax.ShapeDtypeStruct((t, TOP_K), jnp.int32), jax.ShapeDtypeStruct((1, N_EXPERTS), F32)],
        scratch_shapes=[pltpu.VMEM((1, N_EXPERTS), F32)],
        compiler_params=_params("arbitrary"),
        name="merge_route",
    )(y_ssm, y_nsa, proj_f, proj_f, x2, wbs.astype(BF16), wbn.astype(BF16), wo.astype(BF16),
      ffn_w.reshape(1, d), router_w, router_b.reshape(1, N_EXPERTS), jnp.asarray(tri, dtype=BF16))


GU_CHUNK = 2 * LANES


def _deinterleave_kernel(w_ref, perm_ref, o_ref):
    o_ref[0] = jnp.dot(w_ref[0].astype(BF16), perm_ref[...], preferred_element_type=F32).astype(o_ref.dtype)


def _deinterleave(w):
    e, d, n = w.shape
    perm = np.zeros((GU_CHUNK, GU_CHUNK), np.float32)
    perm[2 * np.arange(LANES), np.arange(LANES)] = 1.0
    perm[2 * np.arange(LANES) + 1, LANES + np.arange(LANES)] = 1.0
    blk = pl.BlockSpec((1, d, GU_CHUNK), lambda i, j: (i, 0, j))
    return pl.pallas_call(
        _deinterleave_kernel,
        grid=(e, n // GU_CHUNK),
        in_specs=[blk, pl.BlockSpec((GU_CHUNK, GU_CHUNK), lambda i, j: (0, 0))],
        out_specs=blk,
        out_shape=jax.ShapeDtypeStruct((e, d, n), BF16),
        compiler_params=_params("parallel", "parallel"),
        name="deinterleave",
    )(w, jnp.asarray(perm, dtype=BF16))


MOE_TM = 256


def _moe_kernel(te_ref, nu_ref, tok0_ref, tokn_ref, dst_ref, h_hbm, wgu_ref, bgu_ref, wd_ref, bd_ref,
                y_hbm, xbuf, ybuf, gsem, ssem, *, spare_row):
    tm, rt = MOE_TM, ROW_TILES
    i = pl.program_id(0)
    nu = nu_ref[0]
    slot = i % 2
    other = 1 - slot

    def gather_copy(src_row, r, sl):
        return pltpu.make_async_copy(h_hbm.at[pl.ds(src_row, rt), :], xbuf.at[sl, pl.ds(r * rt, rt), :],
                                     gsem.at[sl])

    def scatter_copy(dst_row, r, sl):
        return pltpu.make_async_copy(ybuf.at[sl, pl.ds(r * rt, rt), :], y_hbm.at[pl.ds(dst_row, rt), :],
                                     ssem.at[sl])

    def start_gather(tok_ref, sl):
        for r in range(tm):
            gather_copy(pl.multiple_of(tok_ref[0, 0, r], rt), r, sl).start()

    def start_scatter(sl):
        for r in range(tm):
            scatter_copy(pl.multiple_of(dst_ref[0, 0, r], rt), r, sl).start()

    def wait_gather(sl):
        pltpu.make_async_copy(h_hbm.at[pl.ds(0, tm * rt), :], xbuf.at[sl], gsem.at[sl]).wait()

    def wait_scatter(sl):
        pltpu.make_async_copy(ybuf.at[sl], y_hbm.at[pl.ds(0, tm * rt), :], ssem.at[sl]).wait()

    @pl.when(i == 0)
    def _():
        ybuf[...] = jnp.zeros(ybuf.shape, ybuf.dtype)
        start_gather(tok0_ref, 0)
        for r in range(tm):
            scatter_copy((spare_row + r) * rt, r, 0).start()

    @pl.when(i < nu)
    def _():
        wait_gather(slot)
        start_gather(tokn_ref, other)
        wait_scatter(slot)
        start_scatter(other)
        x =_load_token_major(xbuf.at[slot], tm).astype(BF16)
        h1 = jnp.dot(x, wgu_ref[0], preferred_element_type=F32) + bgu_ref[0]
        nchunk = h1.shape[1] // GU_CHUNK
        glu = jnp.concatenate([h1[:, c * GU_CHUNK:c * GU_CHUNK + LANES] for c in range(nchunk)], axis=1)
        lin = jnp.concatenate([h1[:, c * GU_CHUNK + LANES:(c + 1) * GU_CHUNK] for c in range(nchunk)], axis=1)
        glu = jnp.minimum(glu, SWIGLU_LIMIT)
        lin = jnp.clip(lin, -SWIGLU_LIMIT, SWIGLU_LIMIT)
        act = glu * _sigmoid(SWIGLU_ALPHA * glu) * (lin + 1.0)
        y = jnp.dot(act.astype(BF16), wd_ref[0], preferred_element_type=F32) + bd_ref[0]
        _store_token_major(ybuf.at[slot], y)

    @pl.when(i == nu)
    def _():
        wait_gather(slot)
        wait_scatter(slot)
        start_scatter(other)
        wait_scatter(other)


def _moe_experts(h_rows, tok_tiles, dst_tiles, tile_expert, n_used, wgu, bgu, wd, bd, n_out_rows, spare_row):
    d = wd.shape[1]
    tm, rt = MOE_TM, ROW_TILES
    n_tiles = tok_tiles.shape[0]
    smem = lambda f: pl.BlockSpec((1, 1, tm), f, memory_space=pltpu.SMEM)
    grid_spec = pltpu.PrefetchScalarGridSpec(
        num_scalar_prefetch=2,
        grid=(n_tiles,),
        in_specs=[
            smem(lambda i, te, nu: (0, 0, 0)),
            smem(lambda i, te, nu: (jnp.minimum(i + 1, n_tiles - 1), 0, 0)),
            smem(lambda i, te, nu: (i, 0, 0)),
            pl.BlockSpec(memory_space=pl.ANY),
            pl.BlockSpec((1, d, 2 * d), lambda i, te, nu: (te[i], 0, 0)),
            pl.BlockSpec((1, 1, 2 * d), lambda i, te, nu: (te[i], 0, 0)),
            pl.BlockSpec((1, d, d), lambda i, te, nu: (te[i], 0, 0)),
            pl.BlockSpec((1, 1, d), lambda i, te, nu: (te[i], 0, 0)),
        ],
        out_specs=pl.BlockSpec(memory_space=pl.ANY),
        scratch_shapes=[
            pltpu.VMEM((2, tm * rt, LANES), F32),
            pltpu.VMEM((2, tm * rt, LANES), F32),
            pltpu.SemaphoreType.DMA((2,)),
            pltpu.SemaphoreType.DMA((2,)),
        ],
    )
    return pl.pallas_call(
        functools.partial(_moe_kernel, spare_row=spare_row),
        grid_spec=grid_spec,
        out_shape=jax.ShapeDtypeStruct((n_out_rows * rt, LANES), F32),
        compiler_params=_params("arbitrary"),
        name="moe_experts",
    )(tile_expert, n_used, tok_tiles, tok_tiles, dst_tiles, h_rows, wgu, bgu, wd, bd)


def _final_kernel(x1_ref, y0_ref, y1_ref, y2_ref, y3_ref, tw_ref, nw_ref, o_ref):
    tm = x1_ref.shape[0]
    tw = tw_ref[...]
    yk = [_load_token_major(r, tm) for r in (y0_ref, y1_ref, y2_ref, y3_ref)]
    moe = ((tw[:, 0:1] * yk[0] + tw[:, 1:2] * yk[1]) + (tw[:, 2:3] * yk[2] + tw[:, 3:4] * yk[3]))
    x = x1_ref[...] + moe
    var = jnp.mean(x * x, axis=-1, keepdims=True)
    o_ref[...] = x * lax.rsqrt(var + NORM_EPS) * nw_ref[...]


def _final_norm(x1, y_rows, top_w, norm_w, tm):
    t, d = x1.shape
    nt = t // tm
    yblk = lambda k: pl.BlockSpec((tm * ROW_TILES, LANES), lambda i: (k * nt + i, 0))
    return pl.pallas_call(
        _final_kernel,
        grid=(nt,),
        in_specs=[
            pl.BlockSpec((tm, d), lambda i: (i, 0)),
            yblk(0), yblk(1), yblk(2), yblk(3),
            pl.BlockSpec((tm, TOP_K), lambda i: (i, 0)),
            pl.BlockSpec((1, d), lambda i: (0, 0)),
        ],
        out_specs=pl.BlockSpec((tm, d), lambda i: (i, 0)),
        out_shape=jax.ShapeDtypeStruct((t, d), F32),
        compiler_params=_params("parallel"),
        name="final_norm",
    )(x1, y_rows, y_rows, y_rows, y_rows, top_w, norm_w.reshape(1, d))


def _dispatch_tables(top_i, slot, counts, n_rows):
    t = top_i.shape[0]
    n = t * TOP_K
    tm, rt = MOE_TM, ROW_TILES
    padded = ((counts + tm - 1) // tm) * tm
    pend = jnp.cumsum(padded)
    pstart = pend - padded
    onehot = top_i[:, :, None] == jnp.arange(N_EXPERTS, dtype=jnp.int32)
    dest = slot + jnp.sum(jnp.where(onehot, pstart, 0), axis=-1)
    row_pair = jnp.full((n_rows,), -1, jnp.int32).at[dest.reshape(-1)].set(jnp.arange(n, dtype=jnp.int32))
    live = row_pair >= 0
    tok_tiles = (jnp.where(live, row_pair // TOP_K, 0) * rt).reshape(n_rows // tm, 1, tm)
    rows = jnp.arange(n_rows, dtype=jnp.int32)
    out_row = jnp.where(live, (row_pair % TOP_K) * t + row_pair // TOP_K, n + tm + rows % tm)
    spare = n + tm + jnp.arange(tm, dtype=jnp.int32)
    dst_tiles = (jnp.concatenate([spare, out_row]) * rt).reshape(n_rows // tm + 1, 1, tm)
    tile_start = jnp.arange(n_rows // tm, dtype=jnp.int32) * tm
    tile_expert = jnp.minimum(jnp.sum(tile_start[:, None] >= pend[None, :], axis=-1), N_EXPERTS - 1)
    n_used = (pend[-1] // tm).reshape(1)
    return tok_tiles, dst_tiles, tile_expert.astype(jnp.int32), n_used.astype(jnp.int32)


def _in_proj_weights(w_in):
    o = np.cumsum([0, D_INNER, XBC_DIM, SSM_HEADS, 1024, 256, 256, 256, 256, 256, 256, 3 * NSA_HEADS, 2 * D_MODEL])
    z, xbc, dt, q, kv, gate, mg = (w_in[:, o[0]:o[1]], w_in[:, o[1]:o[2]], w_in[:, o[2]:o[3]],
                                   w_in[:, o[3]:o[4]], w_in[:, o[4]:o[10]], w_in[:, o[10]:o[11]],
                                   w_in[:, o[11]:o[12]])
    small_pad = jnp.zeros((D_MODEL, F_WIDTH - F_SMALL - SSM_HEADS - 3 * NSA_HEADS), w_in.dtype)
    w_f = jnp.concatenate([z, mg[:, :D_MODEL], xbc, mg[:, D_MODEL:], dt, gate, small_pad], axis=1)
    w_q = jnp.concatenate([q, kv], axis=1)
    return w_f.astype(BF16), w_q.astype(BF16)


def kernel(x, mix_norm_w, w_in, conv_w, conv_b, dt_bias, a_log, d_skip, ssm_norm_w, cmp_pos_k, cmp_w1_k, cmp_w2_k, cmp_pos_v, cmp_w1_v, cmp_w2_v, w_branch_ssm, w_branch_nsa, w_out, ffn_norm_w, router_w, router_b, w_gate_up, b_gate_up, w_down, b_down, final_norm_w):
    bsz, seq, d = x.shape
    t = bsz * seq
    x2 = x.reshape(t, d)
    depth = w_in.shape[0]
    assert depth == 1, "single-layer block"
    for l in range(depth):
        w_f, w_q = _in_proj_weights(w_in[l])
        proj_f = _norm_matmul(x2, mix_norm_w[l], w_f, F32, 1024, 768)
        proj_q = _norm_matmul(x2, mix_norm_w[l], w_q, BF16, 1024, 512)

        small = proj_f[:, F_SMALL:F_SMALL + LANES]
        dt_t = jnp.transpose(small[:, :SSM_HEADS].reshape(bsz, seq, SSM_HEADS), (0, 2, 1))
        y_ssm = _ssd(proj_f, dt_t, conv_w[l], conv_b[l], dt_bias[l], a_log[l], d_skip[l], ssm_norm_w[l],
                     bsz, seq)

        kc = _compress(proj_q[:, 1024:1280], cmp_pos_k[l], cmp_w1_k[l], cmp_w2_k[l], bsz, seq)
        vc = _compress(proj_q[:, 1280:1536], cmp_pos_v[l], cmp_w1_v[l], cmp_w2_v[l], bsz, seq)
        gates = small[:, SSM_HEADS:SSM_HEADS + 3 * NSA_HEADS]
        y_nsa = _nsa(proj_q, gates, kc, vc, bsz, seq)

        x1, h, top_i, top_w, slot, counts = _merge_route(
            y_ssm, y_nsa, proj_f, x2, w_branch_ssm[l], w_branch_nsa[l], w_out[l], ffn_norm_w[l],
            router_w[l], router_b[l], 512)

        n_rows = t * TOP_K + N_EXPERTS * MOE_TM
        tok_tiles, dst_tiles, tile_expert, n_used = _dispatch_tables(
            top_i, slot, counts[0].astype(jnp.int32), n_rows)
        wgu = _deinterleave(w_gate_up[l])
        nchunk = 2 * d // GU_CHUNK
        bgu = jnp.transpose(b_gate_up[l].reshape(N_EXPERTS, nchunk, LANES, 2), (0, 1, 3, 2))
        y_rows = _moe_experts(h, tok_tiles, dst_tiles, tile_expert, n_used, wgu,
                              bgu.reshape(N_EXPERTS, 1, 2 * d), w_down[l].astype(BF16), b_down[l][:, None, :],
                              n_out_rows=t * TOP_K + 2 * MOE_TM, spare_row=t * TOP_K)
    out = _final_norm(x1, y_rows, top_w, final_norm_w, 512)
    return out.reshape(bsz, seq, d)
```

```python
import functools

import numpy as np
import jax
import jax.numpy as jnp
from jax import lax
from jax.experimental import pallas as pl
from jax.experimental.pallas import tpu as pltpu

F32 = jnp.float32
BF16 = jnp.bfloat16

D_MODEL = 1024
D_INNER = 1024
SSM_HEAD_DIM = 64
SSM_HEADS = 16
SSM_GROUPS = 4
D_STATE = 128
CONV_K = 4
XBC_DIM = D_INNER + 2 * SSM_GROUPS * D_STATE
SSD_CHUNK = 128
NSA_HEAD_DIM = 64
NSA_HEADS = 16
NSA_GROUPS = 4
NSA_REP = 4
NSA_KV_WIDTH = 256
CMP_BLOCK = 32
CMP_STRIDE = 16
CMP_HIDDEN = 128
SEL_BLOCK = 64
SEL_TOPN = 8
WINDOW = 256
N_EXPERTS = 32
TOP_K = 4
SWIGLU_ALPHA = 1.702
SWIGLU_LIMIT = 7.0
NORM_EPS = 1e-5

LANES = 128
VMEM_LIMIT = 56 * 1024 * 1024

F_Z, F_MS, F_XBC, F_MN, F_SMALL = 0, 1024, 2048, 4096, 5120
F_WIDTH = 5376
Q_WIDTH = 1024 + 6 * NSA_KV_WIDTH


def _sigmoid(x):
    return 0.5 * jnp.tanh(0.5 * x) + 0.5


def _silu(x):
    return x * _sigmoid(x)


def _softplus(x):
    return jnp.maximum(x, 0.0) + jnp.log1p(jnp.exp(-jnp.abs(x)))


def _split3(x):
    hi = x.astype(BF16)
    r1 = x - hi.astype(F32)
    mid = r1.astype(BF16)
    lo = (r1 - mid.astype(F32)).astype(BF16)
    return hi, mid, lo


def _params(*sem):
    return pltpu.CompilerParams(dimension_semantics=sem, vmem_limit_bytes=VMEM_LIMIT)


def _norm_matmul_kernel(x_ref, nw_ref, w_ref, o_ref, h_ref):
    @pl.when(pl.program_id(1) == 0)
    def _():
        x = x_ref[...]
        var = jnp.mean(x * x, axis=-1, keepdims=True)
        h_ref[...] = (x * lax.rsqrt(var + NORM_EPS) * nw_ref[...]).astype(BF16)

    o_ref[...] = jnp.dot(h_ref[...], w_ref[...], preferred_element_type=F32).astype(o_ref.dtype)


def _norm_matmul(x2, norm_w, w, out_dtype, tm, tn):
    t, d = x2.shape
    n = w.shape[1]
    return pl.pallas_call(
        _norm_matmul_kernel,
        grid=(t // tm, n // tn),
        in_specs=[
            pl.BlockSpec((tm, d), lambda i, j: (i, 0)),
            pl.BlockSpec((1, d), lambda i, j: (0, 0)),
            pl.BlockSpec((d, tn), lambda i, j: (0, j)),
        ],
        out_specs=pl.BlockSpec((tm, tn), lambda i, j: (i, j)),
        out_shape=jax.ShapeDtypeStruct((t, n), out_dtype),
        scratch_shapes=[pltpu.VMEM((tm, d), BF16)],
        compiler_params=_params("parallel", "arbitrary"),
        name="norm_matmul",
    )(x2, norm_w.reshape(1, d), w)


def _ssd_kernel(z_ref, xbc_ref, sm_ref, dtt_ref, convw_ref, convb_ref, dtb_ref, dtbt_ref,
                alog_ref, alogt_ref, dskip_ref, nw_ref, expand_ref, y_ref,
                xbuf, state, ydiag):
    L = SSD_CHUNK
    c = pl.program_id(1)

    @pl.when(c == 0)
    def _():
        xbuf[0:8, :] = jnp.zeros((8, XBC_DIM), F32)
        state[...] = jnp.zeros(state.shape, F32)

    xbuf[8:8 + L, :] = xbc_ref[...]
    acc = convb_ref[...] + convw_ref[0:1, :] * xbuf[5:5 + L, :]
    for k in range(1, CONV_K):
        acc = acc + convw_ref[k:k + 1, :] * xbuf[5 + k:5 + k + L, :]
    xbuf[0:8, :] = xbuf[L:L + 8, :]
    xbc = _silu(acc)
    xs = xbc[:, :D_INNER]

    lane = lax.broadcasted_iota(jnp.int32, (1, LANES), 1)
    a_row = jnp.where(lane < SSM_HEADS, -jnp.exp(alog_ref[...]), 0.0)
    dt = _softplus(sm_ref[...] + dtb_ref[...])
    a_dt = dt * a_row
    dt_t = _softplus(dtt_ref[0] + dtbt_ref[...])
    a_dt_t = dt_t * (-jnp.exp(alogt_ref[...]))

    row = lax.broadcasted_iota(jnp.int32, (L, L), 0)
    col = lax.broadcasted_iota(jnp.int32, (L, L), 1)
    lower = row >= col
    tri = lower.astype(BF16)
    tri_t = (row <= col).astype(BF16)
    cs = sum(jnp.dot(tri, part, preferred_element_type=F32) for part in _split3(a_dt))
    cs_t = sum(jnp.dot(part, tri_t, preferred_element_type=F32) for part in _split3(a_dt_t))
    cs_last = cs[L - 1:L, :]

    stacked = jnp.concatenate([dt, jnp.exp(cs_last - cs), jnp.exp(cs)], axis=0)
    wide = sum(jnp.dot(part, expand_ref[...], preferred_element_type=F32) for part in _split3(stacked))
    dt_x = wide[0:L]
    dte_x = wide[L:2 * L]
    ecs_x = wide[2 * L:3 * L]

    xdt = xs * dt_x
    xw = (xdt * dte_x).astype(BF16)
    xdt_b = xdt.astype(BF16)

    hpg = SSM_HEADS // SSM_GROUPS
    gw = hpg * SSM_HEAD_DIM
    y_off_parts = []
    for g in range(SSM_GROUPS):
        b_g = xbc[:, D_INNER + g * D_STATE:D_INNER + (g + 1) * D_STATE].astype(BF16)
        c_g = xbc[:, D_INNER + SSM_GROUPS * D_STATE + g * D_STATE:
                  D_INNER + SSM_GROUPS * D_STATE + (g + 1) * D_STATE].astype(BF16)
        cb = lax.dot_general(c_g, b_g, (((1,), (1,)), ((), ())), preferred_element_type=F32)
        for hh in range(hpg):
            h = g * hpg + hh
            seg = cs[:, h:h + 1] - cs_t[h:h + 1, :]
            decay = jnp.exp(jnp.where(lower, seg, -jnp.inf))
            m = (cb * decay).astype(BF16)
            ydiag[:, h * SSM_HEAD_DIM:(h + 1) * SSM_HEAD_DIM] = jnp.dot(
                m, xdt_b[:, h * SSM_HEAD_DIM:(h + 1) * SSM_HEAD_DIM], preferred_element_type=F32)
        st_prev = state[g]
        y_off_parts.append(jnp.dot(c_g, st_prev.astype(BF16), preferred_element_type=F32))
        st_new = lax.dot_general(b_g, xw[:, g * gw:(g + 1) * gw], (((0,), (0,)), ((), ())),
                                 preferred_element_type=F32)
        state[g] = st_prev * ecs_x[L - 1:L, g * gw:(g + 1) * gw] + st_new
    y_off = jnp.concatenate(y_off_parts, axis=1) * ecs_x

    y = (ydiag[...] + y_off + dskip_ref[...] * xs) * _silu(z_ref[...])
    for g in range(SSM_GROUPS):
        yg = y[:, g * gw:(g + 1) * gw]
        yg = yg * lax.rsqrt(jnp.mean(yg * yg, axis=-1, keepdims=True) + NORM_EPS)
        y_ref[:, g * gw:(g + 1) * gw] = (yg * nw_ref[:, g * gw:(g + 1) * gw]).astype(y_ref.dtype)


def _ssd(proj_f, dt_t, conv_w, conv_b, dt_bias, a_log, d_skip, norm_w, bsz, seq):
    L = SSD_CHUNK
    nc = seq // L
    pad = LANES - SSM_HEADS
    dtb = jnp.pad(dt_bias, (0, pad)).reshape(1, LANES)
    alog = jnp.pad(a_log, (0, pad)).reshape(1, LANES)
    dskip_x = jnp.repeat(d_skip, SSM_HEAD_DIM).reshape(1, D_INNER)
    expand = (np.arange(LANES)[:, None] == (np.arange(D_INNER)[None, :] // SSM_HEAD_DIM)).astype(np.float32)
    const = lambda shape: pl.BlockSpec(shape, lambda b, c: (0,) * len(shape))
    return pl.pallas_call(
        _ssd_kernel,
        grid=(bsz, nc),
        in_specs=[
            pl.BlockSpec((L, D_INNER), lambda b, c: (b * nc + c, F_Z // D_INNER)),
            pl.BlockSpec((L, XBC_DIM), lambda b, c: (b * nc + c, F_XBC // XBC_DIM)),
            pl.BlockSpec((L, LANES), lambda b, c: (b * nc + c, F_SMALL // LANES)),
            pl.BlockSpec((1, SSM_HEADS, L), lambda b, c: (b, 0, c)),
            const((CONV_K, XBC_DIM)), const((1, XBC_DIM)),
            const((1, LANES)), const((SSM_HEADS, 1)),
            const((1, LANES)), const((SSM_HEADS, 1)),
            const((1, D_INNER)), const((1, D_INNER)),
            const((LANES, D_INNER)),
        ],
        out_specs=pl.BlockSpec((L, D_INNER), lambda b, c: (b * nc + c, 0)),
        out_shape=jax.ShapeDtypeStruct((bsz * seq, D_INNER), BF16),
        scratch_shapes=[
            pltpu.VMEM((L + 8, XBC_DIM), F32),
            pltpu.VMEM((SSM_GROUPS, D_STATE, (SSM_HEADS // SSM_GROUPS) * SSM_HEAD_DIM), F32),
            pltpu.VMEM((L, D_INNER), F32),
        ],
        compiler_params=_params("parallel", "arbitrary"),
        name="ssd",
    )(proj_f, proj_f, proj_f, dt_t, conv_w, conv_b.reshape(1, XBC_DIM),
      dtb, dt_bias.reshape(SSM_HEADS, 1), alog, a_log.reshape(SSM_HEADS, 1),
      dskip_x, norm_w.reshape(1, D_INNER), jnp.asarray(expand, dtype=BF16))


def _gelu_tanh(x):
    return 0.5 * x * (1.0 + jnp.tanh(np.sqrt(2.0 / np.pi) * (x + 0.044715 * (x * x * x))))


def _compress_kernel(cur_ref, nxt_ref, pos_ref, w1_ref, w2_ref, o_ref):
    half = CMP_STRIDE * NSA_HEAD_DIM
    lo = (cur_ref[0, 0].astype(F32) + pos_ref[0:1, :]).astype(BF16)
    hi = (nxt_ref[0, 0].astype(F32) + pos_ref[1:2, :]).astype(BF16)
    hid = (jnp.dot(lo, w1_ref[0:half, :], preferred_element_type=F32)
           + jnp.dot(hi, w1_ref[half:2 * half, :], preferred_element_type=F32))
    o_ref[0, 0] = jnp.dot(_gelu_tanh(hid).astype(BF16), w2_ref[...],
                          preferred_element_type=F32).astype(o_ref.dtype)


def _compress(kv, pos, w1, w2, bsz, seq):
    nch = seq // CMP_STRIDE
    half = CMP_STRIDE * NSA_HEAD_DIM
    ch = kv.reshape(bsz, nch, CMP_STRIDE, NSA_GROUPS, NSA_HEAD_DIM)
    ch = jnp.transpose(ch, (0, 3, 1, 2, 4)).reshape(bsz, NSA_GROUPS, nch, half)
    nxt = jnp.concatenate([ch[:, :, 1:], jnp.zeros_like(ch[:, :, :1])], axis=2)
    blk = pl.BlockSpec((1, 1, nch, half), lambda b, g: (b, g, 0, 0))
    return pl.pallas_call(
        _compress_kernel,
        grid=(bsz, NSA_GROUPS),
        in_specs=[
            blk, blk,
            pl.BlockSpec((2, half), lambda b, g: (0, 0)),
            pl.BlockSpec((2 * half, CMP_HIDDEN), lambda b, g: (0, 0)),
            pl.BlockSpec((CMP_HIDDEN, NSA_HEAD_DIM), lambda b, g: (0, 0)),
        ],
        out_specs=pl.BlockSpec((1, 1, nch, NSA_HEAD_DIM), lambda b, g: (b, g, 0, 0)),
        out_shape=jax.ShapeDtypeStruct((bsz, NSA_GROUPS, nch, NSA_HEAD_DIM), BF16),
        compiler_params=_params("parallel", "parallel"),
        name="compress",
    )(ch, nxt, pos.reshape(2, half), w1.astype(BF16), w2.astype(BF16))


NSA_TQ = 512
NSA_KC = 512
MASK_VALUE = -1e30
SOFTMAX_M0 = -1e20


def _nsa_kernel(q_ref, kc_ref, vc_ref, ks_ref, vs_ref, kw_ref, vw_ref, gate_ref,
                overlap_ref, gexp_ref, o_ref):
    R, TQ, KC, DH = NSA_REP, NSA_TQ, NSA_KC, NSA_HEAD_DIM
    assert 2 * DH == LANES
    i = pl.program_id(2)
    q0 = i * TQ
    nt = (((1,), (1,)), ((), ()))
    tcol = q0 + lax.broadcasted_iota(jnp.int32, (TQ, 1), 0)
    qblk = q_ref[...]
    scale = DH ** -0.5
    qf = [qblk[:, r * DH:(r + 1) * DH].astype(F32) * scale for r in range(R)]
    qh = [q.astype(BF16) for q in qf]
    v_width = vs_ref.shape[3]

    def flash_step(m_i, acc, q, k, v_ext, mask_bias=None):
        s = lax.dot_general(q, k, nt, preferred_element_type=F32)
        if mask_bias is not None:
            s = s + mask_bias
        m_new = jnp.maximum(m_i, jnp.max(s, axis=-1, keepdims=True))
        p = jnp.exp((s - m_new).astype(BF16))
        acc = acc * jnp.exp(m_i - m_new) + jnp.dot(p, v_ext, preferred_element_type=F32)
        return m_new, acc

    def init():
        return tuple((jnp.full((TQ, 1), SOFTMAX_M0, F32), jnp.zeros((TQ, v_width), F32)) for _ in range(R))

    def spread(x, mat):
        hi = x.astype(BF16)
        lo = (x - hi.astype(F32)).astype(BF16)
        return jnp.dot(hi, mat, preferred_element_type=F32) + jnp.dot(lo, mat, preferred_element_type=F32)

    kc = kc_ref[0, 0]
    vc = vc_ref[0, 0]
    ncmp = kc.shape[0]
    cmp_end = lax.broadcasted_iota(jnp.int32, (1, ncmp), 1) * CMP_STRIDE + (CMP_BLOCK - 1)
    cmask = cmp_end <= tcol
    o_cmp = []
    p4 = None
    for r in range(R):
        lg = jnp.where(cmask, lax.dot_general(qh[r], kc, nt, preferred_element_type=F32), -jnp.inf)
        mx = jnp.max(lg, axis=-1, keepdims=True)
        mx = jnp.where(mx > -jnp.inf, mx, 0.0)
        e = jnp.where(cmask, jnp.exp(lg - mx), 0.0)
        p = e / jnp.maximum(jnp.sum(e, axis=-1, keepdims=True), 1e-30)
        o_cmp.append(jnp.dot(p.astype(BF16), vc, preferred_element_type=F32))
        p4 = p if p4 is None else p4 + p

    imp = sum(lax.dot_general(overlap_ref[...], part, nt, preferred_element_type=F32)
              for part in _split3(p4))
    nsel = imp.shape[0]
    trow = q0 + lax.broadcasted_iota(jnp.int32, (1, TQ), 1)
    cur = trow // SEL_BLOCK
    jj = lax.broadcasted_iota(jnp.int32, (nsel, 1), 0)
    forced = (jj == 0) | (jj == cur) | (jj == cur - 1)
    val = jnp.where(forced, jnp.inf, jnp.where(jj <= cur, imp, -jnp.inf))
    rank = jnp.zeros((nsel, TQ), jnp.int32)
    for b in range(nsel):
        vb = val[b:b + 1, :]
        beats = (vb > val) | ((vb == val) & (jj > b))
        rank = rank + beats.astype(jnp.int32)
    bias = jnp.transpose(jnp.where(rank < SEL_TOPN, 0.0, MASK_VALUE))
    pad = jnp.zeros((TQ, ks_ref.shape[3] - DH - nsel), F32)
    qa = [jnp.concatenate([qf[r], bias, pad], axis=1).astype(BF16) for r in range(R)]

    def sel_step(kb, carry, mask_bias):
        k0 = pl.multiple_of(kb * KC, KC)
        k = ks_ref[0, 0, pl.ds(k0, KC), :]
        v = vs_ref[0, 0, pl.ds(k0, KC), :]
        return tuple(flash_step(carry[r][0], carry[r][1], qa[r], k, v, mask_bias) for r in range(R))

    n_full = q0 // KC
    carry = lax.fori_loop(0, n_full, lambda kb, c: sel_step(kb, c, None), init())
    kpos_d = n_full * KC + lax.broadcasted_iota(jnp.int32, (1, KC), 1)
    carry_s = sel_step(n_full, carry, jnp.where(kpos_d <= tcol, 0.0, MASK_VALUE))

    w0 = pl.multiple_of(jnp.maximum(q0 - WINDOW, 0), WINDOW)
    kpos_w = w0 + lax.broadcasted_iota(jnp.int32, (1, WINDOW + TQ), 1)
    bias_w = jnp.where((kpos_w <= tcol) & (kpos_w > tcol - WINDOW), 0.0, MASK_VALUE)
    k_w = kw_ref[0, 0, pl.ds(w0, WINDOW + TQ), :]
    v_w = vw_ref[0, 0, pl.ds(w0, WINDOW + TQ), :]
    carry_w = init()
    carry_w = tuple(flash_step(carry_w[r][0], carry_w[r][1], qh[r], k_w, v_w, bias_w) for r in range(R))

    g = _sigmoid(gate_ref[0, 0])
    g = jnp.concatenate([g, jnp.zeros((TQ, LANES - g.shape[1]), F32)], axis=1)
    gx = spread(g, gexp_ref[...])

    def normalised(branch):
        return jnp.concatenate([(acc / jnp.maximum(pltpu.roll(acc, DH, 1), 1e-30))[:, :DH] for _, acc in branch],
                               axis=1)

    w = R * DH
    out = (gx[:, 0:w] * jnp.concatenate(o_cmp, axis=1) + gx[:, w:2 * w] * normalised(carry_s)
           + gx[:, 2 * w:3 * w] * normalised(carry_w))
    o_ref[...] = out.astype(o_ref.dtype)


def _nsa(proj_q, gates, kc, vc, bsz, seq):
    G, R, DH, TQ = NSA_GROUPS, NSA_REP, NSA_HEAD_DIM, NSA_TQ
    n_cmp_pad = seq // CMP_STRIDE
    n_sel = seq // SEL_BLOCK
    nq = seq // TQ

    assert NSA_KC % TQ == 0 and seq % NSA_KC == 0 and TQ % WINDOW == 0

    def kv(idx, ext=None):
        lo = 1024 + idx * NSA_KV_WIDTH
        a = jnp.transpose(proj_q[:, lo:lo + NSA_KV_WIDTH].reshape(bsz, seq, G, DH), (0, 2, 1, 3))
        if ext is not None:
            a = jnp.concatenate([a, jnp.broadcast_to(jnp.asarray(ext, BF16), (bsz, G) + ext.shape)], axis=-1)
        return a

    ones_col = np.ones((seq, LANES - DH), np.float32)
    block_onehot = np.zeros((seq, LANES - DH), np.float32)
    block_onehot[np.arange(seq), np.arange(seq) // SEL_BLOCK] = 1.0
    ks, vs, kw, vw = kv(2, block_onehot), kv(3, ones_col), kv(4), kv(5, ones_col)
    gt = jnp.transpose(gates.reshape(bsz, seq, G, R * 3), (0, 2, 1, 3))

    c_start = np.arange(n_cmp_pad) * CMP_STRIDE
    s_start = np.arange(n_sel) * SEL_BLOCK
    overlap = ((c_start[:, None] < s_start[None, :] + SEL_BLOCK)
               & (c_start[:, None] + CMP_BLOCK > s_start[None, :])).astype(np.float32)
    overlap[(seq - CMP_BLOCK) // CMP_STRIDE + 1:] = 0.0
    assert n_sel <= LANES - DH
    col = np.arange(3 * R * DH)
    gexp = (np.arange(LANES)[:, None] == (3 * ((col % (R * DH)) // DH) + col // (R * DH))[None, :]).astype(np.float32)

    qblk = pl.BlockSpec((TQ, R * DH), lambda b, g, i: (b * nq + i, g))
    cblk = pl.BlockSpec((1, 1, n_cmp_pad, DH), lambda b, g, i: (b, g, 0, 0))
    kblk = pl.BlockSpec((1, 1, seq, DH), lambda b, g, i: (b, g, 0, 0))
    vblk = pl.BlockSpec((1, 1, seq, LANES), lambda b, g, i: (b, g, 0, 0))
    return pl.pallas_call(
        _nsa_kernel,
        grid=(bsz, G, nq),
        in_specs=[
            qblk, cblk, cblk, vblk, vblk, kblk, vblk,
            pl.BlockSpec((1, 1, TQ, R * 3), lambda b, g, i: (b, g, i, 0)),
            pl.BlockSpec((n_sel, n_cmp_pad), lambda b, g, i: (0, 0)),
            pl.BlockSpec((LANES, 3 * R * DH), lambda b, g, i: (0, 0)),
        ],
        out_specs=qblk,
        out_shape=jax.ShapeDtypeStruct((bsz * seq, G * R * DH), BF16),
        compiler_params=_params("parallel", "parallel", "arbitrary"),
        name="nsa_attention",
    )(proj_q, kc, vc, ks, vs, kw, vw, gt, jnp.asarray(overlap.T, dtype=BF16), jnp.asarray(gexp, dtype=BF16))


ROW_TILES = D_MODEL // LANES


def _store_token_major(ref, val):
    n = val.shape[0]
    for c in range(ROW_TILES):
        ref[pl.ds(c, n, stride=ROW_TILES), :] = val[:, c * LANES:(c + 1) * LANES]


def _load_token_major(ref, n):
    return jnp.concatenate([ref[pl.ds(c, n, stride=ROW_TILES), :] for c in range(ROW_TILES)], axis=1)


def _merge_kernel(ys_ref, yn_ref, gs_ref, gn_ref, x_ref, wbs_ref, wbn_ref, wo_ref, fw_ref,
                  rwh_ref, rwl_ref, rb_ref, tri_ref, x1_ref, h_ref, ti_ref, tw_ref, slot_ref, cnt_ref, base_ref):
    @pl.when(pl.program_id(0) == 0)
    def _():
        base_ref[...] = jnp.zeros(base_ref.shape, F32)

    a = jnp.dot(ys_ref[...], wbs_ref[...], preferred_element_type=F32)
    b = jnp.dot(yn_ref[...], wbn_ref[...], preferred_element_type=F32)
    merged = _sigmoid(gs_ref[...]) * a + _sigmoid(gn_ref[...]) * b
    x1 = x_ref[...] + jnp.dot(merged.astype(BF16), wo_ref[...], preferred_element_type=F32)
    x1_ref[...] = x1
    var = jnp.mean(x1 * x1, axis=-1, keepdims=True)
    h = x1 * lax.rsqrt(var + NORM_EPS) * fw_ref[...]
    _store_token_major(h_ref, h)

    nt = (((1,), (1,)), ((), ()))
    h_hi = h.astype(BF16)
    h_lo = (h - h_hi.astype(F32)).astype(BF16)
    logits = (lax.dot_general(rwh_ref[...], h_hi, nt, preferred_element_type=F32)
              + (lax.dot_general(rwl_ref[...], h_hi, nt, preferred_element_type=F32)
                 + lax.dot_general(rwh_ref[...], h_lo, nt, preferred_element_type=F32))) + rb_ref[...]
    ne = logits.shape[0]
    jj = lax.broadcasted_iota(jnp.int32, (ne, 1), 0)
    rank = jnp.zeros(logits.shape, jnp.int32)
    for e in range(ne):
        ce = logits[e:e + 1, :]
        beats = (ce > logits) | ((ce == logits) & (jj > e))
        rank = rank + beats.astype(jnp.int32)
    sel = rank < TOP_K
    mx = jnp.max(logits, axis=0, keepdims=True)
    p = jnp.where(sel, jnp.exp(logits - mx), 0.0)
    p = p / jnp.sum(p, axis=0, keepdims=True)

    self01 = sel.astype(F32)
    before = jnp.dot(self01.astype(BF16), tri_ref[...], preferred_element_type=F32) + base_ref[...]
    for k in range(TOP_K):
        hit = rank == k
        ti_ref[k:k + 1, :] = jnp.sum(jnp.where(hit, jj, 0), axis=0, keepdims=True)
        tw_ref[k:k + 1, :] = jnp.sum(jnp.where(hit, p, 0.0), axis=0, keepdims=True)
        slot_ref[k:k + 1, :] = jnp.sum(jnp.where(hit, before, 0.0), axis=0, keepdims=True).astype(jnp.int32)
    base_ref[...] = base_ref[...] + jnp.sum(self01, axis=1, keepdims=True)
    cnt_ref[...] = base_ref[...]


def _merge_route(y_ssm, y_nsa, proj_f, x2, wbs, wbn, wo, ffn_w, router_w, router_b, tm):
    t, d = x2.shape
    row = lambda col: pl.BlockSpec((tm, d), lambda i: (i, col))
    const = lambda shape: pl.BlockSpec(shape, lambda i: (0,) * len(shape))
    kblk = pl.BlockSpec((TOP_K, tm), lambda i: (0, i))
    tri = np.triu(np.ones((tm, tm), np.float32), 1)
    rw_t = router_w.T
    rw_hi = rw_t.astype(BF16)
    rw_lo = (rw_t - rw_hi.astype(F32)).astype(BF16)
    return pl.pallas_call(
        _merge_kernel,
        grid=(t // tm,),
        in_specs=[
            row(0), row(0), row(F_MS // d), row(F_MN // d), row(0),
            const((d, d)), const((d, d)), const((d, d)), const((1, d)),
            const((N_EXPERTS, d)), const((N_EXPERTS, d)), const((N_EXPERTS, 1)), const((tm, tm)),
        ],
        out_specs=[row(0), pl.BlockSpec((tm * ROW_TILES, LANES), lambda i: (i, 0)),
                   kblk, kblk, kblk, const((N_EXPERTS, 1))],
        out_shape=[jax.ShapeDtypeStruct((t, d), F32), jax.ShapeDtypeStruct((t * ROW_TILES, LANES), F32),
                   jax.ShapeDtypeStruct((TOP_K, t), jnp.int32), jax.ShapeDtypeStruct((TOP_K, t), F32),
                   jax.ShapeDtypeStruct((TOP_K, t), jnp.int32), jax.ShapeDtypeStruct((N_EXPERTS, 1), F32)],
        scratch_shapes=[pltpu.VMEM((N_EXPERTS, 1), F32)],
        compiler_params=_params("arbitrary"),
        name="merge_route",
    )(y_ssm, y_nsa, proj_f, proj_f, x2, wbs.astype(BF16), wbn.astype(BF16), wo.astype(BF16),
      ffn_w.reshape(1, d), rw_hi, rw_lo, router_b.reshape(N_EXPERTS, 1), jnp.asarray(tri, dtype=BF16))


GU_CHUNK = 2 * LANES


def _deinterleave_kernel(w_ref, perm_ref, o_ref):
    o_ref[0] = jnp.dot(w_ref[0].astype(BF16), perm_ref[...], preferred_element_type=F32).astype(o_ref.dtype)


def _deinterleave(w):
    e, d, n = w.shape
    perm = np.zeros((GU_CHUNK, GU_CHUNK), np.float32)
    perm[2 * np.arange(LANES), np.arange(LANES)] = 1.0
    perm[2 * np.arange(LANES) + 1, LANES + np.arange(LANES)] = 1.0
    blk = pl.BlockSpec((1, d, GU_CHUNK), lambda i, j: (i, 0, j))
    return pl.pallas_call(
        _deinterleave_kernel,
        grid=(e, n // GU_CHUNK),
        in_specs=[blk, pl.BlockSpec((GU_CHUNK, GU_CHUNK), lambda i, j: (0, 0))],
        out_specs=blk,
        out_shape=jax.ShapeDtypeStruct((e, d, n), BF16),
        compiler_params=_params("parallel", "parallel"),
        name="deinterleave",
    )(w, jnp.asarray(perm, dtype=BF16))


MOE_TM = 256


def _moe_kernel(te_ref, nu_ref, tok0_ref, tokn_ref, dst_ref, h_hbm, wgu_ref, bgu_ref, wd_ref, bd_ref,
                y_hbm, xbuf, ybuf, gsem, ssem, *, spare_row):
    tm, rt = MOE_TM, ROW_TILES
    i = pl.program_id(0)
    nu = nu_ref[0]
    slot = i % 2
    other = 1 - slot

    def gather_copy(src_row, r, sl):
        return pltpu.make_async_copy(h_hbm.at[pl.ds(src_row, rt), :], xbuf.at[sl, pl.ds(r * rt, rt), :],
                                     gsem.at[sl])

    def scatter_copy(dst_row, r, sl):
        return pltpu.make_async_copy(ybuf.at[sl, pl.ds(r * rt, rt), :], y_hbm.at[pl.ds(dst_row, rt), :],
                                     ssem.at[sl])

    def start_gather(tok_ref, sl):
        for r in range(tm):
            gather_copy(pl.multiple_of(tok_ref[0, 0, r], rt), r, sl).start()

    def start_scatter(sl):
        for r in range(tm):
            scatter_copy(pl.multiple_of(dst_ref[0, 0, r], rt), r, sl).start()

    def wait_gather(sl):
        pltpu.make_async_copy(h_hbm.at[pl.ds(0, tm * rt), :], xbuf.at[sl], gsem.at[sl]).wait()

    def wait_scatter(sl):
        pltpu.make_async_copy(ybuf.at[sl], y_hbm.at[pl.ds(0, tm * rt), :], ssem.at[sl]).wait()

    @pl.when(i == 0)
    def _():
        ybuf[...] = jnp.zeros(ybuf.shape, ybuf.dtype)
        start_gather(tok0_ref, 0)
        for r in range(tm):
            scatter_copy((spare_row + r) * rt, r, 0).start()

    @pl.when(i < nu)
    def _():
        wait_gather(slot)
        start_gather(tokn_ref, other)
        wait_scatter(slot)
        start_scatter(other)
        x =_load_token_major(xbuf.at[slot], tm).astype(BF16)
        h1 = jnp.dot(x, wgu_ref[0], preferred_element_type=F32) + bgu_ref[0]
        nchunk = h1.shape[1] // GU_CHUNK
        glu = jnp.concatenate([h1[:, c * GU_CHUNK:c * GU_CHUNK + LANES] for c in range(nchunk)], axis=1)
        lin = jnp.concatenate([h1[:, c * GU_CHUNK + LANES:(c + 1) * GU_CHUNK] for c in range(nchunk)], axis=1)
        glu = jnp.minimum(glu, SWIGLU_LIMIT)
        lin = jnp.clip(lin, -SWIGLU_LIMIT, SWIGLU_LIMIT)
        act = glu * _sigmoid(SWIGLU_ALPHA * glu) * (lin + 1.0)
        y = jnp.dot(act.astype(BF16), wd_ref[0], preferred_element_type=F32) + bd_ref[0]
        _store_token_major(ybuf.at[slot], y)

    @pl.when(i == nu)
    def _():
        wait_gather(slot)
        wait_scatter(slot)
        start_scatter(other)
        wait_scatter(other)


def _moe_experts(h_rows, tok_tiles, dst_tiles, tile_expert, n_used, wgu, bgu, wd, bd, n_out_rows, spare_row):
    d = wd.shape[1]
    tm, rt = MOE_TM, ROW_TILES
    n_tiles = tok_tiles.shape[0]
    smem = lambda f: pl.BlockSpec((1, 1, tm), f, memory_space=pltpu.SMEM)
    grid_spec = pltpu.PrefetchScalarGridSpec(
        num_scalar_prefetch=2,
        grid=(n_tiles,),
        in_specs=[
            smem(lambda i, te, nu: (0, 0, 0)),
            smem(lambda i, te, nu: (jnp.minimum(i + 1, n_tiles - 1), 0, 0)),
            smem(lambda i, te, nu: (i, 0, 0)),
            pl.BlockSpec(memory_space=pl.ANY),
            pl.BlockSpec((1, d, 2 * d), lambda i, te, nu: (te[i], 0, 0)),
            pl.BlockSpec((1, 1, 2 * d), lambda i, te, nu: (te[i], 0, 0)),
            pl.BlockSpec((1, d, d), lambda i, te, nu: (te[i], 0, 0)),
            pl.BlockSpec((1, 1, d), lambda i, te, nu: (te[i], 0, 0)),
        ],
        out_specs=pl.BlockSpec(memory_space=pl.ANY),
        scratch_shapes=[
            pltpu.VMEM((2, tm * rt, LANES), F32),
            pltpu.VMEM((2, tm * rt, LANES), F32),
            pltpu.SemaphoreType.DMA((2,)),
            pltpu.SemaphoreType.DMA((2,)),
        ],
    )
    return pl.pallas_call(
        functools.partial(_moe_kernel, spare_row=spare_row),
        grid_spec=grid_spec,
        out_shape=jax.ShapeDtypeStruct((n_out_rows * rt, LANES), F32),
        compiler_params=_params("arbitrary"),
        name="moe_experts",
    )(tile_expert, n_used, tok_tiles, tok_tiles, dst_tiles, h_rows, wgu, bgu, wd, bd)


def _final_kernel(x1_ref, y0_ref, y1_ref, y2_ref, y3_ref, tw_ref, nw_ref, o_ref):
    tm = x1_ref.shape[0]
    tw = tw_ref[...]
    yk = [_load_token_major(r, tm) for r in (y0_ref, y1_ref, y2_ref, y3_ref)]
    moe = ((tw[:, 0:1] * yk[0] + tw[:, 1:2] * yk[1]) + (tw[:, 2:3] * yk[2] + tw[:, 3:4] * yk[3]))
    x = x1_ref[...] + moe
    var = jnp.mean(x * x, axis=-1, keepdims=True)
    o_ref[...] = x * lax.rsqrt(var + NORM_EPS) * nw_ref[...]


def _final_norm(x1, y_rows, top_w, norm_w, tm):
    t, d = x1.shape
    nt = t // tm
    yblk = lambda k: pl.BlockSpec((tm * ROW_TILES, LANES), lambda i: (k * nt + i, 0))
    return pl.pallas_call(
        _final_kernel,
        grid=(nt,),
        in_specs=[
            pl.BlockSpec((tm, d), lambda i: (i, 0)),
            yblk(0), yblk(1), yblk(2), yblk(3),
            pl.BlockSpec((tm, TOP_K), lambda i: (i, 0)),
            pl.BlockSpec((1, d), lambda i: (0, 0)),
        ],
        out_specs=pl.BlockSpec((tm, d), lambda i: (i, 0)),
        out_shape=jax.ShapeDtypeStruct((t, d), F32),
        compiler_params=_params("parallel"),
        name="final_norm",
    )(x1, y_rows, y_rows, y_rows, y_rows, top_w, norm_w.reshape(1, d))


def _dispatch_tables(top_i, slot, counts, n_rows):
    t = top_i.shape[1]
    n = t * TOP_K
    tm, rt = MOE_TM, ROW_TILES
    padded = ((counts + tm - 1) // tm) * tm
    pend = jnp.cumsum(padded)
    pstart = pend - padded
    onehot = top_i[:, :, None] == jnp.arange(N_EXPERTS, dtype=jnp.int32)
    dest = slot + jnp.sum(jnp.where(onehot, pstart, 0), axis=-1)
    row_pair = jnp.full((n_rows,), -1, jnp.int32).at[dest.reshape(-1)].set(jnp.arange(n, dtype=jnp.int32))
    live = row_pair >= 0
    tok_tiles = (jnp.where(live, row_pair % t, 0) * rt).reshape(n_rows // tm, 1, tm)
    rows = jnp.arange(n_rows, dtype=jnp.int32)
    out_row = jnp.where(live, row_pair, n + tm + rows % tm)
    spare = n + tm + jnp.arange(tm, dtype=jnp.int32)
    dst_tiles = (jnp.concatenate([spare, out_row]) * rt).reshape(n_rows // tm + 1, 1, tm)
    tile_start = jnp.arange(n_rows // tm, dtype=jnp.int32) * tm
    tile_expert = jnp.minimum(jnp.sum(tile_start[:, None] >= pend[None, :], axis=-1), N_EXPERTS - 1)
    n_used = (pend[-1] // tm).reshape(1)
    return tok_tiles, dst_tiles, tile_expert.astype(jnp.int32), n_used.astype(jnp.int32)


def _in_proj_weights(w_in):
    o = np.cumsum([0, D_INNER, XBC_DIM, SSM_HEADS, 1024, 256, 256, 256, 256, 256, 256, 3 * NSA_HEADS, 2 * D_MODEL])
    z, xbc, dt, q, kv, gate, mg = (w_in[:, o[0]:o[1]], w_in[:, o[1]:o[2]], w_in[:, o[2]:o[3]],
                                   w_in[:, o[3]:o[4]], w_in[:, o[4]:o[10]], w_in[:, o[10]:o[11]],
                                   w_in[:, o[11]:o[12]])
    small_pad = jnp.zeros((D_MODEL, F_WIDTH - F_SMALL - SSM_HEADS - 3 * NSA_HEADS), w_in.dtype)
    w_f = jnp.concatenate([z, mg[:, :D_MODEL], xbc, mg[:, D_MODEL:], dt, gate, small_pad], axis=1)
    w_q = jnp.concatenate([q, kv], axis=1)
    return w_f.astype(BF16), w_q.astype(BF16)


def kernel(x, mix_norm_w, w_in, conv_w, conv_b, dt_bias, a_log, d_skip, ssm_norm_w, cmp_pos_k, cmp_w1_k, cmp_w2_k, cmp_pos_v, cmp_w1_v, cmp_w2_v, w_branch_ssm, w_branch_nsa, w_out, ffn_norm_w, router_w, router_b, w_gate_up, b_gate_up, w_down, b_down, final_norm_w):
    bsz, seq, d = x.shape
    t = bsz * seq
    x2 = x.reshape(t, d)
    depth = w_in.shape[0]
    assert depth == 1, "single-layer block"
    for l in range(depth):
        w_f, w_q = _in_proj_weights(w_in[l])
        proj_f = _norm_matmul(x2, mix_norm_w[l], w_f, F32, 1024, 768)
        proj_q = _norm_matmul(x2, mix_norm_w[l], w_q, BF16, 1024, 512)

        small = proj_f[:, F_SMALL:F_SMALL + LANES]
        dt_t = jnp.transpose(small[:, :SSM_HEADS].reshape(bsz, seq, SSM_HEADS), (0, 2, 1))
        y_ssm = _ssd(proj_f, dt_t, conv_w[l], conv_b[l], dt_bias[l], a_log[l], d_skip[l], ssm_norm_w[l],
                     bsz, seq)

        kc = _compress(proj_q[:, 1024:1280], cmp_pos_k[l], cmp_w1_k[l], cmp_w2_k[l], bsz, seq)
        vc = _compress(proj_q[:, 1280:1536], cmp_pos_v[l], cmp_w1_v[l], cmp_w2_v[l], bsz, seq)
        gates = small[:, SSM_HEADS:SSM_HEADS + 3 * NSA_HEADS]
        y_nsa = _nsa(proj_q, gates, kc, vc, bsz, seq)

        x1, h, top_i, top_w, slot, counts = _merge_route(
            y_ssm, y_nsa, proj_f, x2, w_branch_ssm[l], w_branch_nsa[l], w_out[l], ffn_norm_w[l],
            router_w[l], router_b[l], 512)

        n_rows = t * TOP_K + N_EXPERTS * MOE_TM
        tok_tiles, dst_tiles, tile_expert, n_used = _dispatch_tables(
            top_i, slot, counts[:, 0].astype(jnp.int32), n_rows)
        wgu = _deinterleave(w_gate_up[l])
        nchunk = 2 * d // GU_CHUNK
        bgu = jnp.transpose(b_gate_up[l].reshape(N_EXPERTS, nchunk, LANES, 2), (0, 1, 3, 2))
        y_rows = _moe_experts(h, tok_tiles, dst_tiles, tile_expert, n_used, wgu,
                              bgu.reshape(N_EXPERTS, 1, 2 * d), w_down[l].astype(BF16), b_down[l][:, None, :],
                              n_out_rows=t * TOP_K + 2 * MOE_TM, spare_row=t * TOP_K)
    out = _final_norm(x1, y_rows, top_w.T, final_norm_w, 512)
    return out.reshape(bsz, seq, d)
```

```python
import functools

import numpy as np
import jax
import jax.numpy as jnp
from jax import lax
from jax.experimental import pallas as pl
from jax.experimental.pallas import tpu as pltpu

F32 = jnp.float32
BF16 = jnp.bfloat16

D_MODEL = 1024
D_INNER = 1024
SSM_HEAD_DIM = 64
SSM_HEADS = 16
SSM_GROUPS = 4
D_STATE = 128
CONV_K = 4
XBC_DIM = D_INNER + 2 * SSM_GROUPS * D_STATE
SSD_CHUNK = 128
NSA_HEAD_DIM = 64
NSA_HEADS = 16
NSA_GROUPS = 4
NSA_REP = 4
NSA_KV_WIDTH = 256
CMP_BLOCK = 32
CMP_STRIDE = 16
CMP_HIDDEN = 128
SEL_BLOCK = 64
SEL_TOPN = 8
WINDOW = 256
N_EXPERTS = 32
TOP_K = 4
SWIGLU_ALPHA = 1.702
SWIGLU_LIMIT = 7.0
NORM_EPS = 1e-5

LANES = 128
VMEM_LIMIT = 56 * 1024 * 1024

F_Z, F_MS, F_XBC, F_MN, P_Q = 0, 1024, 2048, 4096, 5120


def _sigmoid(x):
    return 0.5 * jnp.tanh(0.5 * x) + 0.5


def _silu(x):
    return x * _sigmoid(x)


def _softplus(x):
    return jnp.maximum(x, 0.0) + jnp.log1p(jnp.exp(-jnp.abs(x)))


def _split3(x):
    hi = x.astype(BF16)
    r1 = x - hi.astype(F32)
    mid = r1.astype(BF16)
    lo = (r1 - mid.astype(F32)).astype(BF16)
    return hi, mid, lo


def _params(*sem):
    return pltpu.CompilerParams(dimension_semantics=sem, vmem_limit_bytes=VMEM_LIMIT)


def _norm_matmul_kernel(x_ref, nw_ref, w_ref, o_ref, h_ref):
    @pl.when(pl.program_id(1) == 0)
    def _():
        x = x_ref[...]
        var = jnp.mean(x * x, axis=-1, keepdims=True)
        h_ref[...] = (x * lax.rsqrt(var + NORM_EPS) * nw_ref[...]).astype(BF16)

    o_ref[...] = jnp.dot(h_ref[...], w_ref[...], preferred_element_type=F32).astype(o_ref.dtype)


def _norm_matmul(x2, norm_w, w, out_dtype, tm, tn):
    t, d = x2.shape
    n = w.shape[1]
    return pl.pallas_call(
        _norm_matmul_kernel,
        grid=(t // tm, n // tn),
        in_specs=[
            pl.BlockSpec((tm, d), lambda i, j: (i, 0)),
            pl.BlockSpec((1, d), lambda i, j: (0, 0)),
            pl.BlockSpec((d, tn), lambda i, j: (0, j)),
        ],
        out_specs=pl.BlockSpec((tm, tn), lambda i, j: (i, j)),
        out_shape=jax.ShapeDtypeStruct((t, n), out_dtype),
        scratch_shapes=[pltpu.VMEM((tm, d), BF16)],
        compiler_params=_params("parallel", "arbitrary"),
        name="norm_matmul",
    )(x2, norm_w.reshape(1, d), w)


def _ssd_kernel(z_ref, xbc_ref, sm_ref, dtt_ref, convw_ref, convb_ref, dtb_ref, dtbt_ref,
                alog_ref, alogt_ref, dskip_ref, nw_ref, expand_ref, y_ref,
                xbuf, state, ydiag):
    L = SSD_CHUNK
    c = pl.program_id(1)

    @pl.when(c == 0)
    def _():
        xbuf[0:8, :] = jnp.zeros((8, XBC_DIM), F32)
        state[...] = jnp.zeros(state.shape, F32)

    xbuf[8:8 + L, :] = xbc_ref[...].astype(F32)
    acc = convb_ref[...] + convw_ref[0:1, :] * xbuf[5:5 + L, :]
    for k in range(1, CONV_K):
        acc = acc + convw_ref[k:k + 1, :] * xbuf[5 + k:5 + k + L, :]
    xbuf[0:8, :] = xbuf[L:L + 8, :]
    xbc = _silu(acc)
    xs = xbc[:, :D_INNER]

    lane = lax.broadcasted_iota(jnp.int32, (1, LANES), 1)
    a_row = jnp.where(lane < SSM_HEADS, -jnp.exp(alog_ref[...]), 0.0)
    dt = _softplus(sm_ref[...] + dtb_ref[...])
    a_dt = dt * a_row
    dt_t = _softplus(dtt_ref[0] + dtbt_ref[...])
    a_dt_t = dt_t * (-jnp.exp(alogt_ref[...]))

    row = lax.broadcasted_iota(jnp.int32, (L, L), 0)
    col = lax.broadcasted_iota(jnp.int32, (L, L), 1)
    lower = row >= col
    tri = lower.astype(BF16)
    tri_t = (row <= col).astype(BF16)
    cs = sum(jnp.dot(tri, part, preferred_element_type=F32) for part in _split3(a_dt))
    cs_t = sum(jnp.dot(part, tri_t, preferred_element_type=F32) for part in _split3(a_dt_t))
    cs_last = cs[L - 1:L, :]

    stacked = jnp.concatenate([dt, jnp.exp(cs_last - cs), jnp.exp(cs)], axis=0)
    wide = sum(jnp.dot(part, expand_ref[...], preferred_element_type=F32) for part in _split3(stacked))
    dt_x = wide[0:L]
    dte_x = wide[L:2 * L]
    ecs_x = wide[2 * L:3 * L]

    xdt = xs * dt_x
    xw = (xdt * dte_x).astype(BF16)
    xdt_b = xdt.astype(BF16)

    hpg = SSM_HEADS // SSM_GROUPS
    gw = hpg * SSM_HEAD_DIM
    y_off_parts = []
    for g in range(SSM_GROUPS):
        b_g = xbc[:, D_INNER + g * D_STATE:D_INNER + (g + 1) * D_STATE].astype(BF16)
        c_g = xbc[:, D_INNER + SSM_GROUPS * D_STATE + g * D_STATE:
                  D_INNER + SSM_GROUPS * D_STATE + (g + 1) * D_STATE].astype(BF16)
        cb = lax.dot_general(c_g, b_g, (((1,), (1,)), ((), ())), preferred_element_type=F32)
        for hh in range(hpg):
            h = g * hpg + hh
            seg = cs[:, h:h + 1] - cs_t[h:h + 1, :]
            decay = jnp.exp(jnp.where(lower, seg, -jnp.inf))
            m = (cb * decay).astype(BF16)
            ydiag[:, h * SSM_HEAD_DIM:(h + 1) * SSM_HEAD_DIM] = jnp.dot(
                m, xdt_b[:, h * SSM_HEAD_DIM:(h + 1) * SSM_HEAD_DIM], preferred_element_type=F32)
        st_prev = state[g]
        y_off_parts.append(jnp.dot(c_g, st_prev.astype(BF16), preferred_element_type=F32))
        st_new = lax.dot_general(b_g, xw[:, g * gw:(g + 1) * gw], (((0,), (0,)), ((), ())),
                                 preferred_element_type=F32)
        state[g] = st_prev * ecs_x[L - 1:L, g * gw:(g + 1) * gw] + st_new
    y_off = jnp.concatenate(y_off_parts, axis=1) * ecs_x

    y = (ydiag[...] + y_off + dskip_ref[...] * xs) * _silu(z_ref[...].astype(F32))
    for g in range(SSM_GROUPS):
        yg = y[:, g * gw:(g + 1) * gw]
        yg = yg * lax.rsqrt(jnp.mean(yg * yg, axis=-1, keepdims=True) + NORM_EPS)
        y_ref[:, g * gw:(g + 1) * gw] = (yg * nw_ref[:, g * gw:(g + 1) * gw]).astype(y_ref.dtype)


def _ssd(proj, small, dt_t, conv_w, conv_b, dt_bias, a_log, d_skip, norm_w, bsz, seq):
    L = SSD_CHUNK
    nc = seq // L
    pad = LANES - SSM_HEADS
    dtb = jnp.pad(dt_bias, (0, pad)).reshape(1, LANES)
    alog = jnp.pad(a_log, (0, pad)).reshape(1, LANES)
    dskip_x = jnp.repeat(d_skip, SSM_HEAD_DIM).reshape(1, D_INNER)
    expand = (np.arange(LANES)[:, None] == (np.arange(D_INNER)[None, :] // SSM_HEAD_DIM)).astype(np.float32)
    const = lambda shape: pl.BlockSpec(shape, lambda b, c: (0,) * len(shape))
    return pl.pallas_call(
        _ssd_kernel,
        grid=(bsz, nc),
        in_specs=[
            pl.BlockSpec((L, D_INNER), lambda b, c: (b * nc + c, F_Z // D_INNER)),
            pl.BlockSpec((L, XBC_DIM), lambda b, c: (b * nc + c, F_XBC // XBC_DIM)),
            pl.BlockSpec((L, LANES), lambda b, c: (b * nc + c, 0)),
            pl.BlockSpec((1, SSM_HEADS, L), lambda b, c: (b, 0, c)),
            const((CONV_K, XBC_DIM)), const((1, XBC_DIM)),
            const((1, LANES)), const((SSM_HEADS, 1)),
            const((1, LANES)), const((SSM_HEADS, 1)),
            const((1, D_INNER)), const((1, D_INNER)),
            const((LANES, D_INNER)),
        ],
        out_specs=pl.BlockSpec((L, D_INNER), lambda b, c: (b * nc + c, 0)),
        out_shape=jax.ShapeDtypeStruct((bsz * seq, D_INNER), BF16),
        scratch_shapes=[
            pltpu.VMEM((L + 8, XBC_DIM), F32),
            pltpu.VMEM((SSM_GROUPS, D_STATE, (SSM_HEADS // SSM_GROUPS) * SSM_HEAD_DIM), F32),
            pltpu.VMEM((L, D_INNER), F32),
        ],
        compiler_params=_params("parallel", "arbitrary"),
        name="ssd",
    )(proj, proj, small, dt_t, conv_w, conv_b.reshape(1, XBC_DIM),
      dtb, dt_bias.reshape(SSM_HEADS, 1), alog, a_log.reshape(SSM_HEADS, 1),
      dskip_x, norm_w.reshape(1, D_INNER), jnp.asarray(expand, dtype=BF16))


def _gelu_tanh(x):
    return 0.5 * x * (1.0 + jnp.tanh(np.sqrt(2.0 / np.pi) * (x + 0.044715 * (x * x * x))))


def _compress_kernel(cur_ref, nxt_ref, pos_ref, w1_ref, w2_ref, o_ref):
    half = CMP_STRIDE * NSA_HEAD_DIM
    lo = (cur_ref[0, 0].astype(F32) + pos_ref[0:1, :]).astype(BF16)
    hi = (nxt_ref[0, 0].astype(F32) + pos_ref[1:2, :]).astype(BF16)
    hid = (jnp.dot(lo, w1_ref[0:half, :], preferred_element_type=F32)
           + jnp.dot(hi, w1_ref[half:2 * half, :], preferred_element_type=F32))
    o_ref[0, 0] = jnp.dot(_gelu_tanh(hid).astype(BF16), w2_ref[...],
                          preferred_element_type=F32).astype(o_ref.dtype)


def _compress(kv, pos, w1, w2, bsz, seq):
    nch = seq // CMP_STRIDE
    half = CMP_STRIDE * NSA_HEAD_DIM
    ch = kv.reshape(bsz, nch, CMP_STRIDE, NSA_GROUPS, NSA_HEAD_DIM)
    ch = jnp.transpose(ch, (0, 3, 1, 2, 4)).reshape(bsz, NSA_GROUPS, nch, half)
    nxt = jnp.concatenate([ch[:, :, 1:], jnp.zeros_like(ch[:, :, :1])], axis=2)
    blk = pl.BlockSpec((1, 1, nch, half), lambda b, g: (b, g, 0, 0))
    return pl.pallas_call(
        _compress_kernel,
        grid=(bsz, NSA_GROUPS),
        in_specs=[
            blk, blk,
            pl.BlockSpec((2, half), lambda b, g: (0, 0)),
            pl.BlockSpec((2 * half, CMP_HIDDEN), lambda b, g: (0, 0)),
            pl.BlockSpec((CMP_HIDDEN, NSA_HEAD_DIM), lambda b, g: (0, 0)),
        ],
        out_specs=pl.BlockSpec((1, 1, nch, NSA_HEAD_DIM), lambda b, g: (b, g, 0, 0)),
        out_shape=jax.ShapeDtypeStruct((bsz, NSA_GROUPS, nch, NSA_HEAD_DIM), BF16),
        compiler_params=_params("parallel", "parallel"),
        name="compress",
    )(ch, nxt, pos.reshape(2, half), w1.astype(BF16), w2.astype(BF16))


NSA_TQ = 512
NSA_KC = 512
MASK_VALUE = -1e30
SOFTMAX_M0 = -1e20


def _nsa_kernel(q_ref, kc_ref, vc_ref, ks_ref, vs_ref, kw_ref, vw_ref, gate_ref,
                overlap_ref, gexp_ref, o_ref):
    R, TQ, KC, DH = NSA_REP, NSA_TQ, NSA_KC, NSA_HEAD_DIM
    assert 2 * DH == LANES
    i = pl.program_id(2)
    q0 = i * TQ
    nt = (((1,), (1,)), ((), ()))
    tcol = q0 + lax.broadcasted_iota(jnp.int32, (TQ, 1), 0)
    qblk = q_ref[...]
    scale = DH ** -0.5
    qf = [qblk[:, r * DH:(r + 1) * DH].astype(F32) * scale for r in range(R)]
    qh = [q.astype(BF16) for q in qf]
    v_width = vs_ref.shape[3]

    def flash_step(m_i, acc, q, k, v_ext, mask_bias=None):
        s = lax.dot_general(q, k, nt, preferred_element_type=F32)
        if mask_bias is not None:
            s = s + mask_bias
        m_new = jnp.maximum(m_i, jnp.max(s, axis=-1, keepdims=True))
        p = jnp.exp((s - m_new).astype(BF16))
        acc = acc * jnp.exp(m_i - m_new) + jnp.dot(p, v_ext, preferred_element_type=F32)
        return m_new, acc

    def init():
        return tuple((jnp.full((TQ, 1), SOFTMAX_M0, F32), jnp.zeros((TQ, v_width), F32)) for _ in range(R))

    def spread(x, mat):
        hi = x.astype(BF16)
        lo = (x - hi.astype(F32)).astype(BF16)
        return jnp.dot(hi, mat, preferred_element_type=F32) + jnp.dot(lo, mat, preferred_element_type=F32)

    kc = kc_ref[0, 0]
    vc = vc_ref[0, 0]
    ncmp = kc.shape[0]
    cmp_end = lax.broadcasted_iota(jnp.int32, (1, ncmp), 1) * CMP_STRIDE + (CMP_BLOCK - 1)
    cmask = cmp_end <= tcol
    o_cmp = []
    p4 = None
    for r in range(R):
        lg = jnp.where(cmask, lax.dot_general(qh[r], kc, nt, preferred_element_type=F32), -jnp.inf)
        mx = jnp.max(lg, axis=-1, keepdims=True)
        mx = jnp.where(mx > -jnp.inf, mx, 0.0)
        e = jnp.where(cmask, jnp.exp(lg - mx), 0.0)
        p = e / jnp.maximum(jnp.sum(e, axis=-1, keepdims=True), 1e-30)
        o_cmp.append(jnp.dot(p.astype(BF16), vc, preferred_element_type=F32))
        p4 = p if p4 is None else p4 + p

    imp = sum(lax.dot_general(overlap_ref[...], part, nt, preferred_element_type=F32)
              for part in _split3(p4))
    nsel = imp.shape[0]
    trow = q0 + lax.broadcasted_iota(jnp.int32, (1, TQ), 1)
    cur = trow // SEL_BLOCK
    jj = lax.broadcasted_iota(jnp.int32, (nsel, 1), 0)
    forced = (jj == 0) | (jj == cur) | (jj == cur - 1)
    val = jnp.where(forced, jnp.inf, jnp.where(jj <= cur, imp, -jnp.inf))
    rank = jnp.zeros((nsel, TQ), jnp.int32)
    for b in range(nsel):
        vb = val[b:b + 1, :]
        beats = (vb > val) | ((vb == val) & (jj > b))
        rank = rank + beats.astype(jnp.int32)
    bias = jnp.transpose(jnp.where(rank < SEL_TOPN, 0.0, MASK_VALUE))
    pad = jnp.zeros((TQ, ks_ref.shape[3] - DH - nsel), F32)
    qa = [jnp.concatenate([qf[r], bias, pad], axis=1).astype(BF16) for r in range(R)]

    def sel_step(kb, carry, mask_bias):
        k0 = pl.multiple_of(kb * KC, KC)
        k = ks_ref[0, 0, pl.ds(k0, KC), :]
        v = vs_ref[0, 0, pl.ds(k0, KC), :]
        return tuple(flash_step(carry[r][0], carry[r][1], qa[r], k, v, mask_bias) for r in range(R))

    n_full = q0 // KC
    carry = lax.fori_loop(0, n_full, lambda kb, c: sel_step(kb, c, None), init())
    kpos_d = n_full * KC + lax.broadcasted_iota(jnp.int32, (1, KC), 1)
    carry_s = sel_step(n_full, carry, jnp.where(kpos_d <= tcol, 0.0, MASK_VALUE))

    w0 = pl.multiple_of(jnp.maximum(q0 - WINDOW, 0), WINDOW)
    kpos_w = w0 + lax.broadcasted_iota(jnp.int32, (1, WINDOW + TQ), 1)
    bias_w = jnp.where((kpos_w <= tcol) & (kpos_w > tcol - WINDOW), 0.0, MASK_VALUE)
    k_w = kw_ref[0, 0, pl.ds(w0, WINDOW + TQ), :]
    v_w = vw_ref[0, 0, pl.ds(w0, WINDOW + TQ), :]
    carry_w = init()
    carry_w = tuple(flash_step(carry_w[r][0], carry_w[r][1], qh[r], k_w, v_w, bias_w) for r in range(R))

    g = _sigmoid(gate_ref[0, 0])
    g = jnp.concatenate([g, jnp.zeros((TQ, LANES - g.shape[1]), F32)], axis=1)
    gx = spread(g, gexp_ref[...])

    def normalised(branch):
        return jnp.concatenate([(acc / jnp.maximum(pltpu.roll(acc, DH, 1), 1e-30))[:, :DH] for _, acc in branch],
                               axis=1)

    w = R * DH
    out = (gx[:, 0:w] * jnp.concatenate(o_cmp, axis=1) + gx[:, w:2 * w] * normalised(carry_s)
           + gx[:, 2 * w:3 * w] * normalised(carry_w))
    o_ref[...] = out.astype(o_ref.dtype)


def _nsa(proj, gates, kc, vc, bsz, seq):
    G, R, DH, TQ = NSA_GROUPS, NSA_REP, NSA_HEAD_DIM, NSA_TQ
    n_cmp_pad = seq // CMP_STRIDE
    n_sel = seq // SEL_BLOCK
    nq = seq // TQ

    assert NSA_KC % TQ == 0 and seq % NSA_KC == 0 and TQ % WINDOW == 0

    def kv(idx, ext=None):
        lo = P_Q + G * R * DH + idx * NSA_KV_WIDTH
        a = jnp.transpose(proj[:, lo:lo + NSA_KV_WIDTH].reshape(bsz, seq, G, DH), (0, 2, 1, 3))
        if ext is not None:
            a = jnp.concatenate([a, jnp.broadcast_to(jnp.asarray(ext, BF16), (bsz, G) + ext.shape)], axis=-1)
        return a

    ones_col = np.ones((seq, LANES - DH), np.float32)
    block_onehot = np.zeros((seq, LANES - DH), np.float32)
    block_onehot[np.arange(seq), np.arange(seq) // SEL_BLOCK] = 1.0
    ks, vs, kw, vw = kv(2, block_onehot), kv(3, ones_col), kv(4), kv(5, ones_col)
    gt = jnp.transpose(gates.reshape(bsz, seq, G, R * 3), (0, 2, 1, 3))

    c_start = np.arange(n_cmp_pad) * CMP_STRIDE
    s_start = np.arange(n_sel) * SEL_BLOCK
    overlap = ((c_start[:, None] < s_start[None, :] + SEL_BLOCK)
               & (c_start[:, None] + CMP_BLOCK > s_start[None, :])).astype(np.float32)
    overlap[(seq - CMP_BLOCK) // CMP_STRIDE + 1:] = 0.0
    assert n_sel <= LANES - DH
    col = np.arange(3 * R * DH)
    gexp = (np.arange(LANES)[:, None] == (3 * ((col % (R * DH)) // DH) + col // (R * DH))[None, :]).astype(np.float32)

    qblk = pl.BlockSpec((TQ, R * DH), lambda b, g, i: (b * nq + i, P_Q // (R * DH) + g))
    oblk = pl.BlockSpec((TQ, R * DH), lambda b, g, i: (b * nq + i, g))
    cblk = pl.BlockSpec((1, 1, n_cmp_pad, DH), lambda b, g, i: (b, g, 0, 0))
    kblk = pl.BlockSpec((1, 1, seq, DH), lambda b, g, i: (b, g, 0, 0))
    vblk = pl.BlockSpec((1, 1, seq, LANES), lambda b, g, i: (b, g, 0, 0))
    return pl.pallas_call(
        _nsa_kernel,
        grid=(bsz, G, nq),
        in_specs=[
            qblk, cblk, cblk, vblk, vblk, kblk, vblk,
            pl.BlockSpec((1, 1, TQ, R * 3), lambda b, g, i: (b, g, i, 0)),
            pl.BlockSpec((n_sel, n_cmp_pad), lambda b, g, i: (0, 0)),
            pl.BlockSpec((LANES, 3 * R * DH), lambda b, g, i: (0, 0)),
        ],
        out_specs=oblk,
        out_shape=jax.ShapeDtypeStruct((bsz * seq, G * R * DH), BF16),
        compiler_params=_params("parallel", "parallel", "arbitrary"),
        name="nsa_attention",
    )(proj, kc, vc, ks, vs, kw, vw, gt, jnp.asarray(overlap.T, dtype=BF16), jnp.asarray(gexp, dtype=BF16))


ROW_TILES = D_MODEL // LANES


def _store_token_major(ref, val):
    n = val.shape[0]
    for c in range(ROW_TILES):
        ref[pl.ds(c, n, stride=ROW_TILES), :] = val[:, c * LANES:(c + 1) * LANES]


def _load_token_major(ref, n):
    return jnp.concatenate([ref[pl.ds(c, n, stride=ROW_TILES), :] for c in range(ROW_TILES)], axis=1)


def _merge_kernel(ys_ref, yn_ref, gs_ref, gn_ref, x_ref, wbs_ref, wbn_ref, wo_ref, fw_ref,
                  rwh_ref, rwl_ref, rb_ref, tri_ref, x1_ref, h_ref, ti_ref, tw_ref, slot_ref, cnt_ref, base_ref):
    @pl.when(pl.program_id(0) == 0)
    def _():
        base_ref[...] = jnp.zeros(base_ref.shape, F32)

    a = jnp.dot(ys_ref[...], wbs_ref[...], preferred_element_type=F32)
    b = jnp.dot(yn_ref[...], wbn_ref[...], preferred_element_type=F32)
    merged = _sigmoid(gs_ref[...].astype(F32)) * a + _sigmoid(gn_ref[...].astype(F32)) * b
    x1 = x_ref[...] + jnp.dot(merged.astype(BF16), wo_ref[...], preferred_element_type=F32)
    x1_ref[...] = x1
    var = jnp.mean(x1 * x1, axis=-1, keepdims=True)
    h = x1 * lax.rsqrt(var + NORM_EPS) * fw_ref[...]
    _store_token_major(h_ref, h)

    nt = (((1,), (1,)), ((), ()))
    h_hi = h.astype(BF16)
    h_lo = (h - h_hi.astype(F32)).astype(BF16)
    logits = (lax.dot_general(rwh_ref[...], h_hi, nt, preferred_element_type=F32)
              + (lax.dot_general(rwl_ref[...], h_hi, nt, preferred_element_type=F32)
                 + lax.dot_general(rwh_ref[...], h_lo, nt, preferred_element_type=F32))) + rb_ref[...]
    ne = logits.shape[0]
    jj = lax.broadcasted_iota(jnp.int32, (ne, 1), 0)
    rank = jnp.zeros(logits.shape, jnp.int32)
    for e in range(ne):
        ce = logits[e:e + 1, :]
        beats = (ce > logits) | ((ce == logits) & (jj > e))
        rank = rank + beats.astype(jnp.int32)
    sel = rank < TOP_K
    mx = jnp.max(logits, axis=0, keepdims=True)
    p = jnp.where(sel, jnp.exp(logits - mx), 0.0)
    p = p / jnp.sum(p, axis=0, keepdims=True)

    self01 = sel.astype(F32)
    before = jnp.dot(self01.astype(BF16), tri_ref[...], preferred_element_type=F32) + base_ref[...]
    for k in range(TOP_K):
        hit = rank == k
        ti_ref[k:k + 1, :] = jnp.sum(jnp.where(hit, jj, 0), axis=0, keepdims=True)
        tw_ref[k:k + 1, :] = jnp.sum(jnp.where(hit, p, 0.0), axis=0, keepdims=True)
        slot_ref[k:k + 1, :] = jnp.sum(jnp.where(hit, before, 0.0), axis=0, keepdims=True).astype(jnp.int32)
    base_ref[...] = base_ref[...] + jnp.sum(self01, axis=1, keepdims=True)
    cnt_ref[...] = base_ref[...]


def _merge_route(y_ssm, y_nsa, proj, x2, wbs, wbn, wo, ffn_w, router_w, router_b, tm):
    t, d = x2.shape
    row = lambda col: pl.BlockSpec((tm, d), lambda i: (i, col))
    const = lambda shape: pl.BlockSpec(shape, lambda i: (0,) * len(shape))
    kblk = pl.BlockSpec((TOP_K, tm), lambda i: (0, i))
    tri = np.triu(np.ones((tm, tm), np.float32), 1)
    rw_t = router_w.T
    rw_hi = rw_t.astype(BF16)
    rw_lo = (rw_t - rw_hi.astype(F32)).astype(BF16)
    return pl.pallas_call(
        _merge_kernel,
        grid=(t // tm,),
        in_specs=[
            row(0), row(0), row(F_MS // d), row(F_MN // d), row(0),
            const((d, d)), const((d, d)), const((d, d)), const((1, d)),
            const((N_EXPERTS, d)), const((N_EXPERTS, d)), const((N_EXPERTS, 1)), const((tm, tm)),
        ],
        out_specs=[row(0), pl.BlockSpec((tm * ROW_TILES, LANES), lambda i: (i, 0)),
                   kblk, kblk, kblk, const((N_EXPERTS, 1))],
        out_shape=[jax.ShapeDtypeStruct((t, d), F32), jax.ShapeDtypeStruct((t * ROW_TILES, LANES), F32),
                   jax.ShapeDtypeStruct((TOP_K, t), jnp.int32), jax.ShapeDtypeStruct((TOP_K, t), F32),
                   jax.ShapeDtypeStruct((TOP_K, t), jnp.int32), jax.ShapeDtypeStruct((N_EXPERTS, 1), F32)],
        scratch_shapes=[pltpu.VMEM((N_EXPERTS, 1), F32)],
        compiler_params=_params("arbitrary"),
        name="merge_route",
    )(y_ssm, y_nsa, proj, proj, x2, wbs.astype(BF16), wbn.astype(BF16), wo.astype(BF16),
      ffn_w.reshape(1, d), rw_hi, rw_lo, router_b.reshape(N_EXPERTS, 1), jnp.asarray(tri, dtype=BF16))


MOE_TM = 256
GU_CHUNK = 2 * LANES


def _moe_kernel(te_ref, nu_ref, tok0_ref, tokn_ref, dst_ref, h_hbm, wgu_ref, bgu_ref, wd_ref, bd_ref, perm_ref,
                y_hbm, xbuf, ybuf, wgu_s, wd_s, gsem, ssem, *, spare_row):
    tm, rt = MOE_TM, ROW_TILES
    i = pl.program_id(0)
    nu = nu_ref[0]
    slot = i % 2
    other = 1 - slot
    new_expert = (i == 0) | (te_ref[i] != te_ref[jnp.maximum(i - 1, 0)])

    @pl.when(new_expert & (i < nu))
    def _():
        wd_s[...] = wd_ref[0].astype(BF16)
        for c in range(wgu_s.shape[1] // GU_CHUNK):
            cols = slice(c * GU_CHUNK, (c + 1) * GU_CHUNK)
            wgu_s[:, cols] = jnp.dot(wgu_ref[0, :, cols].astype(BF16), perm_ref[...],
                                     preferred_element_type=F32).astype(BF16)

    def gather_copy(src_row, r, sl):
        return pltpu.make_async_copy(h_hbm.at[pl.ds(src_row, rt), :], xbuf.at[sl, pl.ds(r * rt, rt), :],
                                     gsem.at[sl])

    def scatter_copy(dst_row, r, sl):
        return pltpu.make_async_copy(ybuf.at[sl, pl.ds(r * rt, rt), :], y_hbm.at[pl.ds(dst_row, rt), :],
                                     ssem.at[sl])

    def start_gather(tok_ref, sl):
        for r in range(tm):
            gather_copy(pl.multiple_of(tok_ref[0, 0, r], rt), r, sl).start()

    def start_scatter(sl):
        for r in range(tm):
            scatter_copy(pl.multiple_of(dst_ref[0, 0, r], rt), r, sl).start()

    def wait_gather(sl):
        pltpu.make_async_copy(h_hbm.at[pl.ds(0, tm * rt), :], xbuf.at[sl], gsem.at[sl]).wait()

    def wait_scatter(sl):
        pltpu.make_async_copy(ybuf.at[sl], y_hbm.at[pl.ds(0, tm * rt), :], ssem.at[sl]).wait()

    @pl.when(i == 0)
    def _():
        ybuf[...] = jnp.zeros(ybuf.shape, ybuf.dtype)
        start_gather(tok0_ref, 0)
        for r in range(tm):
            scatter_copy((spare_row + r) * rt, r, 0).start()

    @pl.when(i < nu)
    def _():
        wait_gather(slot)
        start_gather(tokn_ref, other)
        wait_scatter(slot)
        start_scatter(other)
        x = _load_token_major(xbuf.at[slot], tm).astype(BF16)
        h1 = jnp.dot(x, wgu_s[...], preferred_element_type=F32) + bgu_ref[0]
        nchunk = h1.shape[1] // GU_CHUNK
        glu = jnp.concatenate([h1[:, c * GU_CHUNK:c * GU_CHUNK + LANES] for c in range(nchunk)], axis=1)
        lin = jnp.concatenate([h1[:, c * GU_CHUNK + LANES:(c + 1) * GU_CHUNK] for c in range(nchunk)], axis=1)
        glu = jnp.minimum(glu, SWIGLU_LIMIT)
        lin = jnp.clip(lin, -SWIGLU_LIMIT, SWIGLU_LIMIT)
        act = glu * _sigmoid(SWIGLU_ALPHA * glu) * (lin + 1.0)
        y = jnp.dot(act.astype(BF16), wd_s[...], preferred_element_type=F32) + bd_ref[0]
        _store_token_major(ybuf.at[slot], y)

    @pl.when(i == nu)
    def _():
        wait_gather(slot)
        wait_scatter(slot)
        start_scatter(other)
        wait_scatter(other)


def _moe_experts(h_rows, tok_tiles, dst_tiles, tile_expert, n_used, wgu, bgu, wd, bd, n_out_rows, spare_row):
    d = wd.shape[1]
    tm, rt = MOE_TM, ROW_TILES
    n_tiles = tok_tiles.shape[0]
    perm = np.zeros((GU_CHUNK, GU_CHUNK), np.float32)
    perm[2 * np.arange(LANES), np.arange(LANES)] = 1.0
    perm[2 * np.arange(LANES) + 1, LANES + np.arange(LANES)] = 1.0
    smem = lambda f: pl.BlockSpec((1, 1, tm), f, memory_space=pltpu.SMEM)
    grid_spec = pltpu.PrefetchScalarGridSpec(
        num_scalar_prefetch=2,
        grid=(n_tiles,),
        in_specs=[
            smem(lambda i, te, nu: (0, 0, 0)),
            smem(lambda i, te, nu: (jnp.minimum(i + 1, n_tiles - 1), 0, 0)),
            smem(lambda i, te, nu: (i, 0, 0)),
            pl.BlockSpec(memory_space=pl.ANY),
            pl.BlockSpec((1, d, 2 * d), lambda i, te, nu: (te[i], 0, 0)),
            pl.BlockSpec((1, 1, 2 * d), lambda i, te, nu: (te[i], 0, 0)),
            pl.BlockSpec((1, d, d), lambda i, te, nu: (te[i], 0, 0)),
            pl.BlockSpec((1, 1, d), lambda i, te, nu: (te[i], 0, 0)),
            pl.BlockSpec((GU_CHUNK, GU_CHUNK), lambda i, te, nu: (0, 0)),
        ],
        out_specs=pl.BlockSpec(memory_space=pl.ANY),
        scratch_shapes=[
            pltpu.VMEM((2, tm * rt, LANES), F32),
            pltpu.VMEM((2, tm * rt, LANES), F32),
            pltpu.VMEM((d, 2 * d), BF16),
            pltpu.VMEM((d, d), BF16),
            pltpu.SemaphoreType.DMA((2,)),
            pltpu.SemaphoreType.DMA((2,)),
        ],
    )
    return pl.pallas_call(
        functools.partial(_moe_kernel, spare_row=spare_row),
        grid_spec=grid_spec,
        out_shape=jax.ShapeDtypeStruct((n_out_rows * rt, LANES), F32),
        compiler_params=_params("arbitrary"),
        name="moe_experts",
    )(tile_expert, n_used, tok_tiles, tok_tiles, dst_tiles, h_rows, wgu, bgu, wd, bd, jnp.asarray(perm, dtype=BF16))


def _final_kernel(x1_ref, y0_ref, y1_ref, y2_ref, y3_ref, tw_ref, nw_ref, o_ref):
    tm = x1_ref.shape[0]
    tw = tw_ref[...]
    yk = [_load_token_major(r, tm) for r in (y0_ref, y1_ref, y2_ref, y3_ref)]
    moe = ((tw[:, 0:1] * yk[0] + tw[:, 1:2] * yk[1]) + (tw[:, 2:3] * yk[2] + tw[:, 3:4] * yk[3]))
    x = x1_ref[...] + moe
    var = jnp.mean(x * x, axis=-1, keepdims=True)
    o_ref[...] = x * lax.rsqrt(var + NORM_EPS) * nw_ref[...]


def _final_norm(x1, y_rows, top_w, norm_w, tm):
    t, d = x1.shape
    nt = t // tm
    yblk = lambda k: pl.BlockSpec((tm * ROW_TILES, LANES), lambda i: (k * nt + i, 0))
    return pl.pallas_call(
        _final_kernel,
        grid=(nt,),
        in_specs=[
            pl.BlockSpec((tm, d), lambda i: (i, 0)),
            yblk(0), yblk(1), yblk(2), yblk(3),
            pl.BlockSpec((tm, TOP_K), lambda i: (i, 0)),
            pl.BlockSpec((1, d), lambda i: (0, 0)),
        ],
        out_specs=pl.BlockSpec((tm, d), lambda i: (i, 0)),
        out_shape=jax.ShapeDtypeStruct((t, d), F32),
        compiler_params=_params("parallel"),
        name="final_norm",
    )(x1, y_rows, y_rows, y_rows, y_rows, top_w, norm_w.reshape(1, d))


def _dispatch_tables(top_i, slot, counts, n_rows):
    t = top_i.shape[1]
    n = t * TOP_K
    tm, rt = MOE_TM, ROW_TILES
    padded = ((counts + tm - 1) // tm) * tm
    pend = jnp.cumsum(padded)
    pstart = pend - padded
    onehot = top_i[:, :, None] == jnp.arange(N_EXPERTS, dtype=jnp.int32)
    dest = slot + jnp.sum(jnp.where(onehot, pstart, 0), axis=-1)
    row_pair = jnp.full((n_rows,), -1, jnp.int32).at[dest.reshape(-1)].set(jnp.arange(n, dtype=jnp.int32))
    live = row_pair >= 0
    tok_tiles = (jnp.where(live, row_pair % t, 0) * rt).reshape(n_rows // tm, 1, tm)
    rows = jnp.arange(n_rows, dtype=jnp.int32)
    out_row = jnp.where(live, row_pair, n + tm + rows % tm)
    spare = n + tm + jnp.arange(tm, dtype=jnp.int32)
    dst_tiles = (jnp.concatenate([spare, out_row]) * rt).reshape(n_rows // tm + 1, 1, tm)
    tile_start = jnp.arange(n_rows // tm, dtype=jnp.int32) * tm
    tile_expert = jnp.minimum(jnp.sum(tile_start[:, None] >= pend[None, :], axis=-1), N_EXPERTS - 1)
    n_used = (pend[-1] // tm).reshape(1)
    return tok_tiles, dst_tiles, tile_expert.astype(jnp.int32), n_used.astype(jnp.int32)


def _in_proj_weights(w_in):
    o = np.cumsum([0, D_INNER, XBC_DIM, SSM_HEADS, 1024, 256, 256, 256, 256, 256, 256, 3 * NSA_HEADS, 2 * D_MODEL])
    z, xbc, dt, q, kv, gate, mg = (w_in[:, o[0]:o[1]], w_in[:, o[1]:o[2]], w_in[:, o[2]:o[3]],
                                   w_in[:, o[3]:o[4]], w_in[:, o[4]:o[10]], w_in[:, o[10]:o[11]],
                                   w_in[:, o[11]:o[12]])
    small_pad = jnp.zeros((D_MODEL, LANES - SSM_HEADS - 3 * NSA_HEADS), w_in.dtype)
    w_p = jnp.concatenate([z, mg[:, :D_MODEL], xbc, mg[:, D_MODEL:], q, kv], axis=1)
    w_small = jnp.concatenate([dt, gate, small_pad], axis=1)
    return w_p.astype(BF16), w_small.astype(BF16)


def kernel(x, mix_norm_w, w_in, conv_w, conv_b, dt_bias, a_log, d_skip, ssm_norm_w, cmp_pos_k, cmp_w1_k, cmp_w2_k, cmp_pos_v, cmp_w1_v, cmp_w2_v, w_branch_ssm, w_branch_nsa, w_out, ffn_norm_w, router_w, router_b, w_gate_up, b_gate_up, w_down, b_down, final_norm_w):
    bsz, seq, d = x.shape
    t = bsz * seq
    x2 = x.reshape(t, d)
    depth = w_in.shape[0]
    assert depth == 1, "single-layer block"
    for l in range(depth):
        w_p, w_small = _in_proj_weights(w_in[l])
        proj = _norm_matmul(x2, mix_norm_w[l], w_p, BF16, 2048, 512)
        small = _norm_matmul(x2, mix_norm_w[l], w_small, F32, 2048, LANES)

        dt_t = jnp.transpose(small[:, :SSM_HEADS].reshape(bsz, seq, SSM_HEADS), (0, 2, 1))
        y_ssm = _ssd(proj, small, dt_t, conv_w[l], conv_b[l], dt_bias[l], a_log[l], d_skip[l], ssm_norm_w[l],
                     bsz, seq)

        kv0 = P_Q + NSA_HEADS * NSA_HEAD_DIM
        kc = _compress(proj[:, kv0:kv0 + NSA_KV_WIDTH], cmp_pos_k[l], cmp_w1_k[l], cmp_w2_k[l], bsz, seq)
        vc = _compress(proj[:, kv0 + NSA_KV_WIDTH:kv0 + 2 * NSA_KV_WIDTH], cmp_pos_v[l], cmp_w1_v[l], cmp_w2_v[l],
                       bsz, seq)
        gates = small[:, SSM_HEADS:SSM_HEADS + 3 * NSA_HEADS]
        y_nsa = _nsa(proj, gates, kc, vc, bsz, seq)

        x1, h, top_i, top_w, slot, counts = _merge_route(
            y_ssm, y_nsa, proj, x2, w_branch_ssm[l], w_branch_nsa[l], w_out[l], ffn_norm_w[l],
            router_w[l], router_b[l], 512)

        n_rows = t * TOP_K + N_EXPERTS * MOE_TM
        tok_tiles, dst_tiles, tile_expert, n_used = _dispatch_tables(
            top_i, slot, counts[:, 0].astype(jnp.int32), n_rows)
        nchunk = 2 * d // GU_CHUNK
        bgu = jnp.transpose(b_gate_up[l].reshape(N_EXPERTS, nchunk, LANES, 2), (0, 1, 3, 2))
        y_rows = _moe_experts(h, tok_tiles, dst_tiles, tile_expert, n_used, w_gate_up[l],
                              bgu.reshape(N_EXPERTS, 1, 2 * d), w_down[l], b_down[l][:, None, :],
                              n_out_rows=t * TOP_K + 2 * MOE_TM, spare_row=t * TOP_K)
    out = _final_norm(x1, y_rows, top_w.T, final_norm_w, 512)
    return out.reshape(bsz, seq, d)
```

```python
import functools

import numpy as np
import jax
import jax.numpy as jnp
from jax import lax
from jax.experimental import pallas as pl
from jax.experimental.pallas import tpu as pltpu

F32 = jnp.float32
BF16 = jnp.bfloat16

D_MODEL = 1024
D_INNER = 1024
SSM_HEAD_DIM = 64
SSM_HEADS = 16
SSM_GROUPS = 4
D_STATE = 128
CONV_K = 4
XBC_DIM = D_INNER + 2 * SSM_GROUPS * D_STATE
SSD_CHUNK = 128
NSA_HEAD_DIM = 64
NSA_HEADS = 16
NSA_GROUPS = 4
NSA_REP = 4
NSA_KV_WIDTH = 256
CMP_BLOCK = 32
CMP_STRIDE = 16
CMP_HIDDEN = 128
SEL_BLOCK = 64
SEL_TOPN = 8
WINDOW = 256
N_EXPERTS = 32
TOP_K = 4
SWIGLU_ALPHA = 1.702
SWIGLU_LIMIT = 7.0
NORM_EPS = 1e-5

LANES = 128
VMEM_LIMIT = 56 * 1024 * 1024

F_Z, F_MS, F_XBC, F_MN, P_Q = 0, 1024, 2048, 4096, 5120


def _sigmoid(x):
    return 0.5 * jnp.tanh(0.5 * x) + 0.5


def _silu(x):
    return x * _sigmoid(x)


def _softplus(x):
    return jnp.maximum(x, 0.0) + jnp.log1p(jnp.exp(-jnp.abs(x)))


def _split3(x):
    hi = x.astype(BF16)
    r1 = x - hi.astype(F32)
    mid = r1.astype(BF16)
    lo = (r1 - mid.astype(F32)).astype(BF16)
    return hi, mid, lo


def _params(*sem):
    return pltpu.CompilerParams(dimension_semantics=sem, vmem_limit_bytes=VMEM_LIMIT)


def _norm_matmul_kernel(x_ref, nw_ref, w_ref, o_ref, h_ref):
    @pl.when(pl.program_id(1) == 0)
    def _():
        x = x_ref[...]
        var = jnp.mean(x * x, axis=-1, keepdims=True)
        h_ref[...] = (x * lax.rsqrt(var + NORM_EPS) * nw_ref[...]).astype(BF16)

    o_ref[...] = jnp.dot(h_ref[...], w_ref[...], preferred_element_type=F32).astype(o_ref.dtype)


def _norm_matmul(x2, norm_w, w, out_dtype, tm, tn):
    t, d = x2.shape
    n = w.shape[1]
    return pl.pallas_call(
        _norm_matmul_kernel,
        grid=(t // tm, n // tn),
        in_specs=[
            pl.BlockSpec((tm, d), lambda i, j: (i, 0)),
            pl.BlockSpec((1, d), lambda i, j: (0, 0)),
            pl.BlockSpec((d, tn), lambda i, j: (0, j)),
        ],
        out_specs=pl.BlockSpec((tm, tn), lambda i, j: (i, j)),
        out_shape=jax.ShapeDtypeStruct((t, n), out_dtype),
        scratch_shapes=[pltpu.VMEM((tm, d), BF16)],
        compiler_params=_params("parallel", "arbitrary"),
        name="norm_matmul",
    )(x2, norm_w.reshape(1, d), w)


def _ssd_kernel(z_ref, xbc_ref, sm_ref, dtt_ref, convw_ref, convb_ref, dtb_ref, dtbt_ref,
                alog_ref, alogt_ref, dskip_ref, nw_ref, expand_ref, y_ref,
                xbuf, state, ydiag):
    L = SSD_CHUNK
    c = pl.program_id(1)

    @pl.when(c == 0)
    def _():
        xbuf[0:8, :] = jnp.zeros((8, XBC_DIM), F32)
        state[...] = jnp.zeros(state.shape, F32)

    xbuf[8:8 + L, :] = xbc_ref[...].astype(F32)
    acc = convb_ref[...] + convw_ref[0:1, :] * xbuf[5:5 + L, :]
    for k in range(1, CONV_K):
        acc = acc + convw_ref[k:k + 1, :] * xbuf[5 + k:5 + k + L, :]
    xbuf[0:8, :] = xbuf[L:L + 8, :]
    xbc = _silu(acc)
    xs = xbc[:, :D_INNER]

    lane = lax.broadcasted_iota(jnp.int32, (1, LANES), 1)
    a_row = jnp.where(lane < SSM_HEADS, -jnp.exp(alog_ref[...]), 0.0)
    dt = _softplus(sm_ref[...] + dtb_ref[...])
    a_dt = dt * a_row
    dt_t = _softplus(dtt_ref[0] + dtbt_ref[...])
    a_dt_t = dt_t * (-jnp.exp(alogt_ref[...]))

    row = lax.broadcasted_iota(jnp.int32, (L, L), 0)
    col = lax.broadcasted_iota(jnp.int32, (L, L), 1)
    lower = row >= col
    tri = lower.astype(BF16)
    tri_t = (row <= col).astype(BF16)
    cs = sum(jnp.dot(tri, part, preferred_element_type=F32) for part in _split3(a_dt))
    cs_t = sum(jnp.dot(part, tri_t, preferred_element_type=F32) for part in _split3(a_dt_t))
    cs_last = cs[L - 1:L, :]

    stacked = jnp.concatenate([dt, jnp.exp(cs_last - cs), jnp.exp(cs)], axis=0)
    wide = sum(jnp.dot(part, expand_ref[...], preferred_element_type=F32) for part in _split3(stacked))
    dt_x = wide[0:L]
    dte_x = wide[L:2 * L]
    ecs_x = wide[2 * L:3 * L]

    xdt = xs * dt_x
    xw = (xdt * dte_x).astype(BF16)
    xdt_b = xdt.astype(BF16)

    hpg = SSM_HEADS // SSM_GROUPS
    gw = hpg * SSM_HEAD_DIM
    y_off_parts = []
    for g in range(SSM_GROUPS):
        b_g = xbc[:, D_INNER + g * D_STATE:D_INNER + (g + 1) * D_STATE].astype(BF16)
        c_g = xbc[:, D_INNER + SSM_GROUPS * D_STATE + g * D_STATE:
                  D_INNER + SSM_GROUPS * D_STATE + (g + 1) * D_STATE].astype(BF16)
        cb = lax.dot_general(c_g, b_g, (((1,), (1,)), ((), ())), preferred_element_type=F32)
        for hh in range(hpg):
            h = g * hpg + hh
            seg = cs[:, h:h + 1] - cs_t[h:h + 1, :]
            decay = jnp.exp(jnp.where(lower, seg, -jnp.inf))
            m = (cb * decay).astype(BF16)
            ydiag[:, h * SSM_HEAD_DIM:(h + 1) * SSM_HEAD_DIM] = jnp.dot(
                m, xdt_b[:, h * SSM_HEAD_DIM:(h + 1) * SSM_HEAD_DIM], preferred_element_type=F32)
        st_prev = state[g]
        y_off_parts.append(jnp.dot(c_g, st_prev.astype(BF16), preferred_element_type=F32))
        st_new = lax.dot_general(b_g, xw[:, g * gw:(g + 1) * gw], (((0,), (0,)), ((), ())),
                                 preferred_element_type=F32)
        state[g] = st_prev * ecs_x[L - 1:L, g * gw:(g + 1) * gw] + st_new
    y_off = jnp.concatenate(y_off_parts, axis=1) * ecs_x

    y = (ydiag[...] + y_off + dskip_ref[...] * xs) * _silu(z_ref[...].astype(F32))
    for g in range(SSM_GROUPS):
        yg = y[:, g * gw:(g + 1) * gw]
        yg = yg * lax.rsqrt(jnp.mean(yg * yg, axis=-1, keepdims=True) + NORM_EPS)
        y_ref[:, g * gw:(g + 1) * gw] = (yg * nw_ref[:, g * gw:(g + 1) * gw]).astype(y_ref.dtype)


def _ssd(proj, small, dt_t, conv_w, conv_b, dt_bias, a_log, d_skip, norm_w, bsz, seq):
    L = SSD_CHUNK
    nc = seq // L
    pad = LANES - SSM_HEADS
    dtb = jnp.pad(dt_bias, (0, pad)).reshape(1, LANES)
    alog = jnp.pad(a_log, (0, pad)).reshape(1, LANES)
    dskip_x = jnp.repeat(d_skip, SSM_HEAD_DIM).reshape(1, D_INNER)
    expand = (np.arange(LANES)[:, None] == (np.arange(D_INNER)[None, :] // SSM_HEAD_DIM)).astype(np.float32)
    const = lambda shape: pl.BlockSpec(shape, lambda b, c: (0,) * len(shape))
    return pl.pallas_call(
        _ssd_kernel,
        grid=(bsz, nc),
        in_specs=[
            pl.BlockSpec((L, D_INNER), lambda b, c: (b * nc + c, F_Z // D_INNER)),
            pl.BlockSpec((L, XBC_DIM), lambda b, c: (b * nc + c, F_XBC // XBC_DIM)),
            pl.BlockSpec((L, LANES), lambda b, c: (b * nc + c, 0)),
            pl.BlockSpec((1, SSM_HEADS, L), lambda b, c: (b, 0, c)),
            const((CONV_K, XBC_DIM)), const((1, XBC_DIM)),
            const((1, LANES)), const((SSM_HEADS, 1)),
            const((1, LANES)), const((SSM_HEADS, 1)),
            const((1, D_INNER)), const((1, D_INNER)),
            const((LANES, D_INNER)),
        ],
        out_specs=pl.BlockSpec((L, D_INNER), lambda b, c: (b * nc + c, 0)),
        out_shape=jax.ShapeDtypeStruct((bsz * seq, D_INNER), BF16),
        scratch_shapes=[
            pltpu.VMEM((L + 8, XBC_DIM), F32),
            pltpu.VMEM((SSM_GROUPS, D_STATE, (SSM_HEADS // SSM_GROUPS) * SSM_HEAD_DIM), F32),
            pltpu.VMEM((L, D_INNER), F32),
        ],
        compiler_params=_params("parallel", "arbitrary"),
        name="ssd",
    )(proj, proj, small, dt_t, conv_w, conv_b.reshape(1, XBC_DIM),
      dtb, dt_bias.reshape(SSM_HEADS, 1), alog, a_log.reshape(SSM_HEADS, 1),
      dskip_x, norm_w.reshape(1, D_INNER), jnp.asarray(expand, dtype=BF16))


def _gelu_tanh(x):
    return 0.5 * x * (1.0 + jnp.tanh(np.sqrt(2.0 / np.pi) * (x + 0.044715 * (x * x * x))))


def _compress_kernel(cur_ref, nxt_ref, pos_ref, w1_ref, w2_ref, o_ref):
    half = CMP_STRIDE * NSA_HEAD_DIM
    lo = (cur_ref[0, 0].astype(F32) + pos_ref[0:1, :]).astype(BF16)
    hi = (nxt_ref[0, 0].astype(F32) + pos_ref[1:2, :]).astype(BF16)
    hid = (jnp.dot(lo, w1_ref[0:half, :], preferred_element_type=F32)
           + jnp.dot(hi, w1_ref[half:2 * half, :], preferred_element_type=F32))
    o_ref[0, 0] = jnp.dot(_gelu_tanh(hid).astype(BF16), w2_ref[...],
                          preferred_element_type=F32).astype(o_ref.dtype)


def _compress(kv, pos, w1, w2, bsz, seq):
    nch = seq // CMP_STRIDE
    half = CMP_STRIDE * NSA_HEAD_DIM
    ch = kv.reshape(bsz, nch, CMP_STRIDE, NSA_GROUPS, NSA_HEAD_DIM)
    ch = jnp.transpose(ch, (0, 3, 1, 2, 4)).reshape(bsz, NSA_GROUPS, nch, half)
    nxt = jnp.concatenate([ch[:, :, 1:], jnp.zeros_like(ch[:, :, :1])], axis=2)
    blk = pl.BlockSpec((1, 1, nch, half), lambda b, g: (b, g, 0, 0))
    return pl.pallas_call(
        _compress_kernel,
        grid=(bsz, NSA_GROUPS),
        in_specs=[
            blk, blk,
            pl.BlockSpec((2, half), lambda b, g: (0, 0)),
            pl.BlockSpec((2 * half, CMP_HIDDEN), lambda b, g: (0, 0)),
            pl.BlockSpec((CMP_HIDDEN, NSA_HEAD_DIM), lambda b, g: (0, 0)),
        ],
        out_specs=pl.BlockSpec((1, 1, nch, NSA_HEAD_DIM), lambda b, g: (b, g, 0, 0)),
        out_shape=jax.ShapeDtypeStruct((bsz, NSA_GROUPS, nch, NSA_HEAD_DIM), BF16),
        compiler_params=_params("parallel", "parallel"),
        name="compress",
    )(ch, nxt, pos.reshape(2, half), w1.astype(BF16), w2.astype(BF16))


NSA_TQ = 512
NSA_KC = 512
NSA_SUB = 256
MASK_VALUE = -1e30
SOFTMAX_M0 = -1e20


def _nsa_kernel(q_ref, kc_ref, vc_ref, ks_ref, vs_ref, kw_ref, vw_ref, gate_ref,
                overlap_ref, gexp_ref, o_ref):
    R, TQ, KC, DH, SUB = NSA_REP, NSA_TQ, NSA_KC, NSA_HEAD_DIM, NSA_SUB
    assert 2 * DH == LANES and TQ == KC and TQ % SUB == 0 and WINDOW % SUB == 0
    i = pl.program_id(2)
    q0 = i * TQ
    nt = (((1,), (1,)), ((), ()))
    tcol = q0 + lax.broadcasted_iota(jnp.int32, (TQ, 1), 0)
    qblk = q_ref[...]
    scale = DH ** -0.5
    qf = [qblk[:, r * DH:(r + 1) * DH].astype(F32) * scale for r in range(R)]
    qh = [q.astype(BF16) for q in qf]
    v_width = vs_ref.shape[3]

    def flash_step(m_i, acc, q, k, v_ext, mask_bias=None):
        s = lax.dot_general(q, k, nt, preferred_element_type=F32)
        if mask_bias is not None:
            heads = s.shape[0] // mask_bias.shape[0]
            s = (s.reshape((heads,) + mask_bias.shape) + mask_bias[None]).reshape(s.shape)
        m_new = jnp.maximum(m_i, jnp.max(s, axis=-1, keepdims=True))
        p = jnp.exp((s - m_new).astype(BF16))
        acc = acc * jnp.exp(m_i - m_new) + jnp.dot(p, v_ext, preferred_element_type=F32)
        return m_new, acc

    def spread(x, mat):
        hi = x.astype(BF16)
        lo = (x - hi.astype(F32)).astype(BF16)
        return jnp.dot(hi, mat, preferred_element_type=F32) + jnp.dot(lo, mat, preferred_element_type=F32)

    kc = kc_ref[0, 0]
    vc = vc_ref[0, 0]
    ncmp = kc.shape[0]
    cmp_end = lax.broadcasted_iota(jnp.int32, (1, ncmp), 1) * CMP_STRIDE + (CMP_BLOCK - 1)
    cmask = cmp_end <= tcol
    o_cmp = []
    p4 = None
    for r in range(R):
        lg = jnp.where(cmask, lax.dot_general(qh[r], kc, nt, preferred_element_type=F32), -jnp.inf)
        mx = jnp.max(lg, axis=-1, keepdims=True)
        mx = jnp.where(mx > -jnp.inf, mx, 0.0)
        e = jnp.where(cmask, jnp.exp(lg - mx), 0.0)
        p = e / jnp.maximum(jnp.sum(e, axis=-1, keepdims=True), 1e-30)
        o_cmp.append(jnp.dot(p.astype(BF16), vc, preferred_element_type=F32))
        p4 = p if p4 is None else p4 + p

    imp = sum(lax.dot_general(overlap_ref[...], part, nt, preferred_element_type=F32)
              for part in _split3(p4))
    nsel = imp.shape[0]
    trow = q0 + lax.broadcasted_iota(jnp.int32, (1, TQ), 1)
    cur = trow // SEL_BLOCK
    jj = lax.broadcasted_iota(jnp.int32, (nsel, 1), 0)
    forced = (jj == 0) | (jj == cur) | (jj == cur - 1)
    val = jnp.where(forced, jnp.inf, jnp.where(jj <= cur, imp, -jnp.inf))
    rank = jnp.zeros((nsel, TQ), jnp.int32)
    for b in range(nsel):
        vb = val[b:b + 1, :]
        beats = (vb > val) | ((vb == val) & (jj > b))
        rank = rank + beats.astype(jnp.int32)
    bias = jnp.transpose(jnp.where(rank < SEL_TOPN, 0.0, MASK_VALUE))
    pad = jnp.zeros((TQ, ks_ref.shape[3] - DH - nsel), F32)
    qa = [jnp.concatenate([qf[r], bias, pad], axis=1).astype(BF16) for r in range(R)]

    qa_all = jnp.concatenate(qa, axis=0)

    def sel_step(kb, carry):
        k0 = pl.multiple_of(kb * KC, KC)
        k = ks_ref[0, 0, pl.ds(k0, KC), :]
        v = vs_ref[0, 0, pl.ds(k0, KC), :]
        return flash_step(carry[0], carry[1], qa_all, k, v)

    n_full = q0 // KC
    m_all, acc_all = lax.fori_loop(0, n_full, sel_step, (jnp.full((R * TQ, 1), SOFTMAX_M0, F32),
                                                         jnp.zeros((R * TQ, v_width), F32)))
    carry = tuple((m_all[r * TQ:(r + 1) * TQ], acc_all[r * TQ:(r + 1) * TQ]) for r in range(R))

    d0 = pl.multiple_of(q0, TQ)
    stack = lambda parts: jnp.concatenate(parts, axis=0)
    sel_parts, win_parts = [], []
    for j in range(TQ // SUB):
        rows = slice(j * SUB, (j + 1) * SUB)
        tsub = tcol[rows]
        width = (j + 1) * SUB
        bias_d = jnp.where(d0 + lax.broadcasted_iota(jnp.int32, (1, width), 1) <= tsub, 0.0, MASK_VALUE)
        k_d = ks_ref[0, 0, pl.ds(d0, width), :]
        v_d = vs_ref[0, 0, pl.ds(d0, width), :]
        w0 = pl.multiple_of(jnp.maximum(q0 + j * SUB - WINDOW, 0), SUB)
        kpos_w = w0 + lax.broadcasted_iota(jnp.int32, (1, WINDOW + SUB), 1)
        bias_w = jnp.where((kpos_w <= tsub) & (kpos_w > tsub - WINDOW), 0.0, MASK_VALUE)
        k_w = kw_ref[0, 0, pl.ds(w0, WINDOW + SUB), :]
        v_w = vw_ref[0, 0, pl.ds(w0, WINDOW + SUB), :]
        sel_parts.append(flash_step(stack([carry[r][0][rows] for r in range(R)]),
                                    stack([carry[r][1][rows] for r in range(R)]),
                                    stack([qa[r][rows] for r in range(R)]), k_d, v_d, bias_d)[1])
        win_parts.append(flash_step(jnp.full((R * SUB, 1), SOFTMAX_M0, F32), jnp.zeros((R * SUB, v_width), F32),
                                    stack([qh[r][rows] for r in range(R)]), k_w, v_w, bias_w)[1])
    unstack = lambda parts, r: stack([p[r * SUB:(r + 1) * SUB] for p in parts])
    acc_s = [unstack(sel_parts, r) for r in range(R)]
    acc_w = [unstack(win_parts, r) for r in range(R)]

    g = _sigmoid(gate_ref[0, 0])
    g = jnp.concatenate([g, jnp.zeros((TQ, LANES - g.shape[1]), F32)], axis=1)
    gx = spread(g, gexp_ref[...])

    def normalised(accs):
        return jnp.concatenate([(acc / jnp.maximum(pltpu.roll(acc, DH, 1), 1e-30))[:, :DH] for acc in accs],
                               axis=1)

    w = R * DH
    out = (gx[:, 0:w] * jnp.concatenate(o_cmp, axis=1) + gx[:, w:2 * w] * normalised(acc_s)
           + gx[:, 2 * w:3 * w] * normalised(acc_w))
    o_ref[...] = out.astype(o_ref.dtype)


def _nsa(proj, gates, kc, vc, bsz, seq):
    G, R, DH, TQ = NSA_GROUPS, NSA_REP, NSA_HEAD_DIM, NSA_TQ
    n_cmp_pad = seq // CMP_STRIDE
    n_sel = seq // SEL_BLOCK
    nq = seq // TQ

    assert NSA_KC % TQ == 0 and seq % NSA_KC == 0 and TQ % WINDOW == 0

    def kv(idx, ext=None):
        lo = P_Q + G * R * DH + idx * NSA_KV_WIDTH
        a = jnp.transpose(proj[:, lo:lo + NSA_KV_WIDTH].reshape(bsz, seq, G, DH), (0, 2, 1, 3))
        if ext is not None:
            a = jnp.concatenate([a, jnp.broadcast_to(jnp.asarray(ext, BF16), (bsz, G) + ext.shape)], axis=-1)
        return a

    ones_col = np.ones((seq, LANES - DH), np.float32)
    block_onehot = np.zeros((seq, LANES - DH), np.float32)
    block_onehot[np.arange(seq), np.arange(seq) // SEL_BLOCK] = 1.0
    ks, vs, kw, vw = kv(2, block_onehot), kv(3, ones_col), kv(4), kv(5, ones_col)
    gt = jnp.transpose(gates.reshape(bsz, seq, G, R * 3), (0, 2, 1, 3))

    c_start = np.arange(n_cmp_pad) * CMP_STRIDE
    s_start = np.arange(n_sel) * SEL_BLOCK
    overlap = ((c_start[:, None] < s_start[None, :] + SEL_BLOCK)
               & (c_start[:, None] + CMP_BLOCK > s_start[None, :])).astype(np.float32)
    overlap[(seq - CMP_BLOCK) // CMP_STRIDE + 1:] = 0.0
    assert n_sel <= LANES - DH
    col = np.arange(3 * R * DH)
    gexp = (np.arange(LANES)[:, None] == (3 * ((col % (R * DH)) // DH) + col // (R * DH))[None, :]).astype(np.float32)

    qblk = pl.BlockSpec((TQ, R * DH), lambda b, g, i: (b * nq + i, P_Q // (R * DH) + g))
    oblk = pl.BlockSpec((TQ, R * DH), lambda b, g, i: (b * nq + i, g))
    cblk = pl.BlockSpec((1, 1, n_cmp_pad, DH), lambda b, g, i: (b, g, 0, 0))
    kblk = pl.BlockSpec((1, 1, seq, DH), lambda b, g, i: (b, g, 0, 0))
    vblk = pl.BlockSpec((1, 1, seq, LANES), lambda b, g, i: (b, g, 0, 0))
    return pl.pallas_call(
        _nsa_kernel,
        grid=(bsz, G, nq),
        in_specs=[
            qblk, cblk, cblk, vblk, vblk, kblk, vblk,
            pl.BlockSpec((1, 1, TQ, R * 3), lambda b, g, i: (b, g, i, 0)),
            pl.BlockSpec((n_sel, n_cmp_pad), lambda b, g, i: (0, 0)),
            pl.BlockSpec((LANES, 3 * R * DH), lambda b, g, i: (0, 0)),
        ],
        out_specs=oblk,
        out_shape=jax.ShapeDtypeStruct((bsz * seq, G * R * DH), BF16),
        compiler_params=_params("parallel", "parallel", "arbitrary"),
        name="nsa_attention",
    )(proj, kc, vc, ks, vs, kw, vw, gt, jnp.asarray(overlap.T, dtype=BF16), jnp.asarray(gexp, dtype=BF16))


ROW_TILES = D_MODEL // LANES


def _store_token_major(ref, val):
    n = val.shape[0]
    for c in range(ROW_TILES):
        ref[pl.ds(c, n, stride=ROW_TILES), :] = val[:, c * LANES:(c + 1) * LANES]


def _load_token_major(ref, n):
    return jnp.concatenate([ref[pl.ds(c, n, stride=ROW_TILES), :] for c in range(ROW_TILES)], axis=1)


def _merge_kernel(ys_ref, yn_ref, gs_ref, gn_ref, x_ref, wbs_ref, wbn_ref, wo_ref, fw_ref,
                  rwh_ref, rwl_ref, rb_ref, tri_ref, x1_ref, h_ref, ti_ref, tw_ref, slot_ref, cnt_ref, base_ref):
    @pl.when(pl.program_id(0) == 0)
    def _():
        base_ref[...] = jnp.zeros(base_ref.shape, F32)

    a = jnp.dot(ys_ref[...], wbs_ref[...], preferred_element_type=F32)
    b = jnp.dot(yn_ref[...], wbn_ref[...], preferred_element_type=F32)
    merged = _sigmoid(gs_ref[...].astype(F32)) * a + _sigmoid(gn_ref[...].astype(F32)) * b
    x1 = x_ref[...] + jnp.dot(merged.astype(BF16), wo_ref[...], preferred_element_type=F32)
    x1_ref[...] = x1
    var = jnp.mean(x1 * x1, axis=-1, keepdims=True)
    h = x1 * lax.rsqrt(var + NORM_EPS) * fw_ref[...]
    _store_token_major(h_ref, h)

    nt = (((1,), (1,)), ((), ()))
    h_hi = h.astype(BF16)
    h_lo = (h - h_hi.astype(F32)).astype(BF16)
    logits = (lax.dot_general(rwh_ref[...], h_hi, nt, preferred_element_type=F32)
              + (lax.dot_general(rwl_ref[...], h_hi, nt, preferred_element_type=F32)
                 + lax.dot_general(rwh_ref[...], h_lo, nt, preferred_element_type=F32))) + rb_ref[...]
    ne = logits.shape[0]
    jj = lax.broadcasted_iota(jnp.int32, (ne, 1), 0)
    rank = jnp.zeros(logits.shape, jnp.int32)
    for e in range(ne):
        ce = logits[e:e + 1, :]
        beats = (ce > logits) | ((ce == logits) & (jj > e))
        rank = rank + beats.astype(jnp.int32)
    sel = rank < TOP_K
    mx = jnp.max(logits, axis=0, keepdims=True)
    p = jnp.where(sel, jnp.exp(logits - mx), 0.0)
    p = p / jnp.sum(p, axis=0, keepdims=True)

    self01 = sel.astype(F32)
    before = jnp.dot(self01.astype(BF16), tri_ref[...], preferred_element_type=F32) + base_ref[...]
    for k in range(TOP_K):
        hit = rank == k
        ti_ref[k:k + 1, :] = jnp.sum(jnp.where(hit, jj, 0), axis=0, keepdims=True)
        tw_ref[k:k + 1, :] = jnp.sum(jnp.where(hit, p, 0.0), axis=0, keepdims=True)
        slot_ref[k:k + 1, :] = jnp.sum(jnp.where(hit, before, 0.0), axis=0, keepdims=True).astype(jnp.int32)
    base_ref[...] = base_ref[...] + jnp.sum(self01, axis=1, keepdims=True)
    cnt_ref[...] = base_ref[...]


def _merge_route(y_ssm, y_nsa, proj, x2, wbs, wbn, wo, ffn_w, router_w, router_b, tm):
    t, d = x2.shape
    row = lambda col: pl.BlockSpec((tm, d), lambda i: (i, col))
    const = lambda shape: pl.BlockSpec(shape, lambda i: (0,) * len(shape))
    kblk = pl.BlockSpec((TOP_K, tm), lambda i: (0, i))
    tri = np.triu(np.ones((tm, tm), np.float32), 1)
    rw_t = router_w.T
    rw_hi = rw_t.astype(BF16)
    rw_lo = (rw_t - rw_hi.astype(F32)).astype(BF16)
    return pl.pallas_call(
        _merge_kernel,
        grid=(t // tm,),
        in_specs=[
            row(0), row(0), row(F_MS // d), row(F_MN // d), row(0),
            const((d, d)), const((d, d)), const((d, d)), const((1, d)),
            const((N_EXPERTS, d)), const((N_EXPERTS, d)), const((N_EXPERTS, 1)), const((tm, tm)),
        ],
        out_specs=[row(0), pl.BlockSpec((tm * ROW_TILES, LANES), lambda i: (i, 0)),
                   kblk, kblk, kblk, const((N_EXPERTS, 1))],
        out_shape=[jax.ShapeDtypeStruct((t, d), F32), jax.ShapeDtypeStruct((t * ROW_TILES, LANES), F32),
                   jax.ShapeDtypeStruct((TOP_K, t), jnp.int32), jax.ShapeDtypeStruct((TOP_K, t), F32),
                   jax.ShapeDtypeStruct((TOP_K, t), jnp.int32), jax.ShapeDtypeStruct((N_EXPERTS, 1), F32)],
        scratch_shapes=[pltpu.VMEM((N_EXPERTS, 1), F32)],
        compiler_params=_params("arbitrary"),
        name="merge_route",
    )(y_ssm, y_nsa, proj, proj, x2, wbs.astype(BF16), wbn.astype(BF16), wo.astype(BF16),
      ffn_w.reshape(1, d), rw_hi, rw_lo, router_b.reshape(N_EXPERTS, 1), jnp.asarray(tri, dtype=BF16))


MOE_TM = 256
GU_CHUNK = 2 * LANES


def _moe_kernel(te_ref, nu_ref, tok0_ref, tokn_ref, dst_ref, h_hbm, wgu_ref, bgu_ref, wd_ref, bd_ref, perm_ref,
                y_hbm, xbuf, ybuf, wgu_s, wd_s, gsem, ssem, *, spare_row):
    tm, rt = MOE_TM, ROW_TILES
    i = pl.program_id(0)
    nu = nu_ref[0]
    slot = i % 2
    other = 1 - slot
    new_expert = (i == 0) | (te_ref[i] != te_ref[jnp.maximum(i - 1, 0)])

    @pl.when(new_expert & (i < nu))
    def _():
        wd_s[...] = wd_ref[0].astype(BF16)
        for c in range(wgu_s.shape[1] // GU_CHUNK):
            cols = slice(c * GU_CHUNK, (c + 1) * GU_CHUNK)
            wgu_s[:, cols] = jnp.dot(wgu_ref[0, :, cols].astype(BF16), perm_ref[...],
                                     preferred_element_type=F32).astype(BF16)

    def gather_copy(src_row, r, sl):
        return pltpu.make_async_copy(h_hbm.at[pl.ds(src_row, rt), :], xbuf.at[sl, pl.ds(r * rt, rt), :],
                                     gsem.at[sl])

    def scatter_copy(dst_row, r, sl):
        return pltpu.make_async_copy(ybuf.at[sl, pl.ds(r * rt, rt), :], y_hbm.at[pl.ds(dst_row, rt), :],
                                     ssem.at[sl])

    def start_gather(tok_ref, sl):
        for r in range(tm):
            gather_copy(pl.multiple_of(tok_ref[0, 0, r], rt), r, sl).start()

    def start_scatter(sl):
        for r in range(tm):
            scatter_copy(pl.multiple_of(dst_ref[0, 0, r], rt), r, sl).start()

    def wait_gather(sl):
        pltpu.make_async_copy(h_hbm.at[pl.ds(0, tm * rt), :], xbuf.at[sl], gsem.at[sl]).wait()

    def wait_scatter(sl):
        pltpu.make_async_copy(ybuf.at[sl], y_hbm.at[pl.ds(0, tm * rt), :], ssem.at[sl]).wait()

    @pl.when(i == 0)
    def _():
        ybuf[...] = jnp.zeros(ybuf.shape, ybuf.dtype)
        start_gather(tok0_ref, 0)
        for r in range(tm):
            scatter_copy((spare_row + r) * rt, r, 0).start()

    @pl.when(i < nu)
    def _():
        wait_gather(slot)
        start_gather(tokn_ref, other)
        wait_scatter(slot)
        start_scatter(other)
        x = _load_token_major(xbuf.at[slot], tm).astype(BF16)
        h1 = jnp.dot(x, wgu_s[...], preferred_element_type=F32) + bgu_ref[0]
        nchunk = h1.shape[1] // GU_CHUNK
        glu = jnp.concatenate([h1[:, c * GU_CHUNK:c * GU_CHUNK + LANES] for c in range(nchunk)], axis=1)
        lin = jnp.concatenate([h1[:, c * GU_CHUNK + LANES:(c + 1) * GU_CHUNK] for c in range(nchunk)], axis=1)
        glu = jnp.minimum(glu, SWIGLU_LIMIT)
        lin = jnp.clip(lin, -SWIGLU_LIMIT, SWIGLU_LIMIT)
        act = glu * _sigmoid(SWIGLU_ALPHA * glu) * (lin + 1.0)
        y = jnp.dot(act.astype(BF16), wd_s[...], preferred_element_type=F32) + bd_ref[0]
        _store_token_major(ybuf.at[slot], y)

    @pl.when(i == nu)
    def _():
        wait_gather(slot)
        wait_scatter(slot)
        start_scatter(other)
        wait_scatter(other)


def _moe_experts(h_rows, tok_tiles, dst_tiles, tile_expert, n_used, wgu, bgu, wd, bd, n_out_rows, spare_row):
    d = wd.shape[1]
    tm, rt = MOE_TM, ROW_TILES
    n_tiles = tok_tiles.shape[0]
    perm = np.zeros((GU_CHUNK, GU_CHUNK), np.float32)
    perm[2 * np.arange(LANES), np.arange(LANES)] = 1.0
    perm[2 * np.arange(LANES) + 1, LANES + np.arange(LANES)] = 1.0
    smem = lambda f: pl.BlockSpec((1, 1, tm), f, memory_space=pltpu.SMEM)
    grid_spec = pltpu.PrefetchScalarGridSpec(
        num_scalar_prefetch=2,
        grid=(n_tiles,),
        in_specs=[
            smem(lambda i, te, nu: (0, 0, 0)),
            smem(lambda i, te, nu: (jnp.minimum(i + 1, n_tiles - 1), 0, 0)),
            smem(lambda i, te, nu: (i, 0, 0)),
            pl.BlockSpec(memory_space=pl.ANY),
            pl.BlockSpec((1, d, 2 * d), lambda i, te, nu: (te[i], 0, 0)),
            pl.BlockSpec((1, 1, 2 * d), lambda i, te, nu: (te[i], 0, 0)),
            pl.BlockSpec((1, d, d), lambda i, te, nu: (te[i], 0, 0)),
            pl.BlockSpec((1, 1, d), lambda i, te, nu: (te[i], 0, 0)),
            pl.BlockSpec((GU_CHUNK, GU_CHUNK), lambda i, te, nu: (0, 0)),
        ],
        out_specs=pl.BlockSpec(memory_space=pl.ANY),
        scratch_shapes=[
            pltpu.VMEM((2, tm * rt, LANES), F32),
            pltpu.VMEM((2, tm * rt, LANES), F32),
            pltpu.VMEM((d, 2 * d), BF16),
            pltpu.VMEM((d, d), BF16),
            pltpu.SemaphoreType.DMA((2,)),
            pltpu.SemaphoreType.DMA((2,)),
        ],
    )
    return pl.pallas_call(
        functools.partial(_moe_kernel, spare_row=spare_row),
        grid_spec=grid_spec,
        out_shape=jax.ShapeDtypeStruct((n_out_rows * rt, LANES), F32),
        compiler_params=_params("arbitrary"),
        name="moe_experts",
    )(tile_expert, n_used, tok_tiles, tok_tiles, dst_tiles, h_rows, wgu, bgu, wd, bd, jnp.asarray(perm, dtype=BF16))


def _final_kernel(x1_ref, y0_ref, y1_ref, y2_ref, y3_ref, tw_ref, nw_ref, o_ref):
    tm = x1_ref.shape[0]
    tw = tw_ref[...]
    yk = [_load_token_major(r, tm) for r in (y0_ref, y1_ref, y2_ref, y3_ref)]
    moe = ((tw[:, 0:1] * yk[0] + tw[:, 1:2] * yk[1]) + (tw[:, 2:3] * yk[2] + tw[:, 3:4] * yk[3]))
    x = x1_ref[...] + moe
    var = jnp.mean(x * x, axis=-1, keepdims=True)
    o_ref[...] = x * lax.rsqrt(var + NORM_EPS) * nw_ref[...]


def _final_norm(x1, y_rows, top_w, norm_w, tm):
    t, d = x1.shape
    nt = t // tm
    yblk = lambda k: pl.BlockSpec((tm * ROW_TILES, LANES), lambda i: (k * nt + i, 0))
    return pl.pallas_call(
        _final_kernel,
        grid=(nt,),
        in_specs=[
            pl.BlockSpec((tm, d), lambda i: (i, 0)),
            yblk(0), yblk(1), yblk(2), yblk(3),
            pl.BlockSpec((tm, TOP_K), lambda i: (i, 0)),
            pl.BlockSpec((1, d), lambda i: (0, 0)),
        ],
        out_specs=pl.BlockSpec((tm, d), lambda i: (i, 0)),
        out_shape=jax.ShapeDtypeStruct((t, d), F32),
        compiler_params=_params("parallel"),
        name="final_norm",
    )(x1, y_rows, y_rows, y_rows, y_rows, top_w, norm_w.reshape(1, d))


def _dispatch_tables(top_i, slot, counts, n_rows):
    t = top_i.shape[1]
    n = t * TOP_K
    tm, rt = MOE_TM, ROW_TILES
    padded = ((counts + tm - 1) // tm) * tm
    pend = jnp.cumsum(padded)
    pstart = pend - padded
    onehot = top_i[:, :, None] == jnp.arange(N_EXPERTS, dtype=jnp.int32)
    dest = slot + jnp.sum(jnp.where(onehot, pstart, 0), axis=-1)
    row_pair = jnp.full((n_rows,), -1, jnp.int32).at[dest.reshape(-1)].set(jnp.arange(n, dtype=jnp.int32))
    live = row_pair >= 0
    tok_tiles = (jnp.where(live, row_pair % t, 0) * rt).reshape(n_rows // tm, 1, tm)
    rows = jnp.arange(n_rows, dtype=jnp.int32)
    out_row = jnp.where(live, row_pair, n + tm + rows % tm)
    spare = n + tm + jnp.arange(tm, dtype=jnp.int32)
    dst_tiles = (jnp.concatenate([spare, out_row]) * rt).reshape(n_rows // tm + 1, 1, tm)
    tile_start = jnp.arange(n_rows // tm, dtype=jnp.int32) * tm
    tile_expert = jnp.minimum(jnp.sum(tile_start[:, None] >= pend[None, :], axis=-1), N_EXPERTS - 1)
    n_used = (pend[-1] // tm).reshape(1)
    return tok_tiles, dst_tiles, tile_expert.astype(jnp.int32), n_used.astype(jnp.int32)


def _in_proj_weights(w_in):
    o = np.cumsum([0, D_INNER, XBC_DIM, SSM_HEADS, 1024, 256, 256, 256, 256, 256, 256, 3 * NSA_HEADS, 2 * D_MODEL])
    z, xbc, dt, q, kv, gate, mg = (w_in[:, o[0]:o[1]], w_in[:, o[1]:o[2]], w_in[:, o[2]:o[3]],
                                   w_in[:, o[3]:o[4]], w_in[:, o[4]:o[10]], w_in[:, o[10]:o[11]],
                                   w_in[:, o[11]:o[12]])
    small_pad = jnp.zeros((D_MODEL, LANES - SSM_HEADS - 3 * NSA_HEADS), w_in.dtype)
    w_p = jnp.concatenate([z, mg[:, :D_MODEL], xbc, mg[:, D_MODEL:], q, kv], axis=1)
    w_small = jnp.concatenate([dt, gate, small_pad], axis=1)
    return w_p.astype(BF16), w_small.astype(BF16)


def kernel(x, mix_norm_w, w_in, conv_w, conv_b, dt_bias, a_log, d_skip, ssm_norm_w, cmp_pos_k, cmp_w1_k, cmp_w2_k, cmp_pos_v, cmp_w1_v, cmp_w2_v, w_branch_ssm, w_branch_nsa, w_out, ffn_norm_w, router_w, router_b, w_gate_up, b_gate_up, w_down, b_down, final_norm_w):
    bsz, seq, d = x.shape
    t = bsz * seq
    x2 = x.reshape(t, d)
    depth = w_in.shape[0]
    assert depth == 1, "single-layer block"
    for l in range(depth):
        w_p, w_small = _in_proj_weights(w_in[l])
        proj = _norm_matmul(x2, mix_norm_w[l], w_p, BF16, 2048, 512)
        small = _norm_matmul(x2, mix_norm_w[l], w_small, F32, 2048, LANES)

        dt_t = jnp.transpose(small[:, :SSM_HEADS].reshape(bsz, seq, SSM_HEADS), (0, 2, 1))
        y_ssm = _ssd(proj, small, dt_t, conv_w[l], conv_b[l], dt_bias[l], a_log[l], d_skip[l], ssm_norm_w[l],
                     bsz, seq)

        kv0 = P_Q + NSA_HEADS * NSA_HEAD_DIM
        kc = _compress(proj[:, kv0:kv0 + NSA_KV_WIDTH], cmp_pos_k[l], cmp_w1_k[l], cmp_w2_k[l], bsz, seq)
        vc = _compress(proj[:, kv0 + NSA_KV_WIDTH:kv0 + 2 * NSA_KV_WIDTH], cmp_pos_v[l], cmp_w1_v[l], cmp_w2_v[l],
                       bsz, seq)
        gates = small[:, SSM_HEADS:SSM_HEADS + 3 * NSA_HEADS]
        y_nsa = _nsa(proj, gates, kc, vc, bsz, seq)

        x1, h, top_i, top_w, slot, counts = _merge_route(
            y_ssm, y_nsa, proj, x2, w_branch_ssm[l], w_branch_nsa[l], w_out[l], ffn_norm_w[l],
            router_w[l], router_b[l], 512)

        n_rows = t * TOP_K + N_EXPERTS * MOE_TM
        tok_tiles, dst_tiles, tile_expert, n_used = _dispatch_tables(
            top_i, slot, counts[:, 0].astype(jnp.int32), n_rows)
        nchunk = 2 * d // GU_CHUNK
        bgu = jnp.transpose(b_gate_up[l].reshape(N_EXPERTS, nchunk, LANES, 2), (0, 1, 3, 2))
        y_rows = _moe_experts(h, tok_tiles, dst_tiles, tile_expert, n_used, w_gate_up[l],
                              bgu.reshape(N_EXPERTS, 1, 2 * d), w_down[l], b_down[l][:, None, :],
                              n_out_rows=t * TOP_K + 2 * MOE_TM, spare_row=t * TOP_K)
    out = _final_norm(x1, y_rows, top_w.T, final_norm_w, 512)
    return out.reshape(bsz, seq, d)
```

```python
import functools

import numpy as np
import jax
import jax.numpy as jnp
from jax import lax
from jax.experimental import pallas as pl
from jax.experimental.pallas import tpu as pltpu

F32 = jnp.float32
BF16 = jnp.bfloat16

D_MODEL = 1024
D_INNER = 1024
SSM_HEAD_DIM = 64
SSM_HEADS = 16
SSM_GROUPS = 4
D_STATE = 128
CONV_K = 4
XBC_DIM = D_INNER + 2 * SSM_GROUPS * D_STATE
SSD_CHUNK = 128
NSA_HEAD_DIM = 64
NSA_HEADS = 16
NSA_GROUPS = 4
NSA_REP = 4
NSA_KV_WIDTH = 256
CMP_BLOCK = 32
CMP_STRIDE = 16
CMP_HIDDEN = 128
SEL_BLOCK = 64
SEL_TOPN = 8
WINDOW = 256
N_EXPERTS = 32
TOP_K = 4
SWIGLU_ALPHA = 1.702
SWIGLU_LIMIT = 7.0
NORM_EPS = 1e-5

LANES = 128
VMEM_LIMIT = 56 * 1024 * 1024

F_Z, F_MS, F_XBC, F_MN, P_Q, P_KC, P_VC, P_KSW, P_VSW = 0, 1024, 2048, 4096, 5120, 6144, 6400, 6656, 7168


def _sigmoid(x):
    return 0.5 * jnp.tanh(0.5 * x) + 0.5


def _silu(x):
    return x * _sigmoid(x)


def _softplus(x):
    return jnp.maximum(x, 0.0) + jnp.log1p(jnp.exp(-jnp.abs(x)))


def _split3(x):
    hi = x.astype(BF16)
    r1 = x - hi.astype(F32)
    mid = r1.astype(BF16)
    lo = (r1 - mid.astype(F32)).astype(BF16)
    return hi, mid, lo


def _params(*sem):
    return pltpu.CompilerParams(dimension_semantics=sem, vmem_limit_bytes=VMEM_LIMIT)


def _norm_matmul_kernel(x_ref, nw_ref, w_ref, o_ref, h_ref):
    @pl.when(pl.program_id(1) == 0)
    def _():
        x = x_ref[...]
        var = jnp.mean(x * x, axis=-1, keepdims=True)
        h_ref[...] = (x * lax.rsqrt(var + NORM_EPS) * nw_ref[...]).astype(BF16)

    o_ref[...] = jnp.dot(h_ref[...], w_ref[...], preferred_element_type=F32).astype(o_ref.dtype)


def _norm_matmul(x2, norm_w, w, out_dtype, tm, tn):
    t, d = x2.shape
    n = w.shape[1]
    return pl.pallas_call(
        _norm_matmul_kernel,
        grid=(t // tm, n // tn),
        in_specs=[
            pl.BlockSpec((tm, d), lambda i, j: (i, 0)),
            pl.BlockSpec((1, d), lambda i, j: (0, 0)),
            pl.BlockSpec((d, tn), lambda i, j: (0, j)),
        ],
        out_specs=pl.BlockSpec((tm, tn), lambda i, j: (i, j)),
        out_shape=jax.ShapeDtypeStruct((t, n), out_dtype),
        scratch_shapes=[pltpu.VMEM((tm, d), BF16)],
        compiler_params=_params("parallel", "arbitrary"),
        name="norm_matmul",
    )(x2, norm_w.reshape(1, d), w)


def _ssd_kernel(z_ref, xbc_ref, sm_ref, dtt_ref, convw_ref, convb_ref, dtb_ref, dtbt_ref,
                alog_ref, alogt_ref, dskip_ref, nw_ref, expand_ref, y_ref,
                xbuf, state, ydiag):
    L = SSD_CHUNK
    c = pl.program_id(1)

    @pl.when(c == 0)
    def _():
        xbuf[0:8, :] = jnp.zeros((8, XBC_DIM), F32)
        state[...] = jnp.zeros(state.shape, F32)

    xbuf[8:8 + L, :] = xbc_ref[...].astype(F32)
    acc = convb_ref[...] + convw_ref[0:1, :] * xbuf[5:5 + L, :]
    for k in range(1, CONV_K):
        acc = acc + convw_ref[k:k + 1, :] * xbuf[5 + k:5 + k + L, :]
    xbuf[0:8, :] = xbuf[L:L + 8, :]
    xbc = _silu(acc)
    xs = xbc[:, :D_INNER]

    lane = lax.broadcasted_iota(jnp.int32, (1, LANES), 1)
    a_row = jnp.where(lane < SSM_HEADS, -jnp.exp(alog_ref[...]), 0.0)
    dt = _softplus(sm_ref[...] + dtb_ref[...])
    a_dt = dt * a_row
    dt_t = _softplus(dtt_ref[0] + dtbt_ref[...])
    a_dt_t = dt_t * (-jnp.exp(alogt_ref[...]))

    row = lax.broadcasted_iota(jnp.int32, (L, L), 0)
    col = lax.broadcasted_iota(jnp.int32, (L, L), 1)
    lower = row >= col
    tri = lower.astype(BF16)
    tri_t = (row <= col).astype(BF16)
    cs = sum(jnp.dot(tri, part, preferred_element_type=F32) for part in _split3(a_dt))
    cs_t = sum(jnp.dot(part, tri_t, preferred_element_type=F32) for part in _split3(a_dt_t))
    cs_last = cs[L - 1:L, :]

    stacked = jnp.concatenate([dt, jnp.exp(cs_last - cs), jnp.exp(cs)], axis=0)
    wide = sum(jnp.dot(part, expand_ref[...], preferred_element_type=F32) for part in _split3(stacked))
    dt_x = wide[0:L]
    dte_x = wide[L:2 * L]
    ecs_x = wide[2 * L:3 * L]

    xdt = xs * dt_x
    xw = (xdt * dte_x).astype(BF16)
    xdt_b = xdt.astype(BF16)

    hpg = SSM_HEADS // SSM_GROUPS
    gw = hpg * SSM_HEAD_DIM
    y_off_parts = []
    for g in range(SSM_GROUPS):
        b_g = xbc[:, D_INNER + g * D_STATE:D_INNER + (g + 1) * D_STATE].astype(BF16)
        c_g = xbc[:, D_INNER + SSM_GROUPS * D_STATE + g * D_STATE:
                  D_INNER + SSM_GROUPS * D_STATE + (g + 1) * D_STATE].astype(BF16)
        cb = lax.dot_general(c_g, b_g, (((1,), (1,)), ((), ())), preferred_element_type=F32)
        for hh in range(hpg):
            h = g * hpg + hh
            seg = cs[:, h:h + 1] - cs_t[h:h + 1, :]
            decay = jnp.exp(jnp.where(lower, seg, -jnp.inf))
            m = (cb * decay).astype(BF16)
            ydiag[:, h * SSM_HEAD_DIM:(h + 1) * SSM_HEAD_DIM] = jnp.dot(
                m, xdt_b[:, h * SSM_HEAD_DIM:(h + 1) * SSM_HEAD_DIM], preferred_element_type=F32)
        st_prev = state[g]
        y_off_parts.append(jnp.dot(c_g, st_prev.astype(BF16), preferred_element_type=F32))
        st_new = lax.dot_general(b_g, xw[:, g * gw:(g + 1) * gw], (((0,), (0,)), ((), ())),
                                 preferred_element_type=F32)
        state[g] = st_prev * ecs_x[L - 1:L, g * gw:(g + 1) * gw] + st_new
    y_off = jnp.concatenate(y_off_parts, axis=1) * ecs_x

    y = (ydiag[...] + y_off + dskip_ref[...] * xs) * _silu(z_ref[...].astype(F32))
    for g in range(SSM_GROUPS):
        yg = y[:, g * gw:(g + 1) * gw]
        yg = yg * lax.rsqrt(jnp.mean(yg * yg, axis=-1, keepdims=True) + NORM_EPS)
        y_ref[:, g * gw:(g + 1) * gw] = (yg * nw_ref[:, g * gw:(g + 1) * gw]).astype(y_ref.dtype)


def _ssd(proj, small, dt_t, conv_w, conv_b, dt_bias, a_log, d_skip, norm_w, bsz, seq):
    L = SSD_CHUNK
    nc = seq // L
    pad = LANES - SSM_HEADS
    dtb = jnp.pad(dt_bias, (0, pad)).reshape(1, LANES)
    alog = jnp.pad(a_log, (0, pad)).reshape(1, LANES)
    dskip_x = jnp.repeat(d_skip, SSM_HEAD_DIM).reshape(1, D_INNER)
    expand = (np.arange(LANES)[:, None] == (np.arange(D_INNER)[None, :] // SSM_HEAD_DIM)).astype(np.float32)
    const = lambda shape: pl.BlockSpec(shape, lambda b, c: (0,) * len(shape))
    return pl.pallas_call(
        _ssd_kernel,
        grid=(bsz, nc),
        in_specs=[
            pl.BlockSpec((L, D_INNER), lambda b, c: (b * nc + c, F_Z // D_INNER)),
            pl.BlockSpec((L, XBC_DIM), lambda b, c: (b * nc + c, F_XBC // XBC_DIM)),
            pl.BlockSpec((L, LANES), lambda b, c: (b * nc + c, 0)),
            pl.BlockSpec((1, SSM_HEADS, L), lambda b, c: (b, 0, c)),
            const((CONV_K, XBC_DIM)), const((1, XBC_DIM)),
            const((1, LANES)), const((SSM_HEADS, 1)),
            const((1, LANES)), const((SSM_HEADS, 1)),
            const((1, D_INNER)), const((1, D_INNER)),
            const((LANES, D_INNER)),
        ],
        out_specs=pl.BlockSpec((L, D_INNER), lambda b, c: (b * nc + c, 0)),
        out_shape=jax.ShapeDtypeStruct((bsz * seq, D_INNER), BF16),
        scratch_shapes=[
            pltpu.VMEM((L + 8, XBC_DIM), F32),
            pltpu.VMEM((SSM_GROUPS, D_STATE, (SSM_HEADS // SSM_GROUPS) * SSM_HEAD_DIM), F32),
            pltpu.VMEM((L, D_INNER), F32),
        ],
        compiler_params=_params("parallel", "arbitrary"),
        name="ssd",
    )(proj, proj, small, dt_t, conv_w, conv_b.reshape(1, XBC_DIM),
      dtb, dt_bias.reshape(SSM_HEADS, 1), alog, a_log.reshape(SSM_HEADS, 1),
      dskip_x, norm_w.reshape(1, D_INNER), jnp.asarray(expand, dtype=BF16))


def _gelu_tanh(x):
    return 0.5 * x * (1.0 + jnp.tanh(np.sqrt(2.0 / np.pi) * (x + 0.044715 * (x * x * x))))


def _compress_kernel(cur_ref, nxt_ref, pos_ref, w1_ref, w2_ref, o_ref):
    half = CMP_STRIDE * NSA_HEAD_DIM
    lo = (cur_ref[0, 0].astype(F32) + pos_ref[0:1, :]).astype(BF16)
    hi = (nxt_ref[0, 0].astype(F32) + pos_ref[1:2, :]).astype(BF16)
    hid = (jnp.dot(lo, w1_ref[0:half, :], preferred_element_type=F32)
           + jnp.dot(hi, w1_ref[half:2 * half, :], preferred_element_type=F32))
    o_ref[0, 0] = jnp.dot(_gelu_tanh(hid).astype(BF16), w2_ref[...],
                          preferred_element_type=F32).astype(o_ref.dtype)


def _compress(kv, pos, w1, w2, bsz, seq):
    nch = seq // CMP_STRIDE
    half = CMP_STRIDE * NSA_HEAD_DIM
    ch = kv.reshape(bsz, nch, CMP_STRIDE, NSA_GROUPS, NSA_HEAD_DIM)
    ch = jnp.transpose(ch, (0, 3, 1, 2, 4)).reshape(bsz, NSA_GROUPS, nch, half)
    nxt = jnp.concatenate([ch[:, :, 1:], jnp.zeros_like(ch[:, :, :1])], axis=2)
    blk = pl.BlockSpec((1, 1, nch, half), lambda b, g: (b, g, 0, 0))
    return pl.pallas_call(
        _compress_kernel,
        grid=(bsz, NSA_GROUPS),
        in_specs=[
            blk, blk,
            pl.BlockSpec((2, half), lambda b, g: (0, 0)),
            pl.BlockSpec((2 * half, CMP_HIDDEN), lambda b, g: (0, 0)),
            pl.BlockSpec((CMP_HIDDEN, NSA_HEAD_DIM), lambda b, g: (0, 0)),
        ],
        out_specs=pl.BlockSpec((1, 1, nch, NSA_HEAD_DIM), lambda b, g: (b, g, 0, 0)),
        out_shape=jax.ShapeDtypeStruct((bsz, NSA_GROUPS, nch, NSA_HEAD_DIM), BF16),
        compiler_params=_params("parallel", "parallel"),
        name="compress",
    )(ch, nxt, pos.reshape(2, half), w1.astype(BF16), w2.astype(BF16))


NSA_TQ = 512
NSA_KC = 512
NSA_SUB = 256
MASK_VALUE = -1e30
SOFTMAX_M0 = -1e20


def _nsa_kernel(q_ref, kc_ref, vc_ref, k_ref, v_ref, onehot_ref, gate_ref, overlap_ref, gexp_ref, o_ref):
    R, TQ, KC, DH, SUB = NSA_REP, NSA_TQ, NSA_KC, NSA_HEAD_DIM, NSA_SUB
    assert 2 * DH == LANES and TQ == KC and TQ % SUB == 0 and WINDOW % SUB == 0
    i = pl.program_id(2)
    q0 = i * TQ
    nt = (((1,), (1,)), ((), ()))
    tcol = q0 + lax.broadcasted_iota(jnp.int32, (TQ, 1), 0)
    qblk = q_ref[...]
    scale = DH ** -0.5
    qf = [qblk[:, r * DH:(r + 1) * DH].astype(F32) * scale for r in range(R)]
    qh = [q.astype(BF16) for q in qf]
    v_width = LANES
    sel_half = lax.broadcasted_iota(jnp.int32, (1, LANES), 1) < DH
    one = jnp.ones((), BF16)

    def flash_step(m_i, acc, q, k, v_ext, mask_bias=None):
        s = lax.dot_general(q, k, nt, preferred_element_type=F32)
        if mask_bias is not None:
            heads = s.shape[0] // mask_bias.shape[0]
            s = (s.reshape((heads,) + mask_bias.shape) + mask_bias[None]).reshape(s.shape)
        m_new = jnp.maximum(m_i, jnp.max(s, axis=-1, keepdims=True))
        p = jnp.exp((s - m_new).astype(BF16))
        acc = acc * jnp.exp(m_i - m_new) + jnp.dot(p, v_ext, preferred_element_type=F32)
        return m_new, acc

    def spread(x, mat):
        hi = x.astype(BF16)
        lo = (x - hi.astype(F32)).astype(BF16)
        return jnp.dot(hi, mat, preferred_element_type=F32) + jnp.dot(lo, mat, preferred_element_type=F32)

    kc = kc_ref[0, 0]
    vc = vc_ref[0, 0]
    ncmp = kc.shape[0]
    cmp_end = lax.broadcasted_iota(jnp.int32, (1, ncmp), 1) * CMP_STRIDE + (CMP_BLOCK - 1)
    cmask = cmp_end <= tcol
    o_cmp = []
    p4 = None
    for r in range(R):
        lg = jnp.where(cmask, lax.dot_general(qh[r], kc, nt, preferred_element_type=F32), -jnp.inf)
        mx = jnp.max(lg, axis=-1, keepdims=True)
        mx = jnp.where(mx > -jnp.inf, mx, 0.0)
        e = jnp.where(cmask, jnp.exp(lg - mx), 0.0)
        p = e / jnp.maximum(jnp.sum(e, axis=-1, keepdims=True), 1e-30)
        o_cmp.append(jnp.dot(p.astype(BF16), vc, preferred_element_type=F32))
        p4 = p if p4 is None else p4 + p

    imp = sum(lax.dot_general(overlap_ref[...], part, nt, preferred_element_type=F32)
              for part in _split3(p4))
    nsel = imp.shape[0]
    trow = q0 + lax.broadcasted_iota(jnp.int32, (1, TQ), 1)
    cur = trow // SEL_BLOCK
    jj = lax.broadcasted_iota(jnp.int32, (nsel, 1), 0)
    forced = (jj == 0) | (jj == cur) | (jj == cur - 1)
    val = jnp.where(forced, jnp.inf, jnp.where(jj <= cur, imp, -jnp.inf))
    rank = jnp.zeros((nsel, TQ), jnp.int32)
    for b in range(nsel):
        vb = val[b:b + 1, :]
        beats = (vb > val) | ((vb == val) & (jj > b))
        rank = rank + beats.astype(jnp.int32)
    bias = jnp.transpose(jnp.where(rank < SEL_TOPN, 0.0, MASK_VALUE))
    pad = jnp.zeros((TQ, LANES - DH - nsel), F32)
    qa = [jnp.concatenate([qf[r], bias, pad], axis=1).astype(BF16) for r in range(R)]
    qw = [jnp.concatenate([jnp.zeros((TQ, DH), F32), qf[r]], axis=1).astype(BF16) for r in range(R)]

    qa_all = jnp.concatenate(qa, axis=0)

    def sel_kv(k0, n):
        k = jnp.where(sel_half, k_ref[pl.ds(k0, n), :], onehot_ref[pl.ds(k0, n), :])
        v = jnp.where(sel_half, v_ref[pl.ds(k0, n), :], one)
        return k, v

    def sel_step(kb, carry):
        k, v = sel_kv(pl.multiple_of(kb * KC, KC), KC)
        return flash_step(carry[0], carry[1], qa_all, k, v)

    n_full = q0 // KC
    m_all, acc_all = lax.fori_loop(0, n_full, sel_step, (jnp.full((R * TQ, 1), SOFTMAX_M0, F32),
                                                         jnp.zeros((R * TQ, v_width), F32)))
    carry = tuple((m_all[r * TQ:(r + 1) * TQ], acc_all[r * TQ:(r + 1) * TQ]) for r in range(R))

    d0 = pl.multiple_of(q0, TQ)
    stack = lambda parts: jnp.concatenate(parts, axis=0)
    sel_parts, win_parts = [], []
    for j in range(TQ // SUB):
        rows = slice(j * SUB, (j + 1) * SUB)
        tsub = tcol[rows]
        width = (j + 1) * SUB
        bias_d = jnp.where(d0 + lax.broadcasted_iota(jnp.int32, (1, width), 1) <= tsub, 0.0, MASK_VALUE)
        k_d, v_d = sel_kv(d0, width)
        w0 = pl.multiple_of(jnp.maximum(q0 + j * SUB - WINDOW, 0), SUB)
        kpos_w = w0 + lax.broadcasted_iota(jnp.int32, (1, WINDOW + SUB), 1)
        bias_w = jnp.where((kpos_w <= tsub) & (kpos_w > tsub - WINDOW), 0.0, MASK_VALUE)
        k_w = k_ref[pl.ds(w0, WINDOW + SUB), :]
        v_w = jnp.where(sel_half, one, v_ref[pl.ds(w0, WINDOW + SUB), :])
        sel_parts.append(flash_step(stack([carry[r][0][rows] for r in range(R)]),
                                    stack([carry[r][1][rows] for r in range(R)]),
                                    stack([qa[r][rows] for r in range(R)]), k_d, v_d, bias_d)[1])
        win_parts.append(flash_step(jnp.full((R * SUB, 1), SOFTMAX_M0, F32), jnp.zeros((R * SUB, v_width), F32),
                                    stack([qw[r][rows] for r in range(R)]), k_w, v_w, bias_w)[1])
    unstack = lambda parts, r: stack([p[r * SUB:(r + 1) * SUB] for p in parts])
    acc_s = [unstack(sel_parts, r) for r in range(R)]
    acc_w = [unstack(win_parts, r) for r in range(R)]

    gx = spread(_sigmoid(gate_ref[...]), gexp_ref[0])

    def normalised(accs, lo):
        return jnp.concatenate([(acc / jnp.maximum(pltpu.roll(acc, DH, 1), 1e-30))[:, lo:lo + DH] for acc in accs],
                               axis=1)

    w = R * DH
    out = (gx[:, 0:w] * jnp.concatenate(o_cmp, axis=1) + gx[:, w:2 * w] * normalised(acc_s, 0)
           + gx[:, 2 * w:3 * w] * normalised(acc_w, DH))
    o_ref[...] = out.astype(o_ref.dtype)


def _nsa(proj, small, kc, vc, bsz, seq):
    G, R, DH, TQ = NSA_GROUPS, NSA_REP, NSA_HEAD_DIM, NSA_TQ
    n_cmp_pad = seq // CMP_STRIDE
    n_sel = seq // SEL_BLOCK
    nq = seq // TQ

    assert NSA_KC % TQ == 0 and seq % NSA_KC == 0 and TQ % WINDOW == 0 and 2 * DH == LANES

    assert n_sel <= LANES - DH
    block_onehot = np.zeros((seq, LANES), np.float32)
    block_onehot[np.arange(seq), DH + np.arange(seq) // SEL_BLOCK] = 1.0

    c_start = np.arange(n_cmp_pad) * CMP_STRIDE
    s_start = np.arange(n_sel) * SEL_BLOCK
    overlap = ((c_start[:, None] < s_start[None, :] + SEL_BLOCK)
               & (c_start[:, None] + CMP_BLOCK > s_start[None, :])).astype(np.float32)
    overlap[(seq - CMP_BLOCK) // CMP_STRIDE + 1:] = 0.0
    col = np.arange(3 * R * DH)
    gate_lane = SSM_HEADS + 3 * ((col % (R * DH)) // DH) + col // (R * DH)
    gexp = np.stack([(np.arange(LANES)[:, None] == (gate_lane + 3 * R * g)[None, :]) for g in range(G)])

    qblk = pl.BlockSpec((TQ, R * DH), lambda b, g, i: (b * nq + i, P_Q // (R * DH) + g))
    oblk = pl.BlockSpec((TQ, R * DH), lambda b, g, i: (b * nq + i, g))
    cblk = pl.BlockSpec((1, 1, n_cmp_pad, DH), lambda b, g, i: (b, g, 0, 0))
    return pl.pallas_call(
        _nsa_kernel,
        grid=(bsz, G, nq),
        in_specs=[
            qblk, cblk, cblk,
            pl.BlockSpec((seq, LANES), lambda b, g, i: (b, P_KSW // LANES + g)),
            pl.BlockSpec((seq, LANES), lambda b, g, i: (b, P_VSW // LANES + g)),
            pl.BlockSpec((seq, LANES), lambda b, g, i: (0, 0)),
            pl.BlockSpec((TQ, LANES), lambda b, g, i: (b * nq + i, 0)),
            pl.BlockSpec((n_sel, n_cmp_pad), lambda b, g, i: (0, 0)),
            pl.BlockSpec((1, LANES, 3 * R * DH), lambda b, g, i: (g, 0, 0)),
        ],
        out_specs=oblk,
        out_shape=jax.ShapeDtypeStruct((bsz * seq, G * R * DH), BF16),
        compiler_params=_params("parallel", "parallel", "arbitrary"),
        name="nsa_attention",
    )(proj, kc, vc, proj, proj, jnp.asarray(block_onehot, dtype=BF16), small,
      jnp.asarray(overlap.T, dtype=BF16), jnp.asarray(gexp, dtype=BF16))


ROW_TILES = D_MODEL // LANES


def _store_token_major(ref, val):
    n = val.shape[0]
    for c in range(ROW_TILES):
        ref[pl.ds(c, n, stride=ROW_TILES), :] = val[:, c * LANES:(c + 1) * LANES]


def _load_token_major(ref, n):
    return jnp.concatenate([ref[pl.ds(c, n, stride=ROW_TILES), :] for c in range(ROW_TILES)], axis=1)


def _merge_kernel(ys_ref, yn_ref, gs_ref, gn_ref, x_ref, wbs_ref, wbn_ref, wo_ref, fw_ref,
                  rwh_ref, rwl_ref, rb_ref, tri_ref, x1_ref, h_ref, ti_ref, tw_ref, slot_ref, cnt_ref, base_ref):
    @pl.when(pl.program_id(0) == 0)
    def _():
        base_ref[...] = jnp.zeros(base_ref.shape, F32)

    a = jnp.dot(ys_ref[...], wbs_ref[...], preferred_element_type=F32)
    b = jnp.dot(yn_ref[...], wbn_ref[...], preferred_element_type=F32)
    merged = _sigmoid(gs_ref[...].astype(F32)) * a + _sigmoid(gn_ref[...].astype(F32)) * b
    x1 = x_ref[...] + jnp.dot(merged.astype(BF16), wo_ref[...], preferred_element_type=F32)
    x1_ref[...] = x1
    var = jnp.mean(x1 * x1, axis=-1, keepdims=True)
    h = x1 * lax.rsqrt(var + NORM_EPS) * fw_ref[...]
    _store_token_major(h_ref, h)

    nt = (((1,), (1,)), ((), ()))
    h_hi = h.astype(BF16)
    h_lo = (h - h_hi.astype(F32)).astype(BF16)
    logits = (lax.dot_general(rwh_ref[...], h_hi, nt, preferred_element_type=F32)
              + (lax.dot_general(rwl_ref[...], h_hi, nt, preferred_element_type=F32)
                 + lax.dot_general(rwh_ref[...], h_lo, nt, preferred_element_type=F32))) + rb_ref[...]
    ne = logits.shape[0]
    jj = lax.broadcasted_iota(jnp.int32, (ne, 1), 0)
    rank = jnp.zeros(logits.shape, jnp.int32)
    for e in range(ne):
        ce = logits[e:e + 1, :]
        beats = (ce > logits) | ((ce == logits) & (jj > e))
        rank = rank + beats.astype(jnp.int32)
    sel = rank < TOP_K
    mx = jnp.max(logits, axis=0, keepdims=True)
    p = jnp.where(sel, jnp.exp(logits - mx), 0.0)
    p = p / jnp.sum(p, axis=0, keepdims=True)

    self01 = sel.astype(F32)
    before = jnp.dot(self01.astype(BF16), tri_ref[...], preferred_element_type=F32) + base_ref[...]
    for k in range(TOP_K):
        hit = rank == k
        ti_ref[k:k + 1, :] = jnp.sum(jnp.where(hit, jj, 0), axis=0, keepdims=True)
        tw_ref[k:k + 1, :] = jnp.sum(jnp.where(hit, p, 0.0), axis=0, keepdims=True)
        slot_ref[k:k + 1, :] = jnp.sum(jnp.where(hit, before, 0.0), axis=0, keepdims=True).astype(jnp.int32)
    base_ref[...] = base_ref[...] + jnp.sum(self01, axis=1, keepdims=True)
    cnt_ref[...] = base_ref[...]


def _merge_route(y_ssm, y_nsa, proj, x2, wbs, wbn, wo, ffn_w, router_w, router_b, tm):
    t, d = x2.shape
    row = lambda col: pl.BlockSpec((tm, d), lambda i: (i, col))
    const = lambda shape: pl.BlockSpec(shape, lambda i: (0,) * len(shape))
    kblk = pl.BlockSpec((TOP_K, tm), lambda i: (0, i))
    tri = np.triu(np.ones((tm, tm), np.float32), 1)
    rw_t = router_w.T
    rw_hi = rw_t.astype(BF16)
    rw_lo = (rw_t - rw_hi.astype(F32)).astype(BF16)
    return pl.pallas_call(
        _merge_kernel,
        grid=(t // tm,),
        in_specs=[
            row(0), row(0), row(F_MS // d), row(F_MN // d), row(0),
            const((d, d)), const((d, d)), const((d, d)), const((1, d)),
            const((N_EXPERTS, d)), const((N_EXPERTS, d)), const((N_EXPERTS, 1)), const((tm, tm)),
        ],
        out_specs=[row(0), pl.BlockSpec((tm * ROW_TILES, LANES), lambda i: (i, 0)),
                   kblk, kblk, kblk, const((N_EXPERTS, 1))],
        out_shape=[jax.ShapeDtypeStruct((t, d), F32), jax.ShapeDtypeStruct((t * ROW_TILES, LANES), F32),
                   jax.ShapeDtypeStruct((TOP_K, t), jnp.int32), jax.ShapeDtypeStruct((TOP_K, t), F32),
                   jax.ShapeDtypeStruct((TOP_K, t), jnp.int32), jax.ShapeDtypeStruct((N_EXPERTS, 1), F32)],
        scratch_shapes=[pltpu.VMEM((N_EXPERTS, 1), F32)],
        compiler_params=_params("arbitrary"),
        name="merge_route",
    )(y_ssm, y_nsa, proj, proj, x2, wbs.astype(BF16), wbn.astype(BF16), wo.astype(BF16),
      ffn_w.reshape(1, d), rw_hi, rw_lo, router_b.reshape(N_EXPERTS, 1), jnp.asarray(tri, dtype=BF16))


MOE_TM = 256
GU_CHUNK = 2 * LANES


def _moe_kernel(te_ref, nu_ref, tok0_ref, tokn_ref, dst_ref, h_hbm, wgu_ref, bgu_ref, wd_ref, bd_ref, perm_ref,
                y_hbm, xbuf, ybuf, wgu_s, wd_s, gsem, ssem, *, spare_row):
    tm, rt = MOE_TM, ROW_TILES
    i = pl.program_id(0)
    nu = nu_ref[0]
    slot = i % 2
    other = 1 - slot
    new_expert = (i == 0) | (te_ref[i] != te_ref[jnp.maximum(i - 1, 0)])

    @pl.when(new_expert & (i < nu))
    def _():
        wd_s[...] = wd_ref[0].astype(BF16)
        for c in range(wgu_s.shape[1] // GU_CHUNK):
            cols = slice(c * GU_CHUNK, (c + 1) * GU_CHUNK)
            wgu_s[:, cols] = jnp.dot(wgu_ref[0, :, cols].astype(BF16), perm_ref[...],
                                     preferred_element_type=F32).astype(BF16)

    def gather_copy(src_row, r, sl):
        return pltpu.make_async_copy(h_hbm.at[pl.ds(src_row, rt), :], xbuf.at[sl, pl.ds(r * rt, rt), :],
                                     gsem.at[sl])

    def scatter_copy(dst_row, r, sl):
        return pltpu.make_async_copy(ybuf.at[sl, pl.ds(r * rt, rt), :], y_hbm.at[pl.ds(dst_row, rt), :],
                                     ssem.at[sl])

    def start_gather(tok_ref, sl):
        for r in range(tm):
            gather_copy(pl.multiple_of(tok_ref[0, 0, r], rt), r, sl).start()

    def start_scatter(sl):
        for r in range(tm):
            scatter_copy(pl.multiple_of(dst_ref[0, 0, r], rt), r, sl).start()

    def wait_gather(sl):
        pltpu.make_async_copy(h_hbm.at[pl.ds(0, tm * rt), :], xbuf.at[sl], gsem.at[sl]).wait()

    def wait_scatter(sl):
        pltpu.make_async_copy(ybuf.at[sl], y_hbm.at[pl.ds(0, tm * rt), :], ssem.at[sl]).wait()

    @pl.when(i == 0)
    def _():
        ybuf[...] = jnp.zeros(ybuf.shape, ybuf.dtype)
        start_gather(tok0_ref, 0)
        for r in range(tm):
            scatter_copy((spare_row + r) * rt, r, 0).start()

    @pl.when(i < nu)
    def _():
        wait_gather(slot)
        start_gather(tokn_ref, other)
        wait_scatter(slot)
        start_scatter(other)
        x = _load_token_major(xbuf.at[slot], tm).astype(BF16)
        h1 = jnp.dot(x, wgu_s[...], preferred_element_type=F32) + bgu_ref[0]
        nchunk = h1.shape[1] // GU_CHUNK
        glu = jnp.concatenate([h1[:, c * GU_CHUNK:c * GU_CHUNK + LANES] for c in range(nchunk)], axis=1)
        lin = jnp.concatenate([h1[:, c * GU_CHUNK + LANES:(c + 1) * GU_CHUNK] for c in range(nchunk)], axis=1)
        glu = jnp.minimum(glu, SWIGLU_LIMIT)
        lin = jnp.clip(lin, -SWIGLU_LIMIT, SWIGLU_LIMIT)
        act = glu * _sigmoid(SWIGLU_ALPHA * glu) * (lin + 1.0)
        y = jnp.dot(act.astype(BF16), wd_s[...], preferred_element_type=F32) + bd_ref[0]
        _store_token_major(ybuf.at[slot], y)

    @pl.when(i == nu)
    def _():
        wait_gather(slot)
        wait_scatter(slot)
        start_scatter(other)
        wait_scatter(other)


def _moe_experts(h_rows, tok_tiles, dst_tiles, tile_expert, n_used, wgu, bgu, wd, bd, n_out_rows, spare_row):
    d = wd.shape[1]
    tm, rt = MOE_TM, ROW_TILES
    n_tiles = tok_tiles.shape[0]
    perm = np.zeros((GU_CHUNK, GU_CHUNK), np.float32)
    perm[2 * np.arange(LANES), np.arange(LANES)] = 1.0
    perm[2 * np.arange(LANES) + 1, LANES + np.arange(LANES)] = 1.0
    smem = lambda f: pl.BlockSpec((1, 1, tm), f, memory_space=pltpu.SMEM)
    grid_spec = pltpu.PrefetchScalarGridSpec(
        num_scalar_prefetch=2,
        grid=(n_tiles,),
        in_specs=[
            smem(lambda i, te, nu: (0, 0, 0)),
            smem(lambda i, te, nu: (jnp.minimum(i + 1, n_tiles - 1), 0, 0)),
            smem(lambda i, te, nu: (i, 0, 0)),
            pl.BlockSpec(memory_space=pl.ANY),
            pl.BlockSpec((1, d, 2 * d), lambda i, te, nu: (te[i], 0, 0)),
            pl.BlockSpec((1, 1, 2 * d), lambda i, te, nu: (te[i], 0, 0)),
            pl.BlockSpec((1, d, d), lambda i, te, nu: (te[i], 0, 0)),
            pl.BlockSpec((1, 1, d), lambda i, te, nu: (te[i], 0, 0)),
            pl.BlockSpec((GU_CHUNK, GU_CHUNK), lambda i, te, nu: (0, 0)),
        ],
        out_specs=pl.BlockSpec(memory_space=pl.ANY),
        scratch_shapes=[
            pltpu.VMEM((2, tm * rt, LANES), F32),
            pltpu.VMEM((2, tm * rt, LANES), F32),
            pltpu.VMEM((d, 2 * d), BF16),
            pltpu.VMEM((d, d), BF16),
            pltpu.SemaphoreType.DMA((2,)),
            pltpu.SemaphoreType.DMA((2,)),
        ],
    )
    return pl.pallas_call(
        functools.partial(_moe_kernel, spare_row=spare_row),
        grid_spec=grid_spec,
        out_shape=jax.ShapeDtypeStruct((n_out_rows * rt, LANES), F32),
        compiler_params=_params("arbitrary"),
        name="moe_experts",
    )(tile_expert, n_used, tok_tiles, tok_tiles, dst_tiles, h_rows, wgu, bgu, wd, bd, jnp.asarray(perm, dtype=BF16))


def _final_kernel(x1_ref, y0_ref, y1_ref, y2_ref, y3_ref, tw_ref, nw_ref, o_ref):
    tm = x1_ref.shape[0]
    tw = tw_ref[...]
    yk = [_load_token_major(r, tm) for r in (y0_ref, y1_ref, y2_ref, y3_ref)]
    moe = ((tw[:, 0:1] * yk[0] + tw[:, 1:2] * yk[1]) + (tw[:, 2:3] * yk[2] + tw[:, 3:4] * yk[3]))
    x = x1_ref[...] + moe
    var = jnp.mean(x * x, axis=-1, keepdims=True)
    o_ref[...] = x * lax.rsqrt(var + NORM_EPS) * nw_ref[...]


def _final_norm(x1, y_rows, top_w, norm_w, tm):
    t, d = x1.shape
    nt = t // tm
    yblk = lambda k: pl.BlockSpec((tm * ROW_TILES, LANES), lambda i: (k * nt + i, 0))
    return pl.pallas_call(
        _final_kernel,
        grid=(nt,),
        in_specs=[
            pl.BlockSpec((tm, d), lambda i: (i, 0)),
            yblk(0), yblk(1), yblk(2), yblk(3),
            pl.BlockSpec((tm, TOP_K), lambda i: (i, 0)),
            pl.BlockSpec((1, d), lambda i: (0, 0)),
        ],
        out_specs=pl.BlockSpec((tm, d), lambda i: (i, 0)),
        out_shape=jax.ShapeDtypeStruct((t, d), F32),
        compiler_params=_params("parallel"),
        name="final_norm",
    )(x1, y_rows, y_rows, y_rows, y_rows, top_w, norm_w.reshape(1, d))


def _dispatch_tables(top_i, slot, counts, n_rows):
    t = top_i.shape[1]
    n = t * TOP_K
    tm, rt = MOE_TM, ROW_TILES
    padded = ((counts + tm - 1) // tm) * tm
    pend = jnp.cumsum(padded)
    pstart = pend - padded
    onehot = top_i[:, :, None] == jnp.arange(N_EXPERTS, dtype=jnp.int32)
    dest = slot + jnp.sum(jnp.where(onehot, pstart, 0), axis=-1)
    row_pair = jnp.full((n_rows,), -1, jnp.int32).at[dest.reshape(-1)].set(jnp.arange(n, dtype=jnp.int32))
    live = row_pair >= 0
    tok_tiles = (jnp.where(live, row_pair % t, 0) * rt).reshape(n_rows // tm, 1, tm)
    rows = jnp.arange(n_rows, dtype=jnp.int32)
    out_row = jnp.where(live, row_pair, n + tm + rows % tm)
    spare = n + tm + jnp.arange(tm, dtype=jnp.int32)
    dst_tiles = (jnp.concatenate([spare, out_row]) * rt).reshape(n_rows // tm + 1, 1, tm)
    tile_start = jnp.arange(n_rows // tm, dtype=jnp.int32) * tm
    tile_expert = jnp.minimum(jnp.sum(tile_start[:, None] >= pend[None, :], axis=-1), N_EXPERTS - 1)
    n_used = (pend[-1] // tm).reshape(1)
    return tok_tiles, dst_tiles, tile_expert.astype(jnp.int32), n_used.astype(jnp.int32)


def _in_proj_weights(w_in):
    o = np.cumsum([0, D_INNER, XBC_DIM, SSM_HEADS, 1024, 256, 256, 256, 256, 256, 256, 3 * NSA_HEADS, 2 * D_MODEL])
    z, xbc, dt, q, kv, gate, mg = (w_in[:, o[0]:o[1]], w_in[:, o[1]:o[2]], w_in[:, o[2]:o[3]],
                                   w_in[:, o[3]:o[4]], w_in[:, o[4]:o[10]], w_in[:, o[10]:o[11]],
                                   w_in[:, o[11]:o[12]])
    small_pad = jnp.zeros((D_MODEL, LANES - SSM_HEADS - 3 * NSA_HEADS), w_in.dtype)
    kvw = NSA_KV_WIDTH
    k_c, v_c, k_s, v_s, k_w, v_w = [kv[:, n * kvw:(n + 1) * kvw] for n in range(6)]

    def pair_by_group(a, b):
        shape = (D_MODEL, NSA_GROUPS, 1, NSA_HEAD_DIM)
        return jnp.concatenate([a.reshape(shape), b.reshape(shape)], axis=2).reshape(D_MODEL, 2 * kvw)

    w_p = jnp.concatenate([z, mg[:, :D_MODEL], xbc, mg[:, D_MODEL:], q, k_c, v_c,
                           pair_by_group(k_s, k_w), pair_by_group(v_s, v_w)], axis=1)
    w_small = jnp.concatenate([dt, gate, small_pad], axis=1)
    return w_p.astype(BF16), w_small.astype(BF16)


def kernel(x, mix_norm_w, w_in, conv_w, conv_b, dt_bias, a_log, d_skip, ssm_norm_w, cmp_pos_k, cmp_w1_k, cmp_w2_k, cmp_pos_v, cmp_w1_v, cmp_w2_v, w_branch_ssm, w_branch_nsa, w_out, ffn_norm_w, router_w, router_b, w_gate_up, b_gate_up, w_down, b_down, final_norm_w):
    bsz, seq, d = x.shape
    t = bsz * seq
    x2 = x.reshape(t, d)
    depth = w_in.shape[0]
    assert depth == 1, "single-layer block"
    for l in range(depth):
        w_p, w_small = _in_proj_weights(w_in[l])
        proj = _norm_matmul(x2, mix_norm_w[l], w_p, BF16, 2048, 512)
        small = _norm_matmul(x2, mix_norm_w[l], w_small, F32, 2048, LANES)

        dt_t = jnp.transpose(small[:, :SSM_HEADS].reshape(bsz, seq, SSM_HEADS), (0, 2, 1))
        y_ssm = _ssd(proj, small, dt_t, conv_w[l], conv_b[l], dt_bias[l], a_log[l], d_skip[l], ssm_norm_w[l],
                     bsz, seq)

        kc = _compress(proj[:, P_KC:P_KC + NSA_KV_WIDTH], cmp_pos_k[l], cmp_w1_k[l], cmp_w2_k[l], bsz, seq)
        vc = _compress(proj[:, P_VC:P_VC + NSA_KV_WIDTH], cmp_pos_v[l], cmp_w1_v[l], cmp_w2_v[l], bsz, seq)
        y_nsa = _nsa(proj, small, kc, vc, bsz, seq)

        x1, h, top_i, top_w, slot, counts = _merge_route(
            y_ssm, y_nsa, proj, x2, w_branch_ssm[l], w_branch_nsa[l], w_out[l], ffn_norm_w[l],
            router_w[l], router_b[l], 512)

        n_rows = t * TOP_K + N_EXPERTS * MOE_TM
        tok_tiles, dst_tiles, tile_expert, n_used = _dispatch_tables(
            top_i, slot, counts[:, 0].astype(jnp.int32), n_rows)
        nchunk = 2 * d // GU_CHUNK
        bgu = jnp.transpose(b_gate_up[l].reshape(N_EXPERTS, nchunk, LANES, 2), (0, 1, 3, 2))
        y_rows = _moe_experts(h, tok_tiles, dst_tiles, tile_expert, n_used, w_gate_up[l],
                              bgu.reshape(N_EXPERTS, 1, 2 * d), w_down[l], b_down[l][:, None, :],
                              n_out_rows=t * TOP_K + 2 * MOE_TM, spare_row=t * TOP_K)
    out = _final_norm(x1, y_rows, top_w.T, final_norm_w, 512)
    return out.reshape(bsz, seq, d)
```

```python
import functools

import numpy as np
import jax
import jax.numpy as jnp
from jax import lax
from jax.experimental import pallas as pl
from jax.experimental.pallas import tpu as pltpu

F32 = jnp.float32
BF16 = jnp.bfloat16

D_MODEL = 1024
D_INNER = 1024
SSM_HEAD_DIM = 64
SSM_HEADS = 16
SSM_GROUPS = 4
D_STATE = 128
CONV_K = 4
XBC_DIM = D_INNER + 2 * SSM_GROUPS * D_STATE
SSD_CHUNK = 128
NSA_HEAD_DIM = 64
NSA_HEADS = 16
NSA_GROUPS = 4
NSA_REP = 4
NSA_KV_WIDTH = 256
CMP_BLOCK = 32
CMP_STRIDE = 16
CMP_HIDDEN = 128
SEL_BLOCK = 64
SEL_TOPN = 8
WINDOW = 256
N_EXPERTS = 32
TOP_K = 4
SWIGLU_ALPHA = 1.702
SWIGLU_LIMIT = 7.0
NORM_EPS = 1e-5

LANES = 128
VMEM_LIMIT = 56 * 1024 * 1024

F_Z, F_MS, F_XBC, F_MN, P_Q, P_KC, P_VC, P_KSW, P_VSW = 0, 1024, 2048, 4096, 5120, 6144, 6400, 6656, 7168


def _sigmoid(x):
    return 0.5 * jnp.tanh(0.5 * x) + 0.5


def _silu(x):
    return x * _sigmoid(x)


def _softplus(x):
    return jnp.maximum(x, 0.0) + jnp.log1p(jnp.exp(-jnp.abs(x)))


def _split3(x):
    hi = x.astype(BF16)
    r1 = x - hi.astype(F32)
    mid = r1.astype(BF16)
    lo = (r1 - mid.astype(F32)).astype(BF16)
    return hi, mid, lo


def _params(*sem):
    return pltpu.CompilerParams(dimension_semantics=sem, vmem_limit_bytes=VMEM_LIMIT)


def _norm_matmul_kernel(x_ref, nw_ref, w_ref, ws_ref, o_ref, small_ref, h_ref):
    @pl.when(pl.program_id(1) == 0)
    def _():
        x = x_ref[...]
        var = jnp.mean(x * x, axis=-1, keepdims=True)
        h_ref[...] = (x * lax.rsqrt(var + NORM_EPS) * nw_ref[...]).astype(BF16)
        small_ref[...] = jnp.dot(h_ref[...], ws_ref[...], preferred_element_type=F32)

    o_ref[...] = jnp.dot(h_ref[...], w_ref[...], preferred_element_type=F32).astype(o_ref.dtype)


def _norm_matmul(x2, norm_w, w, w_small, tm, tn):
    t, d = x2.shape
    n = w.shape[1]
    ns = w_small.shape[1]
    return pl.pallas_call(
        _norm_matmul_kernel,
        grid=(t // tm, n // tn),
        in_specs=[
            pl.BlockSpec((tm, d), lambda i, j: (i, 0)),
            pl.BlockSpec((1, d), lambda i, j: (0, 0)),
            pl.BlockSpec((d, tn), lambda i, j: (0, j)),
            pl.BlockSpec((d, ns), lambda i, j: (0, 0)),
        ],
        out_specs=[pl.BlockSpec((tm, tn), lambda i, j: (i, j)), pl.BlockSpec((tm, ns), lambda i, j: (i, 0))],
        out_shape=[jax.ShapeDtypeStruct((t, n), BF16), jax.ShapeDtypeStruct((t, ns), F32)],
        scratch_shapes=[pltpu.VMEM((tm, d), BF16)],
        compiler_params=_params("parallel", "arbitrary"),
        name="norm_matmul",
    )(x2, norm_w.reshape(1, d), w, w_small)


def _ssd_kernel(z_ref, xbc_ref, sm_ref, dtt_ref, convw_ref, convb_ref, dtb_ref, dtbt_ref,
                alog_ref, alogt_ref, dskip_ref, nw_ref, expand_ref, y_ref,
                xbuf, state, ydiag):
    L = SSD_CHUNK
    c = pl.program_id(1)

    @pl.when(c == 0)
    def _():
        xbuf[0:8, :] = jnp.zeros((8, XBC_DIM), F32)
        state[...] = jnp.zeros(state.shape, F32)

    xbuf[8:8 + L, :] = xbc_ref[...].astype(F32)
    acc = convb_ref[...] + convw_ref[0:1, :] * xbuf[5:5 + L, :]
    for k in range(1, CONV_K):
        acc = acc + convw_ref[k:k + 1, :] * xbuf[5 + k:5 + k + L, :]
    xbuf[0:8, :] = xbuf[L:L + 8, :]
    xbc = _silu(acc)
    xs = xbc[:, :D_INNER]

    lane = lax.broadcasted_iota(jnp.int32, (1, LANES), 1)
    a_row = jnp.where(lane < SSM_HEADS, -jnp.exp(alog_ref[...]), 0.0)
    dt = _softplus(sm_ref[...] + dtb_ref[...])
    a_dt = dt * a_row
    dt_t = _softplus(dtt_ref[0] + dtbt_ref[...])
    a_dt_t = dt_t * (-jnp.exp(alogt_ref[...]))

    row = lax.broadcasted_iota(jnp.int32, (L, L), 0)
    col = lax.broadcasted_iota(jnp.int32, (L, L), 1)
    lower = row >= col
    tri = lower.astype(BF16)
    tri_t = (row <= col).astype(BF16)
    cs = sum(jnp.dot(tri, part, preferred_element_type=F32) for part in _split3(a_dt))
    cs_t = sum(jnp.dot(part, tri_t, preferred_element_type=F32) for part in _split3(a_dt_t))
    cs_last = cs[L - 1:L, :]

    stacked = jnp.concatenate([dt, jnp.exp(cs_last - cs), jnp.exp(cs)], axis=0)
    wide = sum(jnp.dot(part, expand_ref[...], preferred_element_type=F32) for part in _split3(stacked))
    dt_x = wide[0:L]
    dte_x = wide[L:2 * L]
    ecs_x = wide[2 * L:3 * L]

    xdt = xs * dt_x
    xw = (xdt * dte_x).astype(BF16)
    xdt_b = xdt.astype(BF16)

    hpg = SSM_HEADS // SSM_GROUPS
    gw = hpg * SSM_HEAD_DIM
    y_off_parts = []
    for g in range(SSM_GROUPS):
        b_g = xbc[:, D_INNER + g * D_STATE:D_INNER + (g + 1) * D_STATE].astype(BF16)
        c_g = xbc[:, D_INNER + SSM_GROUPS * D_STATE + g * D_STATE:
                  D_INNER + SSM_GROUPS * D_STATE + (g + 1) * D_STATE].astype(BF16)
        cb = lax.dot_general(c_g, b_g, (((1,), (1,)), ((), ())), preferred_element_type=F32)
        for hh in range(hpg):
            h = g * hpg + hh
            seg = cs[:, h:h + 1] - cs_t[h:h + 1, :]
            decay = jnp.exp(jnp.where(lower, seg, -jnp.inf))
            m = (cb * decay).astype(BF16)
            ydiag[:, h * SSM_HEAD_DIM:(h + 1) * SSM_HEAD_DIM] = jnp.dot(
                m, xdt_b[:, h * SSM_HEAD_DIM:(h + 1) * SSM_HEAD_DIM], preferred_element_type=F32)
        st_prev = state[g]
        y_off_parts.append(jnp.dot(c_g, st_prev.astype(BF16), preferred_element_type=F32))
        st_new = lax.dot_general(b_g, xw[:, g * gw:(g + 1) * gw], (((0,), (0,)), ((), ())),
                                 preferred_element_type=F32)
        state[g] = st_prev * ecs_x[L - 1:L, g * gw:(g + 1) * gw] + st_new
    y_off = jnp.concatenate(y_off_parts, axis=1) * ecs_x

    y = (ydiag[...] + y_off + dskip_ref[...] * xs) * _silu(z_ref[...].astype(F32))
    for g in range(SSM_GROUPS):
        yg = y[:, g * gw:(g + 1) * gw]
        yg = yg * lax.rsqrt(jnp.mean(yg * yg, axis=-1, keepdims=True) + NORM_EPS)
        y_ref[:, g * gw:(g + 1) * gw] = (yg * nw_ref[:, g * gw:(g + 1) * gw]).astype(y_ref.dtype)


def _ssd(proj, small, dt_t, conv_w, conv_b, dt_bias, a_log, d_skip, norm_w, bsz, seq):
    L = SSD_CHUNK
    nc = seq // L
    pad = LANES - SSM_HEADS
    dtb = jnp.pad(dt_bias, (0, pad)).reshape(1, LANES)
    alog = jnp.pad(a_log, (0, pad)).reshape(1, LANES)
    dskip_x = jnp.repeat(d_skip, SSM_HEAD_DIM).reshape(1, D_INNER)
    expand = (np.arange(LANES)[:, None] == (np.arange(D_INNER)[None, :] // SSM_HEAD_DIM)).astype(np.float32)
    const = lambda shape: pl.BlockSpec(shape, lambda b, c: (0,) * len(shape))
    return pl.pallas_call(
        _ssd_kernel,
        grid=(bsz, nc),
        in_specs=[
            pl.BlockSpec((L, D_INNER), lambda b, c: (b * nc + c, F_Z // D_INNER)),
            pl.BlockSpec((L, XBC_DIM), lambda b, c: (b * nc + c, F_XBC // XBC_DIM)),
            pl.BlockSpec((L, LANES), lambda b, c: (b * nc + c, 0)),
            pl.BlockSpec((1, SSM_HEADS, L), lambda b, c: (b, 0, c)),
            const((CONV_K, XBC_DIM)), const((1, XBC_DIM)),
            const((1, LANES)), const((SSM_HEADS, 1)),
            const((1, LANES)), const((SSM_HEADS, 1)),
            const((1, D_INNER)), const((1, D_INNER)),
            const((LANES, D_INNER)),
        ],
        out_specs=pl.BlockSpec((L, D_INNER), lambda b, c: (b * nc + c, 0)),
        out_shape=jax.ShapeDtypeStruct((bsz * seq, D_INNER), BF16),
        scratch_shapes=[
            pltpu.VMEM((L + 8, XBC_DIM), F32),
            pltpu.VMEM((SSM_GROUPS, D_STATE, (SSM_HEADS // SSM_GROUPS) * SSM_HEAD_DIM), F32),
            pltpu.VMEM((L, D_INNER), F32),
        ],
        compiler_params=_params("parallel", "arbitrary"),
        name="ssd",
    )(proj, proj, small, dt_t, conv_w, conv_b.reshape(1, XBC_DIM),
      dtb, dt_bias.reshape(SSM_HEADS, 1), alog, a_log.reshape(SSM_HEADS, 1),
      dskip_x, norm_w.reshape(1, D_INNER), jnp.asarray(expand, dtype=BF16))


def _gelu_tanh(x):
    return 0.5 * x * (1.0 + jnp.tanh(np.sqrt(2.0 / np.pi) * (x + 0.044715 * (x * x * x))))


def _compress_kernel(cur_ref, nxt_ref, pos_ref, w1_ref, w2_ref, o_ref):
    half = CMP_STRIDE * NSA_HEAD_DIM
    lo = (cur_ref[0, 0].astype(F32) + pos_ref[0:1, :]).astype(BF16)
    hi = (nxt_ref[0, 0].astype(F32) + pos_ref[1:2, :]).astype(BF16)
    hid = (jnp.dot(lo, w1_ref[0:half, :], preferred_element_type=F32)
           + jnp.dot(hi, w1_ref[half:2 * half, :], preferred_element_type=F32))
    o_ref[0, 0] = jnp.dot(_gelu_tanh(hid).astype(BF16), w2_ref[...],
                          preferred_element_type=F32).astype(o_ref.dtype)


def _compress(kv, pos, w1, w2, bsz, seq):
    nch = seq // CMP_STRIDE
    half = CMP_STRIDE * NSA_HEAD_DIM
    ch = kv.reshape(bsz, nch, CMP_STRIDE, NSA_GROUPS, NSA_HEAD_DIM)
    ch = jnp.transpose(ch, (0, 3, 1, 2, 4)).reshape(bsz, NSA_GROUPS, nch, half)
    nxt = jnp.concatenate([ch[:, :, 1:], jnp.zeros_like(ch[:, :, :1])], axis=2)
    blk = pl.BlockSpec((1, 1, nch, half), lambda b, g: (b, g, 0, 0))
    return pl.pallas_call(
        _compress_kernel,
        grid=(bsz, NSA_GROUPS),
        in_specs=[
            blk, blk,
            pl.BlockSpec((2, half), lambda b, g: (0, 0)),
            pl.BlockSpec((2 * half, CMP_HIDDEN), lambda b, g: (0, 0)),
            pl.BlockSpec((CMP_HIDDEN, NSA_HEAD_DIM), lambda b, g: (0, 0)),
        ],
        out_specs=pl.BlockSpec((1, 1, nch, NSA_HEAD_DIM), lambda b, g: (b, g, 0, 0)),
        out_shape=jax.ShapeDtypeStruct((bsz, NSA_GROUPS, nch, NSA_HEAD_DIM), BF16),
        compiler_params=_params("parallel", "parallel"),
        name="compress",
    )(ch, nxt, pos.reshape(2, half), w1.astype(BF16), w2.astype(BF16))


NSA_TQ = 512
NSA_KC = 512
NSA_SUB = 256
MASK_VALUE = -1e30
SOFTMAX_M0 = -1e20


def _nsa_kernel(q_ref, kc_ref, vc_ref, k_ref, v_ref, onehot_ref, gate_ref, overlap_ref, gexp_ref, o_ref):
    R, TQ, KC, DH, SUB = NSA_REP, NSA_TQ, NSA_KC, NSA_HEAD_DIM, NSA_SUB
    assert 2 * DH == LANES and TQ == KC and TQ % SUB == 0 and WINDOW % SUB == 0
    i = pl.program_id(2)
    q0 = i * TQ
    nt = (((1,), (1,)), ((), ()))
    tcol = q0 + lax.broadcasted_iota(jnp.int32, (TQ, 1), 0)
    qblk = q_ref[...]
    scale = DH ** -0.5
    qf = [qblk[:, r * DH:(r + 1) * DH].astype(F32) * scale for r in range(R)]
    qh = [q.astype(BF16) for q in qf]
    v_width = LANES
    sel_half = lax.broadcasted_iota(jnp.int32, (1, LANES), 1) < DH
    one = jnp.ones((), BF16)

    def flash_step(m_i, acc, q, k, v_ext, mask_bias=None):
        s = lax.dot_general(q, k, nt, preferred_element_type=F32)
        if mask_bias is not None:
            heads = s.shape[0] // mask_bias.shape[0]
            s = (s.reshape((heads,) + mask_bias.shape) + mask_bias[None]).reshape(s.shape)
        m_new = jnp.maximum(m_i, jnp.max(s, axis=-1, keepdims=True))
        p = jnp.exp((s - m_new).astype(BF16))
        acc = acc * jnp.exp(m_i - m_new) + jnp.dot(p, v_ext, preferred_element_type=F32)
        return m_new, acc

    def spread(x, mat):
        hi = x.astype(BF16)
        lo = (x - hi.astype(F32)).astype(BF16)
        return jnp.dot(hi, mat, preferred_element_type=F32) + jnp.dot(lo, mat, preferred_element_type=F32)

    kc = kc_ref[0, 0]
    vc = vc_ref[0, 0]
    ncmp = kc.shape[0]
    cmp_end = lax.broadcasted_iota(jnp.int32, (1, ncmp), 1) * CMP_STRIDE + (CMP_BLOCK - 1)
    cmask = cmp_end <= tcol
    o_cmp = []
    p4 = None
    for r in range(R):
        lg = jnp.where(cmask, lax.dot_general(qh[r], kc, nt, preferred_element_type=F32), -jnp.inf)
        mx = jnp.max(lg, axis=-1, keepdims=True)
        mx = jnp.where(mx > -jnp.inf, mx, 0.0)
        e = jnp.where(cmask, jnp.exp(lg - mx), 0.0)
        p = e / jnp.maximum(jnp.sum(e, axis=-1, keepdims=True), 1e-30)
        o_cmp.append(jnp.dot(p.astype(BF16), vc, preferred_element_type=F32))
        p4 = p if p4 is None else p4 + p

    imp = sum(lax.dot_general(overlap_ref[...], part, nt, preferred_element_type=F32)
              for part in _split3(p4))
    nsel = imp.shape[0]
    trow = q0 + lax.broadcasted_iota(jnp.int32, (1, TQ), 1)
    cur = trow // SEL_BLOCK
    jj = lax.broadcasted_iota(jnp.int32, (nsel, 1), 0)
    forced = (jj == 0) | (jj == cur) | (jj == cur - 1)
    val = jnp.where(forced, jnp.inf, jnp.where(jj <= cur, imp, -jnp.inf))
    rank = jnp.zeros((nsel, TQ), jnp.int32)
    for b in range(nsel):
        vb = val[b:b + 1, :]
        beats = (vb > val) | ((vb == val) & (jj > b))
        rank = rank + beats.astype(jnp.int32)
    bias = jnp.transpose(jnp.where(rank < SEL_TOPN, 0.0, MASK_VALUE))
    pad = jnp.zeros((TQ, LANES - DH - nsel), F32)
    qa = [jnp.concatenate([qf[r], bias, pad], axis=1).astype(BF16) for r in range(R)]
    qw = [jnp.concatenate([jnp.zeros((TQ, DH), F32), qf[r]], axis=1).astype(BF16) for r in range(R)]

    qa_all = jnp.concatenate(qa, axis=0)

    def sel_kv(k0, n):
        k = jnp.where(sel_half, k_ref[pl.ds(k0, n), :], onehot_ref[pl.ds(k0, n), :])
        v = jnp.where(sel_half, v_ref[pl.ds(k0, n), :], one)
        return k, v

    def sel_step(kb, carry):
        k, v = sel_kv(pl.multiple_of(kb * KC, KC), KC)
        return flash_step(carry[0], carry[1], qa_all, k, v)

    n_full = q0 // KC
    m_all, acc_all = lax.fori_loop(0, n_full, sel_step, (jnp.full((R * TQ, 1), SOFTMAX_M0, F32),
                                                         jnp.zeros((R * TQ, v_width), F32)))
    carry = tuple((m_all[r * TQ:(r + 1) * TQ], acc_all[r * TQ:(r + 1) * TQ]) for r in range(R))

    d0 = pl.multiple_of(q0, TQ)
    stack = lambda parts: jnp.concatenate(parts, axis=0)
    sel_parts, win_parts = [], []
    for j in range(TQ // SUB):
        rows = slice(j * SUB, (j + 1) * SUB)
        tsub = tcol[rows]
        width = (j + 1) * SUB
        bias_d = jnp.where(d0 + lax.broadcasted_iota(jnp.int32, (1, width), 1) <= tsub, 0.0, MASK_VALUE)
        k_d, v_d = sel_kv(d0, width)
        w0 = pl.multiple_of(jnp.maximum(q0 + j * SUB - WINDOW, 0), SUB)
        kpos_w = w0 + lax.broadcasted_iota(jnp.int32, (1, WINDOW + SUB), 1)
        bias_w = jnp.where((kpos_w <= tsub) & (kpos_w > tsub - WINDOW), 0.0, MASK_VALUE)
        k_w = k_ref[pl.ds(w0, WINDOW + SUB), :]
        v_w = jnp.where(sel_half, one, v_ref[pl.ds(w0, WINDOW + SUB), :])
        sel_parts.append(flash_step(stack([carry[r][0][rows] for r in range(R)]),
                                    stack([carry[r][1][rows] for r in range(R)]),
                                    stack([qa[r][rows] for r in range(R)]), k_d, v_d, bias_d)[1])
        win_parts.append(flash_step(jnp.full((R * SUB, 1), SOFTMAX_M0, F32), jnp.zeros((R * SUB, v_width), F32),
                                    stack([qw[r][rows] for r in range(R)]), k_w, v_w, bias_w)[1])
    unstack = lambda parts, r: stack([p[r * SUB:(r + 1) * SUB] for p in parts])
    acc_s = [unstack(sel_parts, r) for r in range(R)]
    acc_w = [unstack(win_parts, r) for r in range(R)]

    gx = spread(_sigmoid(gate_ref[...]), gexp_ref[0])

    def normalised(accs, lo):
        return jnp.concatenate([(acc / jnp.maximum(pltpu.roll(acc, DH, 1), 1e-30))[:, lo:lo + DH] for acc in accs],
                               axis=1)

    w = R * DH
    out = (gx[:, 0:w] * jnp.concatenate(o_cmp, axis=1) + gx[:, w:2 * w] * normalised(acc_s, 0)
           + gx[:, 2 * w:3 * w] * normalised(acc_w, DH))
    o_ref[...] = out.astype(o_ref.dtype)


def _nsa(proj, small, kc, vc, bsz, seq):
    G, R, DH, TQ = NSA_GROUPS, NSA_REP, NSA_HEAD_DIM, NSA_TQ
    n_cmp_pad = seq // CMP_STRIDE
    n_sel = seq // SEL_BLOCK
    nq = seq // TQ

    assert NSA_KC % TQ == 0 and seq % NSA_KC == 0 and TQ % WINDOW == 0 and 2 * DH == LANES

    assert n_sel <= LANES - DH
    block_onehot = np.zeros((seq, LANES), np.float32)
    block_onehot[np.arange(seq), DH + np.arange(seq) // SEL_BLOCK] = 1.0

    c_start = np.arange(n_cmp_pad) * CMP_STRIDE
    s_start = np.arange(n_sel) * SEL_BLOCK
    overlap = ((c_start[:, None] < s_start[None, :] + SEL_BLOCK)
               & (c_start[:, None] + CMP_BLOCK > s_start[None, :])).astype(np.float32)
    overlap[(seq - CMP_BLOCK) // CMP_STRIDE + 1:] = 0.0
    col = np.arange(3 * R * DH)
    gate_lane = SSM_HEADS + 3 * ((col % (R * DH)) // DH) + col // (R * DH)
    gexp = np.stack([(np.arange(LANES)[:, None] == (gate_lane + 3 * R * g)[None, :]) for g in range(G)])

    qblk = pl.BlockSpec((TQ, R * DH), lambda b, g, i: (b * nq + i, P_Q // (R * DH) + g))
    oblk = pl.BlockSpec((TQ, R * DH), lambda b, g, i: (b * nq + i, g))
    cblk = pl.BlockSpec((1, 1, n_cmp_pad, DH), lambda b, g, i: (b, g, 0, 0))
    return pl.pallas_call(
        _nsa_kernel,
        grid=(bsz, G, nq),
        in_specs=[
            qblk, cblk, cblk,
            pl.BlockSpec((seq, LANES), lambda b, g, i: (b, P_KSW // LANES + g)),
            pl.BlockSpec((seq, LANES), lambda b, g, i: (b, P_VSW // LANES + g)),
            pl.BlockSpec((seq, LANES), lambda b, g, i: (0, 0)),
            pl.BlockSpec((TQ, LANES), lambda b, g, i: (b * nq + i, 0)),
            pl.BlockSpec((n_sel, n_cmp_pad), lambda b, g, i: (0, 0)),
            pl.BlockSpec((1, LANES, 3 * R * DH), lambda b, g, i: (g, 0, 0)),
        ],
        out_specs=oblk,
        out_shape=jax.ShapeDtypeStruct((bsz * seq, G * R * DH), BF16),
        compiler_params=_params("parallel", "parallel", "arbitrary"),
        name="nsa_attention",
    )(proj, kc, vc, proj, proj, jnp.asarray(block_onehot, dtype=BF16), small,
      jnp.asarray(overlap.T, dtype=BF16), jnp.asarray(gexp, dtype=BF16))


ROW_TILES = D_MODEL // LANES


def _store_token_major(ref, val):
    n = val.shape[0]
    for c in range(ROW_TILES):
        ref[pl.ds(c, n, stride=ROW_TILES), :] = val[:, c * LANES:(c + 1) * LANES]


def _load_token_major(ref, n):
    return jnp.concatenate([ref[pl.ds(c, n, stride=ROW_TILES), :] for c in range(ROW_TILES)], axis=1)


def _merge_kernel(ys_ref, yn_ref, gs_ref, gn_ref, x_ref, wbs_ref, wbn_ref, wo_ref, fw_ref,
                  rwh_ref, rwl_ref, rb_ref, tri_ref, x1_ref, h_ref, ti_ref, tw_ref, slot_ref, cnt_ref, base_ref):
    @pl.when(pl.program_id(0) == 0)
    def _():
        base_ref[...] = jnp.zeros(base_ref.shape, F32)

    a = jnp.dot(ys_ref[...], wbs_ref[...], preferred_element_type=F32)
    b = jnp.dot(yn_ref[...], wbn_ref[...], preferred_element_type=F32)
    merged = _sigmoid(gs_ref[...].astype(F32)) * a + _sigmoid(gn_ref[...].astype(F32)) * b
    x1 = x_ref[...] + jnp.dot(merged.astype(BF16), wo_ref[...], preferred_element_type=F32)
    x1_ref[...] = x1
    var = jnp.mean(x1 * x1, axis=-1, keepdims=True)
    h = x1 * lax.rsqrt(var + NORM_EPS) * fw_ref[...]
    _store_token_major(h_ref, h)

    nt = (((1,), (1,)), ((), ()))
    h_hi = h.astype(BF16)
    h_lo = (h - h_hi.astype(F32)).astype(BF16)
    logits = (lax.dot_general(rwh_ref[...], h_hi, nt, preferred_element_type=F32)
              + (lax.dot_general(rwl_ref[...], h_hi, nt, preferred_element_type=F32)
                 + lax.dot_general(rwh_ref[...], h_lo, nt, preferred_element_type=F32))) + rb_ref[...]
    ne = logits.shape[0]
    jj = lax.broadcasted_iota(jnp.int32, (ne, 1), 0)
    rank = jnp.zeros(logits.shape, jnp.int32)
    for e in range(ne):
        ce = logits[e:e + 1, :]
        beats = (ce > logits) | ((ce == logits) & (jj > e))
        rank = rank + beats.astype(jnp.int32)
    sel = rank < TOP_K
    mx = jnp.max(logits, axis=0, keepdims=True)
    p = jnp.where(sel, jnp.exp(logits - mx), 0.0)
    p = p / jnp.sum(p, axis=0, keepdims=True)

    self01 = sel.astype(F32)
    before = jnp.dot(self01.astype(BF16), tri_ref[...], preferred_element_type=F32) + base_ref[...]
    for k in range(TOP_K):
        hit = rank == k
        ti_ref[k:k + 1, :] = jnp.sum(jnp.where(hit, jj, 0), axis=0, keepdims=True)
        tw_ref[k:k + 1, :] = jnp.sum(jnp.where(hit, p, 0.0), axis=0, keepdims=True)
        slot_ref[k:k + 1, :] = jnp.sum(jnp.where(hit, before, 0.0), axis=0, keepdims=True).astype(jnp.int32)
    base_ref[...] = base_ref[...] + jnp.sum(self01, axis=1, keepdims=True)
    cnt_ref[...] = base_ref[...]


def _merge_route(y_ssm, y_nsa, proj, x2, wbs, wbn, wo, ffn_w, router_w, router_b, tm):
    t, d = x2.shape
    row = lambda col: pl.BlockSpec((tm, d), lambda i: (i, col))
    const = lambda shape: pl.BlockSpec(shape, lambda i: (0,) * len(shape))
    kblk = pl.BlockSpec((TOP_K, tm), lambda i: (0, i))
    tri = np.triu(np.ones((tm, tm), np.float32), 1)
    rw_t = router_w.T
    rw_hi = rw_t.astype(BF16)
    rw_lo = (rw_t - rw_hi.astype(F32)).astype(BF16)
    return pl.pallas_call(
        _merge_kernel,
        grid=(t // tm,),
        in_specs=[
            row(0), row(0), row(F_MS // d), row(F_MN // d), row(0),
            const((d, d)), const((d, d)), const((d, d)), const((1, d)),
            const((N_EXPERTS, d)), const((N_EXPERTS, d)), const((N_EXPERTS, 1)), const((tm, tm)),
        ],
        out_specs=[row(0), pl.BlockSpec((tm * ROW_TILES, LANES), lambda i: (i, 0)),
                   kblk, kblk, kblk, const((N_EXPERTS, 1))],
        out_shape=[jax.ShapeDtypeStruct((t, d), F32), jax.ShapeDtypeStruct((t * ROW_TILES, LANES), F32),
                   jax.ShapeDtypeStruct((TOP_K, t), jnp.int32), jax.ShapeDtypeStruct((TOP_K, t), F32),
                   jax.ShapeDtypeStruct((TOP_K, t), jnp.int32), jax.ShapeDtypeStruct((N_EXPERTS, 1), F32)],
        scratch_shapes=[pltpu.VMEM((N_EXPERTS, 1), F32)],
        compiler_params=_params("arbitrary"),
        name="merge_route",
    )(y_ssm, y_nsa, proj, proj, x2, wbs.astype(BF16), wbn.astype(BF16), wo.astype(BF16),
      ffn_w.reshape(1, d), rw_hi, rw_lo, router_b.reshape(N_EXPERTS, 1), jnp.asarray(tri, dtype=BF16))


MOE_TM = 256
GU_CHUNK = 2 * LANES


def _moe_kernel(te_ref, nu_ref, tok0_ref, tokn_ref, dst_ref, h_hbm, wgu_ref, bgu_ref, wd_ref, bd_ref, perm_ref,
                y_hbm, xbuf, ybuf, wgu_s, wd_s, gsem, ssem, *, spare_row):
    tm, rt = MOE_TM, ROW_TILES
    i = pl.program_id(0)
    nu = nu_ref[0]
    slot = i % 2
    other = 1 - slot
    new_expert = (i == 0) | (te_ref[i] != te_ref[jnp.maximum(i - 1, 0)])

    @pl.when(new_expert & (i < nu))
    def _():
        wd_s[...] = wd_ref[0].astype(BF16)
        for c in range(wgu_s.shape[1] // GU_CHUNK):
            cols = slice(c * GU_CHUNK, (c + 1) * GU_CHUNK)
            wgu_s[:, cols] = jnp.dot(wgu_ref[0, :, cols].astype(BF16), perm_ref[...],
                                     preferred_element_type=F32).astype(BF16)

    def gather_copy(src_row, r, sl):
        return pltpu.make_async_copy(h_hbm.at[pl.ds(src_row, rt), :], xbuf.at[sl, pl.ds(r * rt, rt), :],
                                     gsem.at[sl])

    def scatter_copy(dst_row, r, sl):
        return pltpu.make_async_copy(ybuf.at[sl, pl.ds(r * rt, rt), :], y_hbm.at[pl.ds(dst_row, rt), :],
                                     ssem.at[sl])

    def start_gather(tok_ref, sl):
        for r in range(tm):
            gather_copy(pl.multiple_of(tok_ref[0, 0, r], rt), r, sl).start()

    def start_scatter(sl):
        for r in range(tm):
            scatter_copy(pl.multiple_of(dst_ref[0, 0, r], rt), r, sl).start(priority=1)

    def wait_gather(sl):
        pltpu.make_async_copy(h_hbm.at[pl.ds(0, tm * rt), :], xbuf.at[sl], gsem.at[sl]).wait()

    def wait_scatter(sl):
        pltpu.make_async_copy(ybuf.at[sl], y_hbm.at[pl.ds(0, tm * rt), :], ssem.at[sl]).wait()

    @pl.when(i == 0)
    def _():
        ybuf[...] = jnp.zeros(ybuf.shape, ybuf.dtype)
        start_gather(tok0_ref, 0)
        for r in range(tm):
            scatter_copy((spare_row + r) * rt, r, 0).start()

    @pl.when(i < nu)
    def _():
        wait_gather(slot)
        start_gather(tokn_ref, other)
        wait_scatter(slot)
        start_scatter(other)
        x = _load_token_major(xbuf.at[slot], tm).astype(BF16)
        h1 = jnp.dot(x, wgu_s[...], preferred_element_type=F32) + bgu_ref[0]
        nchunk = h1.shape[1] // GU_CHUNK
        glu = jnp.concatenate([h1[:, c * GU_CHUNK:c * GU_CHUNK + LANES] for c in range(nchunk)], axis=1)
        lin = jnp.concatenate([h1[:, c * GU_CHUNK + LANES:(c + 1) * GU_CHUNK] for c in range(nchunk)], axis=1)
        glu = jnp.minimum(glu, SWIGLU_LIMIT)
        lin = jnp.clip(lin, -SWIGLU_LIMIT, SWIGLU_LIMIT)
        act = glu * _sigmoid(SWIGLU_ALPHA * glu) * (lin + 1.0)
        y = jnp.dot(act.astype(BF16), wd_s[...], preferred_element_type=F32) + bd_ref[0]
        _store_token_major(ybuf.at[slot], y)

    @pl.when(i == nu)
    def _():
        wait_gather(slot)
        wait_scatter(slot)
        start_scatter(other)
        wait_scatter(other)


def _moe_experts(h_rows, tok_tiles, dst_tiles, tile_expert, n_used, wgu, bgu, wd, bd, n_out_rows, spare_row):
    d = wd.shape[1]
    tm, rt = MOE_TM, ROW_TILES
    n_tiles = tok_tiles.shape[0]
    perm = np.zeros((GU_CHUNK, GU_CHUNK), np.float32)
    perm[2 * np.arange(LANES), np.arange(LANES)] = 1.0
    perm[2 * np.arange(LANES) + 1, LANES + np.arange(LANES)] = 1.0
    smem = lambda f: pl.BlockSpec((1, 1, tm), f, memory_space=pltpu.SMEM)
    grid_spec = pltpu.PrefetchScalarGridSpec(
        num_scalar_prefetch=2,
        grid=(n_tiles,),
        in_specs=[
            smem(lambda i, te, nu: (0, 0, 0)),
            smem(lambda i, te, nu: (jnp.minimum(i + 1, n_tiles - 1), 0, 0)),
            smem(lambda i, te, nu: (i, 0, 0)),
            pl.BlockSpec(memory_space=pl.ANY),
            pl.BlockSpec((1, d, 2 * d), lambda i, te, nu: (te[i], 0, 0)),
            pl.BlockSpec((1, 1, 2 * d), lambda i, te, nu: (te[i], 0, 0)),
            pl.BlockSpec((1, d, d), lambda i, te, nu: (te[i], 0, 0)),
            pl.BlockSpec((1, 1, d), lambda i, te, nu: (te[i], 0, 0)),
            pl.BlockSpec((GU_CHUNK, GU_CHUNK), lambda i, te, nu: (0, 0)),
        ],
        out_specs=pl.BlockSpec(memory_space=pl.ANY),
        scratch_shapes=[
            pltpu.VMEM((2, tm * rt, LANES), F32),
            pltpu.VMEM((2, tm * rt, LANES), F32),
            pltpu.VMEM((d, 2 * d), BF16),
            pltpu.VMEM((d, d), BF16),
            pltpu.SemaphoreType.DMA((2,)),
            pltpu.SemaphoreType.DMA((2,)),
        ],
    )
    return pl.pallas_call(
        functools.partial(_moe_kernel, spare_row=spare_row),
        grid_spec=grid_spec,
        out_shape=jax.ShapeDtypeStruct((n_out_rows * rt, LANES), F32),
        compiler_params=_params("arbitrary"),
        name="moe_experts",
    )(tile_expert, n_used, tok_tiles, tok_tiles, dst_tiles, h_rows, wgu, bgu, wd, bd, jnp.asarray(perm, dtype=BF16))


def _final_kernel(x1_ref, y0_ref, y1_ref, y2_ref, y3_ref, tw_ref, nw_ref, o_ref):
    tm = x1_ref.shape[0]
    tw = tw_ref[...]
    yk = [_load_token_major(r, tm) for r in (y0_ref, y1_ref, y2_ref, y3_ref)]
    moe = ((tw[:, 0:1] * yk[0] + tw[:, 1:2] * yk[1]) + (tw[:, 2:3] * yk[2] + tw[:, 3:4] * yk[3]))
    x = x1_ref[...] + moe
    var = jnp.mean(x * x, axis=-1, keepdims=True)
    o_ref[...] = x * lax.rsqrt(var + NORM_EPS) * nw_ref[...]


def _final_norm(x1, y_rows, top_w, norm_w, tm):
    t, d = x1.shape
    nt = t // tm
    yblk = lambda k: pl.BlockSpec((tm * ROW_TILES, LANES), lambda i: (k * nt + i, 0))
    return pl.pallas_call(
        _final_kernel,
        grid=(nt,),
        in_specs=[
            pl.BlockSpec((tm, d), lambda i: (i, 0)),
            yblk(0), yblk(1), yblk(2), yblk(3),
            pl.BlockSpec((tm, TOP_K), lambda i: (i, 0)),
            pl.BlockSpec((1, d), lambda i: (0, 0)),
        ],
        out_specs=pl.BlockSpec((tm, d), lambda i: (i, 0)),
        out_shape=jax.ShapeDtypeStruct((t, d), F32),
        compiler_params=_params("parallel"),
        name="final_norm",
    )(x1, y_rows, y_rows, y_rows, y_rows, top_w, norm_w.reshape(1, d))


def _dispatch_tables(top_i, slot, counts, n_rows):
    t = top_i.shape[1]
    n = t * TOP_K
    tm, rt = MOE_TM, ROW_TILES
    padded = ((counts + tm - 1) // tm) * tm
    pend = jnp.cumsum(padded)
    pstart = pend - padded
    onehot = top_i[:, :, None] == jnp.arange(N_EXPERTS, dtype=jnp.int32)
    dest = slot + jnp.sum(jnp.where(onehot, pstart, 0), axis=-1)
    row_pair = jnp.full((n_rows,), -1, jnp.int32).at[dest.reshape(-1)].set(jnp.arange(n, dtype=jnp.int32))
    live = row_pair >= 0
    tok_tiles = (jnp.where(live, row_pair % t, 0) * rt).reshape(n_rows // tm, 1, tm)
    rows = jnp.arange(n_rows, dtype=jnp.int32)
    out_row = jnp.where(live, row_pair, n + tm + rows % tm)
    spare = n + tm + jnp.arange(tm, dtype=jnp.int32)
    dst_tiles = (jnp.concatenate([spare, out_row]) * rt).reshape(n_rows // tm + 1, 1, tm)
    tile_start = jnp.arange(n_rows // tm, dtype=jnp.int32) * tm
    tile_expert = jnp.minimum(jnp.sum(tile_start[:, None] >= pend[None, :], axis=-1), N_EXPERTS - 1)
    n_used = (pend[-1] // tm).reshape(1)
    return tok_tiles, dst_tiles, tile_expert.astype(jnp.int32), n_used.astype(jnp.int32)


def _in_proj_weights(w_in):
    o = np.cumsum([0, D_INNER, XBC_DIM, SSM_HEADS, 1024, 256, 256, 256, 256, 256, 256, 3 * NSA_HEADS, 2 * D_MODEL])
    z, xbc, dt, q, kv, gate, mg = (w_in[:, o[0]:o[1]], w_in[:, o[1]:o[2]], w_in[:, o[2]:o[3]],
                                   w_in[:, o[3]:o[4]], w_in[:, o[4]:o[10]], w_in[:, o[10]:o[11]],
                                   w_in[:, o[11]:o[12]])
    small_pad = jnp.zeros((D_MODEL, LANES - SSM_HEADS - 3 * NSA_HEADS), w_in.dtype)
    kvw = NSA_KV_WIDTH
    k_c, v_c, k_s, v_s, k_w, v_w = [kv[:, n * kvw:(n + 1) * kvw] for n in range(6)]

    def pair_by_group(a, b):
        shape = (D_MODEL, NSA_GROUPS, 1, NSA_HEAD_DIM)
        return jnp.concatenate([a.reshape(shape), b.reshape(shape)], axis=2).reshape(D_MODEL, 2 * kvw)

    w_p = jnp.concatenate([z, mg[:, :D_MODEL], xbc, mg[:, D_MODEL:], q, k_c, v_c,
                           pair_by_group(k_s, k_w), pair_by_group(v_s, v_w)], axis=1)
    w_small = jnp.concatenate([dt, gate, small_pad], axis=1)
    return w_p.astype(BF16), w_small.astype(BF16)


def kernel(x, mix_norm_w, w_in, conv_w, conv_b, dt_bias, a_log, d_skip, ssm_norm_w, cmp_pos_k, cmp_w1_k, cmp_w2_k, cmp_pos_v, cmp_w1_v, cmp_w2_v, w_branch_ssm, w_branch_nsa, w_out, ffn_norm_w, router_w, router_b, w_gate_up, b_gate_up, w_down, b_down, final_norm_w):
    bsz, seq, d = x.shape
    t = bsz * seq
    x2 = x.reshape(t, d)
    depth = w_in.shape[0]
    assert depth == 1, "single-layer block"
    for l in range(depth):
        w_p, w_small = _in_proj_weights(w_in[l])
        proj, small = _norm_matmul(x2, mix_norm_w[l], w_p, w_small, 2048, 512)

        dt_t = jnp.transpose(small[:, :SSM_HEADS].reshape(bsz, seq, SSM_HEADS), (0, 2, 1))
        y_ssm = _ssd(proj, small, dt_t, conv_w[l], conv_b[l], dt_bias[l], a_log[l], d_skip[l], ssm_norm_w[l],
                     bsz, seq)

        kc = _compress(proj[:, P_KC:P_KC + NSA_KV_WIDTH], cmp_pos_k[l], cmp_w1_k[l], cmp_w2_k[l], bsz, seq)
        vc = _compress(proj[:, P_VC:P_VC + NSA_KV_WIDTH], cmp_pos_v[l], cmp_w1_v[l], cmp_w2_v[l], bsz, seq)
        y_nsa = _nsa(proj, small, kc, vc, bsz, seq)

        x1, h, top_i, top_w, slot, counts = _merge_route(
            y_ssm, y_nsa, proj, x2, w_branch_ssm[l], w_branch_nsa[l], w_out[l], ffn_norm_w[l],
            router_w[l], router_b[l], 512)

        n_rows = t * TOP_K + N_EXPERTS * MOE_TM
        tok_tiles, dst_tiles, tile_expert, n_used = _dispatch_tables(
            top_i, slot, counts[:, 0].astype(jnp.int32), n_rows)
        nchunk = 2 * d // GU_CHUNK
        bgu = jnp.transpose(b_gate_up[l].reshape(N_EXPERTS, nchunk, LANES, 2), (0, 1, 3, 2))
        y_rows = _moe_experts(h, tok_tiles, dst_tiles, tile_expert, n_used, w_gate_up[l],
                              bgu.reshape(N_EXPERTS, 1, 2 * d), w_down[l], b_down[l][:, None, :],
                              n_out_rows=t * TOP_K + 2 * MOE_TM, spare_row=t * TOP_K)
    out = _final_norm(x1, y_rows, top_w.T, final_norm_w, 512)
    return out.reshape(bsz, seq, d)
```

```python
import functools

import numpy as np
import jax
import jax.numpy as jnp
from jax import lax
from jax.experimental import pallas as pl
from jax.experimental.pallas import tpu as pltpu

F32 = jnp.float32
BF16 = jnp.bfloat16

D_MODEL = 1024
D_INNER = 1024
SSM_HEAD_DIM = 64
SSM_HEADS = 16
SSM_GROUPS = 4
D_STATE = 128
CONV_K = 4
XBC_DIM = D_INNER + 2 * SSM_GROUPS * D_STATE
SSD_CHUNK = 128
NSA_HEAD_DIM = 64
NSA_HEADS = 16
NSA_GROUPS = 4
NSA_REP = 4
NSA_KV_WIDTH = 256
CMP_BLOCK = 32
CMP_STRIDE = 16
CMP_HIDDEN = 128
SEL_BLOCK = 64
SEL_TOPN = 8
WINDOW = 256
N_EXPERTS = 32
TOP_K = 4
SWIGLU_ALPHA = 1.702
SWIGLU_LIMIT = 7.0
NORM_EPS = 1e-5

LANES = 128
VMEM_LIMIT = 56 * 1024 * 1024

F_Z, F_MS, F_XBC, F_MN, P_Q, P_KC, P_VC, P_KSW, P_VSW = 0, 1024, 2048, 4096, 5120, 6144, 6400, 6656, 7168


def _sigmoid(x):
    return 0.5 * jnp.tanh(0.5 * x) + 0.5


def _silu(x):
    return x * _sigmoid(x)


def _softplus(x):
    return jnp.maximum(x, 0.0) + jnp.log1p(jnp.exp(-jnp.abs(x)))


def _split3(x):
    hi = x.astype(BF16)
    r1 = x - hi.astype(F32)
    mid = r1.astype(BF16)
    lo = (r1 - mid.astype(F32)).astype(BF16)
    return hi, mid, lo


def _params(*sem):
    return pltpu.CompilerParams(dimension_semantics=sem, vmem_limit_bytes=VMEM_LIMIT)


def _norm_matmul_kernel(x_ref, nw_ref, w_ref, ws_ref, o_ref, small_ref, h_ref):
    @pl.when(pl.program_id(1) == 0)
    def _():
        x = x_ref[...]
        var = jnp.mean(x * x, axis=-1, keepdims=True)
        h_ref[...] = (x * lax.rsqrt(var + NORM_EPS) * nw_ref[...]).astype(BF16)
        small_ref[...] = jnp.dot(h_ref[...], ws_ref[...], preferred_element_type=F32)

    o_ref[...] = jnp.dot(h_ref[...], w_ref[...], preferred_element_type=F32).astype(o_ref.dtype)


def _norm_matmul(x2, norm_w, w, w_small, tm, tn):
    t, d = x2.shape
    n = w.shape[1]
    ns = w_small.shape[1]
    return pl.pallas_call(
        _norm_matmul_kernel,
        grid=(t // tm, n // tn),
        in_specs=[
            pl.BlockSpec((tm, d), lambda i, j: (i, 0)),
            pl.BlockSpec((1, d), lambda i, j: (0, 0)),
            pl.BlockSpec((d, tn), lambda i, j: (0, j)),
            pl.BlockSpec((d, ns), lambda i, j: (0, 0)),
        ],
        out_specs=[pl.BlockSpec((tm, tn), lambda i, j: (i, j)), pl.BlockSpec((tm, ns), lambda i, j: (i, 0))],
        out_shape=[jax.ShapeDtypeStruct((t, n), BF16), jax.ShapeDtypeStruct((t, ns), F32)],
        scratch_shapes=[pltpu.VMEM((tm, d), BF16)],
        compiler_params=_params("parallel", "arbitrary"),
        name="norm_matmul",
    )(x2, norm_w.reshape(1, d), w, w_small)


def _ssd_kernel(z_ref, xbc_ref, sm_ref, dtt_ref, convw_ref, convb_ref, dtb_ref, dtbt_ref,
                alog_ref, alogt_ref, dskip_ref, nw_ref, expand_ref, y_ref,
                xbuf, state, ydiag):
    L = SSD_CHUNK
    c = pl.program_id(1)

    @pl.when(c == 0)
    def _():
        xbuf[0:8, :] = jnp.zeros((8, XBC_DIM), F32)
        state[...] = jnp.zeros(state.shape, F32)

    xbuf[8:8 + L, :] = xbc_ref[...].astype(F32)
    acc = convb_ref[...] + convw_ref[0:1, :] * xbuf[5:5 + L, :]
    for k in range(1, CONV_K):
        acc = acc + convw_ref[k:k + 1, :] * xbuf[5 + k:5 + k + L, :]
    xbuf[0:8, :] = xbuf[L:L + 8, :]
    xbc = _silu(acc)
    xs = xbc[:, :D_INNER]

    lane = lax.broadcasted_iota(jnp.int32, (1, LANES), 1)
    a_row = jnp.where(lane < SSM_HEADS, -jnp.exp(alog_ref[...]), 0.0)
    dt = _softplus(sm_ref[...] + dtb_ref[...])
    a_dt = dt * a_row
    dt_t = _softplus(dtt_ref[0] + dtbt_ref[...])
    a_dt_t = dt_t * (-jnp.exp(alogt_ref[...]))

    row = lax.broadcasted_iota(jnp.int32, (L, L), 0)
    col = lax.broadcasted_iota(jnp.int32, (L, L), 1)
    lower = row >= col
    tri = lower.astype(BF16)
    tri_t = (row <= col).astype(BF16)
    cs = sum(jnp.dot(tri, part, preferred_element_type=F32) for part in _split3(a_dt))
    cs_t = sum(jnp.dot(part, tri_t, preferred_element_type=F32) for part in _split3(a_dt_t))
    cs_last = cs[L - 1:L, :]

    stacked = jnp.concatenate([dt, jnp.exp(cs_last - cs), jnp.exp(cs)], axis=0)
    wide = sum(jnp.dot(part, expand_ref[...], preferred_element_type=F32) for part in _split3(stacked))
    dt_x = wide[0:L]
    dte_x = wide[L:2 * L]
    ecs_x = wide[2 * L:3 * L]

    xdt = xs * dt_x
    xw = (xdt * dte_x).astype(BF16)
    xdt_b = xdt.astype(BF16)

    hpg = SSM_HEADS // SSM_GROUPS
    gw = hpg * SSM_HEAD_DIM
    y_off_parts = []
    for g in range(SSM_GROUPS):
        b_g = xbc[:, D_INNER + g * D_STATE:D_INNER + (g + 1) * D_STATE].astype(BF16)
        c_g = xbc[:, D_INNER + SSM_GROUPS * D_STATE + g * D_STATE:
                  D_INNER + SSM_GROUPS * D_STATE + (g + 1) * D_STATE].astype(BF16)
        cb = lax.dot_general(c_g, b_g, (((1,), (1,)), ((), ())), preferred_element_type=F32)
        for hh in range(hpg):
            h = g * hpg + hh
            seg = cs[:, h:h + 1] - cs_t[h:h + 1, :]
            decay = jnp.exp(jnp.where(lower, seg, -jnp.inf))
            m = (cb * decay).astype(BF16)
            ydiag[:, h * SSM_HEAD_DIM:(h + 1) * SSM_HEAD_DIM] = jnp.dot(
                m, xdt_b[:, h * SSM_HEAD_DIM:(h + 1) * SSM_HEAD_DIM], preferred_element_type=F32)
        st_prev = state[g]
        y_off_parts.append(jnp.dot(c_g, st_prev.astype(BF16), preferred_element_type=F32))
        st_new = lax.dot_general(b_g, xw[:, g * gw:(g + 1) * gw], (((0,), (0,)), ((), ())),
                                 preferred_element_type=F32)
        state[g] = st_prev * ecs_x[L - 1:L, g * gw:(g + 1) * gw] + st_new
    y_off = jnp.concatenate(y_off_parts, axis=1) * ecs_x

    y = (ydiag[...] + y_off + dskip_ref[...] * xs) * _silu(z_ref[...].astype(F32))
    for g in range(SSM_GROUPS):
        yg = y[:, g * gw:(g + 1) * gw]
        yg = yg * lax.rsqrt(jnp.mean(yg * yg, axis=-1, keepdims=True) + NORM_EPS)
        y_ref[:, g * gw:(g + 1) * gw] = (yg * nw_ref[:, g * gw:(g + 1) * gw]).astype(y_ref.dtype)


def _ssd(proj, small, dt_t, conv_w, conv_b, dt_bias, a_log, d_skip, norm_w, bsz, seq):
    L = SSD_CHUNK
    nc = seq // L
    pad = LANES - SSM_HEADS
    dtb = jnp.pad(dt_bias, (0, pad)).reshape(1, LANES)
    alog = jnp.pad(a_log, (0, pad)).reshape(1, LANES)
    dskip_x = jnp.repeat(d_skip, SSM_HEAD_DIM).reshape(1, D_INNER)
    expand = (np.arange(LANES)[:, None] == (np.arange(D_INNER)[None, :] // SSM_HEAD_DIM)).astype(np.float32)
    const = lambda shape: pl.BlockSpec(shape, lambda b, c: (0,) * len(shape))
    return pl.pallas_call(
        _ssd_kernel,
        grid=(bsz, nc),
        in_specs=[
            pl.BlockSpec((L, D_INNER), lambda b, c: (b * nc + c, F_Z // D_INNER)),
            pl.BlockSpec((L, XBC_DIM), lambda b, c: (b * nc + c, F_XBC // XBC_DIM)),
            pl.BlockSpec((L, LANES), lambda b, c: (b * nc + c, 0)),
            pl.BlockSpec((1, SSM_HEADS, L), lambda b, c: (b, 0, c)),
            const((CONV_K, XBC_DIM)), const((1, XBC_DIM)),
            const((1, LANES)), const((SSM_HEADS, 1)),
            const((1, LANES)), const((SSM_HEADS, 1)),
            const((1, D_INNER)), const((1, D_INNER)),
            const((LANES, D_INNER)),
        ],
        out_specs=pl.BlockSpec((L, D_INNER), lambda b, c: (b * nc + c, 0)),
        out_shape=jax.ShapeDtypeStruct((bsz * seq, D_INNER), BF16),
        scratch_shapes=[
            pltpu.VMEM((L + 8, XBC_DIM), F32),
            pltpu.VMEM((SSM_GROUPS, D_STATE, (SSM_HEADS // SSM_GROUPS) * SSM_HEAD_DIM), F32),
            pltpu.VMEM((L, D_INNER), F32),
        ],
        compiler_params=_params("parallel", "arbitrary"),
        name="ssd",
    )(proj, proj, small, dt_t, conv_w, conv_b.reshape(1, XBC_DIM),
      dtb, dt_bias.reshape(SSM_HEADS, 1), alog, a_log.reshape(SSM_HEADS, 1),
      dskip_x, norm_w.reshape(1, D_INNER), jnp.asarray(expand, dtype=BF16))


def _gelu_tanh(x):
    return 0.5 * x * (1.0 + jnp.tanh(np.sqrt(2.0 / np.pi) * (x + 0.044715 * (x * x * x))))


def _compress_kernel(cur_ref, nxt_ref, pos_ref, w1_ref, w2_ref, o_ref):
    half = CMP_STRIDE * NSA_HEAD_DIM
    lo = (cur_ref[0, 0].astype(F32) + pos_ref[0:1, :]).astype(BF16)
    hi = (nxt_ref[0, 0].astype(F32) + pos_ref[1:2, :]).astype(BF16)
    hid = (jnp.dot(lo, w1_ref[0:half, :], preferred_element_type=F32)
           + jnp.dot(hi, w1_ref[half:2 * half, :], preferred_element_type=F32))
    o_ref[0, 0] = jnp.dot(_gelu_tanh(hid).astype(BF16), w2_ref[...],
                          preferred_element_type=F32).astype(o_ref.dtype)


def _compress(kv, pos, w1, w2, bsz, seq):
    nch = seq // CMP_STRIDE
    half = CMP_STRIDE * NSA_HEAD_DIM
    ch = kv.reshape(bsz, nch, CMP_STRIDE, NSA_GROUPS, NSA_HEAD_DIM)
    ch = jnp.transpose(ch, (0, 3, 1, 2, 4)).reshape(bsz, NSA_GROUPS, nch, half)
    nxt = jnp.concatenate([ch[:, :, 1:], jnp.zeros_like(ch[:, :, :1])], axis=2)
    blk = pl.BlockSpec((1, 1, nch, half), lambda b, g: (b, g, 0, 0))
    return pl.pallas_call(
        _compress_kernel,
        grid=(bsz, NSA_GROUPS),
        in_specs=[
            blk, blk,
            pl.BlockSpec((2, half), lambda b, g: (0, 0)),
            pl.BlockSpec((2 * half, CMP_HIDDEN), lambda b, g: (0, 0)),
            pl.BlockSpec((CMP_HIDDEN, NSA_HEAD_DIM), lambda b, g: (0, 0)),
        ],
        out_specs=pl.BlockSpec((1, 1, nch, NSA_HEAD_DIM), lambda b, g: (b, g, 0, 0)),
        out_shape=jax.ShapeDtypeStruct((bsz, NSA_GROUPS, nch, NSA_HEAD_DIM), BF16),
        compiler_params=_params("parallel", "parallel"),
        name="compress",
    )(ch, nxt, pos.reshape(2, half), w1.astype(BF16), w2.astype(BF16))


NSA_TQ = 512
NSA_KC = 512
NSA_SUB = 256
MASK_VALUE = -1e30
SOFTMAX_M0 = -1e20


def _nsa_kernel(q_ref, kc_ref, vc_ref, k_ref, v_ref, onehot_ref, gate_ref, overlap_ref, gexp_ref, o_ref):
    R, TQ, KC, DH, SUB = NSA_REP, NSA_TQ, NSA_KC, NSA_HEAD_DIM, NSA_SUB
    assert 2 * DH == LANES and TQ == KC and TQ % SUB == 0 and WINDOW % SUB == 0
    i = pl.program_id(2)
    q0 = i * TQ
    nt = (((1,), (1,)), ((), ()))
    tcol = q0 + lax.broadcasted_iota(jnp.int32, (TQ, 1), 0)
    qblk = q_ref[...]
    scale = DH ** -0.5
    qf = [qblk[:, r * DH:(r + 1) * DH].astype(F32) * scale for r in range(R)]
    qh = [q.astype(BF16) for q in qf]
    v_width = LANES
    sel_half = lax.broadcasted_iota(jnp.int32, (1, LANES), 1) < DH
    one = jnp.ones((), BF16)

    def flash_step(m_i, acc, q, k, v_ext, mask_bias=None):
        s = lax.dot_general(q, k, nt, preferred_element_type=F32)
        if mask_bias is not None:
            heads = s.shape[0] // mask_bias.shape[0]
            s = (s.reshape((heads,) + mask_bias.shape) + mask_bias[None]).reshape(s.shape)
        m_new = jnp.maximum(m_i, jnp.max(s, axis=-1, keepdims=True))
        p = jnp.exp((s - m_new).astype(BF16))
        acc = acc * jnp.exp(m_i - m_new) + jnp.dot(p, v_ext, preferred_element_type=F32)
        return m_new, acc

    def spread(x, mat):
        hi = x.astype(BF16)
        lo = (x - hi.astype(F32)).astype(BF16)
        return jnp.dot(hi, mat, preferred_element_type=F32) + jnp.dot(lo, mat, preferred_element_type=F32)

    kc = kc_ref[0, 0]
    vc = vc_ref[0, 0]
    ncmp = kc.shape[0]
    cmp_end = lax.broadcasted_iota(jnp.int32, (1, ncmp), 1) * CMP_STRIDE + (CMP_BLOCK - 1)
    cmask = cmp_end <= tcol
    o_cmp = []
    p4 = None
    for r in range(R):
        lg = jnp.where(cmask, lax.dot_general(qh[r], kc, nt, preferred_element_type=F32), -jnp.inf)
        mx = jnp.max(lg, axis=-1, keepdims=True)
        mx = jnp.where(mx > -jnp.inf, mx, 0.0)
        e = jnp.where(cmask, jnp.exp(lg - mx), 0.0)
        p = e / jnp.maximum(jnp.sum(e, axis=-1, keepdims=True), 1e-30)
        o_cmp.append(jnp.dot(p.astype(BF16), vc, preferred_element_type=F32))
        p4 = p if p4 is None else p4 + p

    imp = sum(lax.dot_general(overlap_ref[...], part, nt, preferred_element_type=F32)
              for part in _split3(p4))
    nsel = imp.shape[0]
    trow = q0 + lax.broadcasted_iota(jnp.int32, (1, TQ), 1)
    cur = trow // SEL_BLOCK
    jj = lax.broadcasted_iota(jnp.int32, (nsel, 1), 0)
    forced = (jj == 0) | (jj == cur) | (jj == cur - 1)
    val = jnp.where(forced, jnp.inf, jnp.where(jj <= cur, imp, -jnp.inf))
    rank = jnp.zeros((nsel, TQ), jnp.int32)
    for b in range(nsel):
        vb = val[b:b + 1, :]
        beats = (vb > val) | ((vb == val) & (jj > b))
        rank = rank + beats.astype(jnp.int32)
    bias = jnp.transpose(jnp.where(rank < SEL_TOPN, 0.0, MASK_VALUE))
    pad = jnp.zeros((TQ, LANES - DH - nsel), F32)
    qa = [jnp.concatenate([qf[r], bias, pad], axis=1).astype(BF16) for r in range(R)]
    qw = [jnp.concatenate([jnp.zeros((TQ, DH), F32), qf[r]], axis=1).astype(BF16) for r in range(R)]

    qa_all = jnp.concatenate(qa, axis=0)

    def sel_kv(k0, n):
        k = jnp.where(sel_half, k_ref[pl.ds(k0, n), :], onehot_ref[pl.ds(k0, n), :])
        v = jnp.where(sel_half, v_ref[pl.ds(k0, n), :], one)
        return k, v

    def sel_step(kb, carry):
        k, v = sel_kv(pl.multiple_of(kb * KC, KC), KC)
        return flash_step(carry[0], carry[1], qa_all, k, v)

    n_full = q0 // KC
    m_all, acc_all = lax.fori_loop(0, n_full, sel_step, (jnp.full((R * TQ, 1), SOFTMAX_M0, F32),
                                                         jnp.zeros((R * TQ, v_width), F32)))
    carry = tuple((m_all[r * TQ:(r + 1) * TQ], acc_all[r * TQ:(r + 1) * TQ]) for r in range(R))

    d0 = pl.multiple_of(q0, TQ)
    stack = lambda parts: jnp.concatenate(parts, axis=0)
    sel_parts, win_parts = [], []
    for j in range(TQ // SUB):
        rows = slice(j * SUB, (j + 1) * SUB)
        tsub = tcol[rows]
        width = (j + 1) * SUB
        bias_d = jnp.where(d0 + lax.broadcasted_iota(jnp.int32, (1, width), 1) <= tsub, 0.0, MASK_VALUE)
        k_d, v_d = sel_kv(d0, width)
        w0 = pl.multiple_of(jnp.maximum(q0 + j * SUB - WINDOW, 0), SUB)
        kpos_w = w0 + lax.broadcasted_iota(jnp.int32, (1, WINDOW + SUB), 1)
        bias_w = jnp.where((kpos_w <= tsub) & (kpos_w > tsub - WINDOW), 0.0, MASK_VALUE)
        k_w = k_ref[pl.ds(w0, WINDOW + SUB), :]
        v_w = jnp.where(sel_half, one, v_ref[pl.ds(w0, WINDOW + SUB), :])
        sel_parts.append(flash_step(stack([carry[r][0][rows] for r in range(R)]),
                                    stack([carry[r][1][rows] for r in range(R)]),
                                    stack([qa[r][rows] for r in range(R)]), k_d, v_d, bias_d)[1])
        win_parts.append(flash_step(jnp.full((R * SUB, 1), SOFTMAX_M0, F32), jnp.zeros((R * SUB, v_width), F32),
                                    stack([qw[r][rows] for r in range(R)]), k_w, v_w, bias_w)[1])
    unstack = lambda parts, r: stack([p[r * SUB:(r + 1) * SUB] for p in parts])
    acc_s = [unstack(sel_parts, r) for r in range(R)]
    acc_w = [unstack(win_parts, r) for r in range(R)]

    gx = spread(_sigmoid(gate_ref[...]), gexp_ref[0])

    def normalised(accs, lo):
        return jnp.concatenate([(acc / jnp.maximum(pltpu.roll(acc, DH, 1), 1e-30))[:, lo:lo + DH] for acc in accs],
                               axis=1)

    w = R * DH
    out = (gx[:, 0:w] * jnp.concatenate(o_cmp, axis=1) + gx[:, w:2 * w] * normalised(acc_s, 0)
           + gx[:, 2 * w:3 * w] * normalised(acc_w, DH))
    o_ref[...] = out.astype(o_ref.dtype)


def _nsa(proj, small, kc, vc, bsz, seq):
    G, R, DH, TQ = NSA_GROUPS, NSA_REP, NSA_HEAD_DIM, NSA_TQ
    n_cmp_pad = seq // CMP_STRIDE
    n_sel = seq // SEL_BLOCK
    nq = seq // TQ

    assert NSA_KC % TQ == 0 and seq % NSA_KC == 0 and TQ % WINDOW == 0 and 2 * DH == LANES

    assert n_sel <= LANES - DH
    block_onehot = np.zeros((seq, LANES), np.float32)
    block_onehot[np.arange(seq), DH + np.arange(seq) // SEL_BLOCK] = 1.0

    c_start = np.arange(n_cmp_pad) * CMP_STRIDE
    s_start = np.arange(n_sel) * SEL_BLOCK
    overlap = ((c_start[:, None] < s_start[None, :] + SEL_BLOCK)
               & (c_start[:, None] + CMP_BLOCK > s_start[None, :])).astype(np.float32)
    overlap[(seq - CMP_BLOCK) // CMP_STRIDE + 1:] = 0.0
    col = np.arange(3 * R * DH)
    gate_lane = SSM_HEADS + 3 * ((col % (R * DH)) // DH) + col // (R * DH)
    gexp = np.stack([(np.arange(LANES)[:, None] == (gate_lane + 3 * R * g)[None, :]) for g in range(G)])

    qblk = pl.BlockSpec((TQ, R * DH), lambda b, g, i: (b * nq + i, P_Q // (R * DH) + g))
    oblk = pl.BlockSpec((TQ, R * DH), lambda b, g, i: (b * nq + i, g))
    cblk = pl.BlockSpec((1, 1, n_cmp_pad, DH), lambda b, g, i: (b, g, 0, 0))
    return pl.pallas_call(
        _nsa_kernel,
        grid=(bsz, G, nq),
        in_specs=[
            qblk, cblk, cblk,
            pl.BlockSpec((seq, LANES), lambda b, g, i: (b, P_KSW // LANES + g)),
            pl.BlockSpec((seq, LANES), lambda b, g, i: (b, P_VSW // LANES + g)),
            pl.BlockSpec((seq, LANES), lambda b, g, i: (0, 0)),
            pl.BlockSpec((TQ, LANES), lambda b, g, i: (b * nq + i, 0)),
            pl.BlockSpec((n_sel, n_cmp_pad), lambda b, g, i: (0, 0)),
            pl.BlockSpec((1, LANES, 3 * R * DH), lambda b, g, i: (g, 0, 0)),
        ],
        out_specs=oblk,
        out_shape=jax.ShapeDtypeStruct((bsz * seq, G * R * DH), BF16),
        compiler_params=_params("parallel", "parallel", "arbitrary"),
        name="nsa_attention",
    )(proj, kc, vc, proj, proj, jnp.asarray(block_onehot, dtype=BF16), small,
      jnp.asarray(overlap.T, dtype=BF16), jnp.asarray(gexp, dtype=BF16))


ROW_TILES = D_MODEL // LANES


def _store_token_major(ref, val):
    n = val.shape[0]
    for c in range(ROW_TILES):
        ref[pl.ds(c, n, stride=ROW_TILES), :] = val[:, c * LANES:(c + 1) * LANES]


def _load_token_major(ref, n):
    return jnp.concatenate([ref[pl.ds(c, n, stride=ROW_TILES), :] for c in range(ROW_TILES)], axis=1)


def _merge_kernel(ys_ref, yn_ref, gs_ref, gn_ref, x_ref, wbs_ref, wbn_ref, wo_ref, fw_ref,
                  rwh_ref, rwl_ref, rb_ref, tri_ref, x1_ref, h_ref, ti_ref, tw_ref, slot_ref, cnt_ref, base_ref):
    @pl.when(pl.program_id(0) == 0)
    def _():
        base_ref[...] = jnp.zeros(base_ref.shape, F32)

    a = jnp.dot(ys_ref[...], wbs_ref[...], preferred_element_type=F32)
    b = jnp.dot(yn_ref[...], wbn_ref[...], preferred_element_type=F32)
    merged = _sigmoid(gs_ref[...].astype(F32)) * a + _sigmoid(gn_ref[...].astype(F32)) * b
    x1 = x_ref[...] + jnp.dot(merged.astype(BF16), wo_ref[...], preferred_element_type=F32)
    x1_ref[...] = x1
    var = jnp.mean(x1 * x1, axis=-1, keepdims=True)
    h = x1 * lax.rsqrt(var + NORM_EPS) * fw_ref[...]
    _store_token_major(h_ref, h)

    nt = (((1,), (1,)), ((), ()))
    h_hi = h.astype(BF16)
    h_lo = (h - h_hi.astype(F32)).astype(BF16)
    logits = (lax.dot_general(rwh_ref[...], h_hi, nt, preferred_element_type=F32)
              + (lax.dot_general(rwl_ref[...], h_hi, nt, preferred_element_type=F32)
                 + lax.dot_general(rwh_ref[...], h_lo, nt, preferred_element_type=F32))) + rb_ref[...]
    ne = logits.shape[0]
    jj = lax.broadcasted_iota(jnp.int32, (ne, 1), 0)
    rank = jnp.zeros(logits.shape, jnp.int32)
    for e in range(ne):
        ce = logits[e:e + 1, :]
        beats = (ce > logits) | ((ce == logits) & (jj > e))
        rank = rank + beats.astype(jnp.int32)
    sel = rank < TOP_K
    mx = jnp.max(logits, axis=0, keepdims=True)
    p = jnp.where(sel, jnp.exp(logits - mx), 0.0)
    p = p / jnp.sum(p, axis=0, keepdims=True)

    self01 = sel.astype(F32)
    before = jnp.dot(self01.astype(BF16), tri_ref[...], preferred_element_type=F32) + base_ref[...]
    for k in range(TOP_K):
        hit = rank == k
        ti_ref[k:k + 1, :] = jnp.sum(jnp.where(hit, jj, 0), axis=0, keepdims=True)
        tw_ref[k:k + 1, :] = jnp.sum(jnp.where(hit, p, 0.0), axis=0, keepdims=True)
        slot_ref[k:k + 1, :] = jnp.sum(jnp.where(hit, before, 0.0), axis=0, keepdims=True).astype(jnp.int32)
    base_ref[...] = base_ref[...] + jnp.sum(self01, axis=1, keepdims=True)
    cnt_ref[...] = base_ref[...]


def _merge_route(y_ssm, y_nsa, proj, x2, wbs, wbn, wo, ffn_w, router_w, router_b, tm):
    t, d = x2.shape
    row = lambda col: pl.BlockSpec((tm, d), lambda i: (i, col))
    const = lambda shape: pl.BlockSpec(shape, lambda i: (0,) * len(shape))
    kblk = pl.BlockSpec((TOP_K, tm), lambda i: (0, i))
    tri = np.triu(np.ones((tm, tm), np.float32), 1)
    rw_t = router_w.T
    rw_hi = rw_t.astype(BF16)
    rw_lo = (rw_t - rw_hi.astype(F32)).astype(BF16)
    return pl.pallas_call(
        _merge_kernel,
        grid=(t // tm,),
        in_specs=[
            row(0), row(0), row(F_MS // d), row(F_MN // d), row(0),
            const((d, d)), const((d, d)), const((d, d)), const((1, d)),
            const((N_EXPERTS, d)), const((N_EXPERTS, d)), const((N_EXPERTS, 1)), const((tm, tm)),
        ],
        out_specs=[row(0), pl.BlockSpec((tm * ROW_TILES, LANES), lambda i: (i, 0)),
                   kblk, kblk, kblk, const((N_EXPERTS, 1))],
        out_shape=[jax.ShapeDtypeStruct((t, d), F32), jax.ShapeDtypeStruct((t * ROW_TILES, LANES), F32),
                   jax.ShapeDtypeStruct((TOP_K, t), jnp.int32), jax.ShapeDtypeStruct((TOP_K, t), F32),
                   jax.ShapeDtypeStruct((TOP_K, t), jnp.int32), jax.ShapeDtypeStruct((N_EXPERTS, 1), F32)],
        scratch_shapes=[pltpu.VMEM((N_EXPERTS, 1), F32)],
        compiler_params=_params("arbitrary"),
        name="merge_route",
    )(y_ssm, y_nsa, proj, proj, x2, wbs.astype(BF16), wbn.astype(BF16), wo.astype(BF16),
      ffn_w.reshape(1, d), rw_hi, rw_lo, router_b.reshape(N_EXPERTS, 1), jnp.asarray(tri, dtype=BF16))


MOE_TM = 256
GU_CHUNK = 2 * LANES


def _moe_kernel(te_ref, nu_ref, tok0_ref, tokn_ref, dst_ref, h_hbm, wgu_ref, bgu_ref, wd_ref, bd_ref, perm_ref,
                y_hbm, xbuf, ybuf, wgu_s, wd_s, gsem, ssem, *, spare_row):
    tm, rt = MOE_TM, ROW_TILES
    i = pl.program_id(0)
    nu = nu_ref[0]
    slot = i % 2
    other = 1 - slot
    new_expert = (i == 0) | (te_ref[i] != te_ref[jnp.maximum(i - 1, 0)])

    @pl.when(new_expert & (i < nu))
    def _():
        wd_s[...] = wd_ref[0].astype(BF16)
        for c in range(wgu_s.shape[1] // GU_CHUNK):
            cols = slice(c * GU_CHUNK, (c + 1) * GU_CHUNK)
            wgu_s[:, cols] = jnp.dot(wgu_ref[0, :, cols].astype(BF16), perm_ref[...],
                                     preferred_element_type=F32).astype(BF16)

    def gather_copy(src_row, r, sl):
        return pltpu.make_async_copy(h_hbm.at[pl.ds(src_row, rt), :], xbuf.at[sl, pl.ds(r * rt, rt), :],
                                     gsem.at[sl])

    def scatter_copy(dst_row, r, sl):
        return pltpu.make_async_copy(ybuf.at[sl, pl.ds(r * rt, rt), :], y_hbm.at[pl.ds(dst_row, rt), :],
                                     ssem.at[sl])

    def start_gather(tok_ref, sl):
        for r in range(tm):
            gather_copy(pl.multiple_of(tok_ref[0, 0, r], rt), r, sl).start()

    def start_scatter(sl):
        for r in range(tm):
            scatter_copy(pl.multiple_of(dst_ref[0, 0, r], rt), r, sl).start(priority=1)

    def wait_gather(sl):
        pltpu.make_async_copy(h_hbm.at[pl.ds(0, tm * rt), :], xbuf.at[sl], gsem.at[sl]).wait()

    def wait_scatter(sl):
        pltpu.make_async_copy(ybuf.at[sl], y_hbm.at[pl.ds(0, tm * rt), :], ssem.at[sl]).wait()

    @pl.when(i == 0)
    def _():
        ybuf[...] = jnp.zeros(ybuf.shape, ybuf.dtype)
        start_gather(tok0_ref, 0)
        for r in range(tm):
            scatter_copy((spare_row + r) * rt, r, 0).start()

    @pl.when(i < nu)
    def _():
        wait_gather(slot)
        wait_scatter(slot)
        x = _load_token_major(xbuf.at[slot], tm).astype(BF16)
        start_gather(tokn_ref, other)
        start_scatter(other)
        h1 = jnp.dot(x, wgu_s[...], preferred_element_type=F32) + bgu_ref[0]
        nchunk = h1.shape[1] // GU_CHUNK
        glu = jnp.concatenate([h1[:, c * GU_CHUNK:c * GU_CHUNK + LANES] for c in range(nchunk)], axis=1)
        lin = jnp.concatenate([h1[:, c * GU_CHUNK + LANES:(c + 1) * GU_CHUNK] for c in range(nchunk)], axis=1)
        glu = jnp.minimum(glu, SWIGLU_LIMIT)
        lin = jnp.clip(lin, -SWIGLU_LIMIT, SWIGLU_LIMIT)
        act = glu * _sigmoid(SWIGLU_ALPHA * glu) * (lin + 1.0)
        y = jnp.dot(act.astype(BF16), wd_s[...], preferred_element_type=F32) + bd_ref[0]
        _store_token_major(ybuf.at[slot], y)

    @pl.when(i == nu)
    def _():
        wait_gather(slot)
        wait_scatter(slot)
        start_scatter(other)
        wait_scatter(other)


def _moe_experts(h_rows, tok_tiles, dst_tiles, tile_expert, n_used, wgu, bgu, wd, bd, n_out_rows, spare_row):
    d = wd.shape[1]
    tm, rt = MOE_TM, ROW_TILES
    n_tiles = tok_tiles.shape[0]
    perm = np.zeros((GU_CHUNK, GU_CHUNK), np.float32)
    perm[2 * np.arange(LANES), np.arange(LANES)] = 1.0
    perm[2 * np.arange(LANES) + 1, LANES + np.arange(LANES)] = 1.0
    smem = lambda f: pl.BlockSpec((1, 1, tm), f, memory_space=pltpu.SMEM)
    grid_spec = pltpu.PrefetchScalarGridSpec(
        num_scalar_prefetch=2,
        grid=(n_tiles,),
        in_specs=[
            smem(lambda i, te, nu: (0, 0, 0)),
            smem(lambda i, te, nu: (jnp.minimum(i + 1, n_tiles - 1), 0, 0)),
            smem(lambda i, te, nu: (i, 0, 0)),
            pl.BlockSpec(memory_space=pl.ANY),
            pl.BlockSpec((1, d, 2 * d), lambda i, te, nu: (te[i], 0, 0)),
            pl.BlockSpec((1, 1, 2 * d), lambda i, te, nu: (te[i], 0, 0)),
            pl.BlockSpec((1, d, d), lambda i, te, nu: (te[i], 0, 0)),
            pl.BlockSpec((1, 1, d), lambda i, te, nu: (te[i], 0, 0)),
            pl.BlockSpec((GU_CHUNK, GU_CHUNK), lambda i, te, nu: (0, 0)),
        ],
        out_specs=pl.BlockSpec(memory_space=pl.ANY),
        scratch_shapes=[
            pltpu.VMEM((2, tm * rt, LANES), F32),
            pltpu.VMEM((2, tm * rt, LANES), F32),
            pltpu.VMEM((d, 2 * d), BF16),
            pltpu.VMEM((d, d), BF16),
            pltpu.SemaphoreType.DMA((2,)),
            pltpu.SemaphoreType.DMA((2,)),
        ],
    )
    return pl.pallas_call(
        functools.partial(_moe_kernel, spare_row=spare_row),
        grid_spec=grid_spec,
        out_shape=jax.ShapeDtypeStruct((n_out_rows * rt, LANES), F32),
        compiler_params=_params("arbitrary"),
        name="moe_experts",
    )(tile_expert, n_used, tok_tiles, tok_tiles, dst_tiles, h_rows, wgu, bgu, wd, bd, jnp.asarray(perm, dtype=BF16))


def _final_kernel(x1_ref, y0_ref, y1_ref, y2_ref, y3_ref, tw_ref, nw_ref, o_ref):
    tm = x1_ref.shape[0]
    tw = tw_ref[...]
    yk = [_load_token_major(r, tm) for r in (y0_ref, y1_ref, y2_ref, y3_ref)]
    moe = ((tw[:, 0:1] * yk[0] + tw[:, 1:2] * yk[1]) + (tw[:, 2:3] * yk[2] + tw[:, 3:4] * yk[3]))
    x = x1_ref[...] + moe
    var = jnp.mean(x * x, axis=-1, keepdims=True)
    o_ref[...] = x * lax.rsqrt(var + NORM_EPS) * nw_ref[...]


def _final_norm(x1, y_rows, top_w, norm_w, tm):
    t, d = x1.shape
    nt = t // tm
    yblk = lambda k: pl.BlockSpec((tm * ROW_TILES, LANES), lambda i: (k * nt + i, 0))
    return pl.pallas_call(
        _final_kernel,
        grid=(nt,),
        in_specs=[
            pl.BlockSpec((tm, d), lambda i: (i, 0)),
            yblk(0), yblk(1), yblk(2), yblk(3),
            pl.BlockSpec((tm, TOP_K), lambda i: (i, 0)),
            pl.BlockSpec((1, d), lambda i: (0, 0)),
        ],
        out_specs=pl.BlockSpec((tm, d), lambda i: (i, 0)),
        out_shape=jax.ShapeDtypeStruct((t, d), F32),
        compiler_params=_params("parallel"),
        name="final_norm",
    )(x1, y_rows, y_rows, y_rows, y_rows, top_w, norm_w.reshape(1, d))


def _dispatch_tables(top_i, slot, counts, n_rows):
    t = top_i.shape[1]
    n = t * TOP_K
    tm, rt = MOE_TM, ROW_TILES
    padded = ((counts + tm - 1) // tm) * tm
    pend = jnp.cumsum(padded)
    pstart = pend - padded
    onehot = top_i[:, :, None] == jnp.arange(N_EXPERTS, dtype=jnp.int32)
    dest = slot + jnp.sum(jnp.where(onehot, pstart, 0), axis=-1)
    row_pair = jnp.full((n_rows,), -1, jnp.int32).at[dest.reshape(-1)].set(jnp.arange(n, dtype=jnp.int32))
    live = row_pair >= 0
    tok_tiles = (jnp.where(live, row_pair % t, 0) * rt).reshape(n_rows // tm, 1, tm)
    rows = jnp.arange(n_rows, dtype=jnp.int32)
    out_row = jnp.where(live, row_pair, n + tm + rows % tm)
    spare = n + tm + jnp.arange(tm, dtype=jnp.int32)
    dst_tiles = (jnp.concatenate([spare, out_row]) * rt).reshape(n_rows // tm + 1, 1, tm)
    tile_start = jnp.arange(n_rows // tm, dtype=jnp.int32) * tm
    tile_expert = jnp.minimum(jnp.sum(tile_start[:, None] >= pend[None, :], axis=-1), N_EXPERTS - 1)
    n_used = (pend[-1] // tm).reshape(1)
    return tok_tiles, dst_tiles, tile_expert.astype(jnp.int32), n_used.astype(jnp.int32)


def _in_proj_weights(w_in):
    o = np.cumsum([0, D_INNER, XBC_DIM, SSM_HEADS, 1024, 256, 256, 256, 256, 256, 256, 3 * NSA_HEADS, 2 * D_MODEL])
    z, xbc, dt, q, kv, gate, mg = (w_in[:, o[0]:o[1]], w_in[:, o[1]:o[2]], w_in[:, o[2]:o[3]],
                                   w_in[:, o[3]:o[4]], w_in[:, o[4]:o[10]], w_in[:, o[10]:o[11]],
                                   w_in[:, o[11]:o[12]])
    small_pad = jnp.zeros((D_MODEL, LANES - SSM_HEADS - 3 * NSA_HEADS), w_in.dtype)
    kvw = NSA_KV_WIDTH
    k_c, v_c, k_s, v_s, k_w, v_w = [kv[:, n * kvw:(n + 1) * kvw] for n in range(6)]

    def pair_by_group(a, b):
        shape = (D_MODEL, NSA_GROUPS, 1, NSA_HEAD_DIM)
        return jnp.concatenate([a.reshape(shape), b.reshape(shape)], axis=2).reshape(D_MODEL, 2 * kvw)

    w_p = jnp.concatenate([z, mg[:, :D_MODEL], xbc, mg[:, D_MODEL:], q, k_c, v_c,
                           pair_by_group(k_s, k_w), pair_by_group(v_s, v_w)], axis=1)
    w_small = jnp.concatenate([dt, gate, small_pad], axis=1)
    return w_p.astype(BF16), w_small.astype(BF16)


def kernel(x, mix_norm_w, w_in, conv_w, conv_b, dt_bias, a_log, d_skip, ssm_norm_w, cmp_pos_k, cmp_w1_k, cmp_w2_k, cmp_pos_v, cmp_w1_v, cmp_w2_v, w_branch_ssm, w_branch_nsa, w_out, ffn_norm_w, router_w, router_b, w_gate_up, b_gate_up, w_down, b_down, final_norm_w):
    bsz, seq, d = x.shape
    t = bsz * seq
    x2 = x.reshape(t, d)
    depth = w_in.shape[0]
    assert depth == 1, "single-layer block"
    for l in range(depth):
        w_p, w_small = _in_proj_weights(w_in[l])
        proj, small = _norm_matmul(x2, mix_norm_w[l], w_p, w_small, 2048, 512)

        dt_t = jnp.transpose(small[:, :SSM_HEADS].reshape(bsz, seq, SSM_HEADS), (0, 2, 1))
        y_ssm = _ssd(proj, small, dt_t, conv_w[l], conv_b[l], dt_bias[l], a_log[l], d_skip[l], ssm_norm_w[l],
                     bsz, seq)

        kc = _compress(proj[:, P_KC:P_KC + NSA_KV_WIDTH], cmp_pos_k[l], cmp_w1_k[l], cmp_w2_k[l], bsz, seq)
        vc = _compress(proj[:, P_VC:P_VC + NSA_KV_WIDTH], cmp_pos_v[l], cmp_w1_v[l], cmp_w2_v[l], bsz, seq)
        y_nsa = _nsa(proj, small, kc, vc, bsz, seq)

        x1, h, top_i, top_w, slot, counts = _merge_route(
            y_ssm, y_nsa, proj, x2, w_branch_ssm[l], w_branch_nsa[l], w_out[l], ffn_norm_w[l],
            router_w[l], router_b[l], 512)

        n_rows = t * TOP_K + N_EXPERTS * MOE_TM
        tok_tiles, dst_tiles, tile_expert, n_used = _dispatch_tables(
            top_i, slot, counts[:, 0].astype(jnp.int32), n_rows)
        nchunk = 2 * d // GU_CHUNK
        bgu = jnp.transpose(b_gate_up[l].reshape(N_EXPERTS, nchunk, LANES, 2), (0, 1, 3, 2))
        y_rows = _moe_experts(h, tok_tiles, dst_tiles, tile_expert, n_used, w_gate_up[l],
                              bgu.reshape(N_EXPERTS, 1, 2 * d), w_down[l], b_down[l][:, None, :],
                              n_out_rows=t * TOP_K + 2 * MOE_TM, spare_row=t * TOP_K)
    out = _final_norm(x1, y_rows, top_w.T, final_norm_w, 512)
    return out.reshape(bsz, seq, d)
```

```python
import functools

import numpy as np
import jax
import jax.numpy as jnp
from jax import lax
from jax.experimental import pallas as pl
from jax.experimental.pallas import tpu as pltpu

F32 = jnp.float32
BF16 = jnp.bfloat16

D_MODEL = 1024
D_INNER = 1024
SSM_HEAD_DIM = 64
SSM_HEADS = 16
SSM_GROUPS = 4
D_STATE = 128
CONV_K = 4
XBC_DIM = D_INNER + 2 * SSM_GROUPS * D_STATE
SSD_CHUNK = 128
NSA_HEAD_DIM = 64
NSA_HEADS = 16
NSA_GROUPS = 4
NSA_REP = 4
NSA_KV_WIDTH = 256
CMP_BLOCK = 32
CMP_STRIDE = 16
CMP_HIDDEN = 128
SEL_BLOCK = 64
SEL_TOPN = 8
WINDOW = 256
N_EXPERTS = 32
TOP_K = 4
SWIGLU_ALPHA = 1.702
SWIGLU_LIMIT = 7.0
NORM_EPS = 1e-5

LANES = 128
VMEM_LIMIT = 56 * 1024 * 1024

F_Z, F_MS, F_XBC, F_MN, P_Q, P_KC, P_VC, P_KSW, P_VSW = 0, 1024, 2048, 4096, 5120, 6144, 6400, 6656, 7168


def _sigmoid(x):
    return 0.5 * jnp.tanh(0.5 * x) + 0.5


def _silu(x):
    return x * _sigmoid(x)


def _softplus(x):
    return jnp.maximum(x, 0.0) + jnp.log1p(jnp.exp(-jnp.abs(x)))


def _split3(x):
    hi = x.astype(BF16)
    r1 = x - hi.astype(F32)
    mid = r1.astype(BF16)
    lo = (r1 - mid.astype(F32)).astype(BF16)
    return hi, mid, lo


def _params(*sem):
    return pltpu.CompilerParams(dimension_semantics=sem, vmem_limit_bytes=VMEM_LIMIT)


def _norm_matmul_kernel(x_ref, nw_ref, w_ref, ws_ref, o_ref, small_ref, h_ref):
    @pl.when(pl.program_id(1) == 0)
    def _():
        x = x_ref[...]
        var = jnp.mean(x * x, axis=-1, keepdims=True)
        h_ref[...] = (x * lax.rsqrt(var + NORM_EPS) * nw_ref[...]).astype(BF16)
        small_ref[...] = jnp.dot(h_ref[...], ws_ref[...], preferred_element_type=F32)

    o_ref[...] = jnp.dot(h_ref[...], w_ref[...], preferred_element_type=F32).astype(o_ref.dtype)


def _norm_matmul(x2, norm_w, w, w_small, tm, tn):
    t, d = x2.shape
    n = w.shape[1]
    ns = w_small.shape[1]
    return pl.pallas_call(
        _norm_matmul_kernel,
        grid=(t // tm, n // tn),
        in_specs=[
            pl.BlockSpec((tm, d), lambda i, j: (i, 0)),
            pl.BlockSpec((1, d), lambda i, j: (0, 0)),
            pl.BlockSpec((d, tn), lambda i, j: (0, j)),
            pl.BlockSpec((d, ns), lambda i, j: (0, 0)),
        ],
        out_specs=[pl.BlockSpec((tm, tn), lambda i, j: (i, j)), pl.BlockSpec((tm, ns), lambda i, j: (i, 0))],
        out_shape=[jax.ShapeDtypeStruct((t, n), BF16), jax.ShapeDtypeStruct((t, ns), F32)],
        scratch_shapes=[pltpu.VMEM((tm, d), BF16)],
        compiler_params=_params("parallel", "arbitrary"),
        name="norm_matmul",
    )(x2, norm_w.reshape(1, d), w, w_small)


def _ssd_kernel(z_ref, xbc_ref, sm_ref, dtt_ref, convw_ref, convb_ref, dtb_ref, dtbt_ref,
                alog_ref, alogt_ref, dskip_ref, nw_ref, expand_ref, y_ref,
                xbuf, state, ydiag):
    L = SSD_CHUNK
    c = pl.program_id(1)

    @pl.when(c == 0)
    def _():
        xbuf[0:8, :] = jnp.zeros((8, XBC_DIM), F32)
        state[...] = jnp.zeros(state.shape, F32)

    xbuf[8:8 + L, :] = xbc_ref[...].astype(F32)
    acc = convb_ref[...] + convw_ref[0:1, :] * xbuf[5:5 + L, :]
    for k in range(1, CONV_K):
        acc = acc + convw_ref[k:k + 1, :] * xbuf[5 + k:5 + k + L, :]
    xbuf[0:8, :] = xbuf[L:L + 8, :]
    xbc = _silu(acc)
    xs = xbc[:, :D_INNER]

    lane = lax.broadcasted_iota(jnp.int32, (1, LANES), 1)
    a_row = jnp.where(lane < SSM_HEADS, -jnp.exp(alog_ref[...]), 0.0)
    dt = _softplus(sm_ref[...] + dtb_ref[...])
    a_dt = dt * a_row
    dt_t = _softplus(dtt_ref[0] + dtbt_ref[...])
    a_dt_t = dt_t * (-jnp.exp(alogt_ref[...]))

    row = lax.broadcasted_iota(jnp.int32, (L, L), 0)
    col = lax.broadcasted_iota(jnp.int32, (L, L), 1)
    lower = row >= col
    tri = lower.astype(BF16)
    tri_t = (row <= col).astype(BF16)
    cs = sum(jnp.dot(tri, part, preferred_element_type=F32) for part in _split3(a_dt))
    cs_t = sum(jnp.dot(part, tri_t, preferred_element_type=F32) for part in _split3(a_dt_t))
    cs_last = cs[L - 1:L, :]

    stacked = jnp.concatenate([dt, jnp.exp(cs_last - cs), jnp.exp(cs)], axis=0)
    wide = sum(jnp.dot(part, expand_ref[...], preferred_element_type=F32) for part in _split3(stacked))
    dt_x = wide[0:L]
    dte_x = wide[L:2 * L]
    ecs_x = wide[2 * L:3 * L]

    xdt = xs * dt_x
    xw = (xdt * dte_x).astype(BF16)
    xdt_b = xdt.astype(BF16)

    hpg = SSM_HEADS // SSM_GROUPS
    gw = hpg * SSM_HEAD_DIM
    y_off_parts = []
    for g in range(SSM_GROUPS):
        b_g = xbc[:, D_INNER + g * D_STATE:D_INNER + (g + 1) * D_STATE].astype(BF16)
        c_g = xbc[:, D_INNER + SSM_GROUPS * D_STATE + g * D_STATE:
                  D_INNER + SSM_GROUPS * D_STATE + (g + 1) * D_STATE].astype(BF16)
        cb = lax.dot_general(c_g, b_g, (((1,), (1,)), ((), ())), preferred_element_type=F32)
        for hh in range(hpg):
            h = g * hpg + hh
            seg = cs[:, h:h + 1] - cs_t[h:h + 1, :]
            decay = jnp.exp(jnp.where(lower, seg, -jnp.inf))
            m = (cb * decay).astype(BF16)
            ydiag[:, h * SSM_HEAD_DIM:(h + 1) * SSM_HEAD_DIM] = jnp.dot(
                m, xdt_b[:, h * SSM_HEAD_DIM:(h + 1) * SSM_HEAD_DIM], preferred_element_type=F32)
        st_prev = state[g]
        y_off_parts.append(jnp.dot(c_g, st_prev.astype(BF16), preferred_element_type=F32))
        st_new = lax.dot_general(b_g, xw[:, g * gw:(g + 1) * gw], (((0,), (0,)), ((), ())),
                                 preferred_element_type=F32)
        state[g] = st_prev * ecs_x[L - 1:L, g * gw:(g + 1) * gw] + st_new
    y_off = jnp.concatenate(y_off_parts, axis=1) * ecs_x

    y = (ydiag[...] + y_off + dskip_ref[...] * xs) * _silu(z_ref[...].astype(F32))
    for g in range(SSM_GROUPS):
        yg = y[:, g * gw:(g + 1) * gw]
        yg = yg * lax.rsqrt(jnp.mean(yg * yg, axis=-1, keepdims=True) + NORM_EPS)
        y_ref[:, g * gw:(g + 1) * gw] = (yg * nw_ref[:, g * gw:(g + 1) * gw]).astype(y_ref.dtype)


def _ssd(proj, small, dt_t, conv_w, conv_b, dt_bias, a_log, d_skip, norm_w, bsz, seq):
    L = SSD_CHUNK
    nc = seq // L
    pad = LANES - SSM_HEADS
    dtb = jnp.pad(dt_bias, (0, pad)).reshape(1, LANES)
    alog = jnp.pad(a_log, (0, pad)).reshape(1, LANES)
    dskip_x = jnp.repeat(d_skip, SSM_HEAD_DIM).reshape(1, D_INNER)
    expand = (np.arange(LANES)[:, None] == (np.arange(D_INNER)[None, :] // SSM_HEAD_DIM)).astype(np.float32)
    const = lambda shape: pl.BlockSpec(shape, lambda b, c: (0,) * len(shape))
    return pl.pallas_call(
        _ssd_kernel,
        grid=(bsz, nc),
        in_specs=[
            pl.BlockSpec((L, D_INNER), lambda b, c: (b * nc + c, F_Z // D_INNER)),
            pl.BlockSpec((L, XBC_DIM), lambda b, c: (b * nc + c, F_XBC // XBC_DIM)),
            pl.BlockSpec((L, LANES), lambda b, c: (b * nc + c, 0)),
            pl.BlockSpec((1, SSM_HEADS, L), lambda b, c: (b, 0, c)),
            const((CONV_K, XBC_DIM)), const((1, XBC_DIM)),
            const((1, LANES)), const((SSM_HEADS, 1)),
            const((1, LANES)), const((SSM_HEADS, 1)),
            const((1, D_INNER)), const((1, D_INNER)),
            const((LANES, D_INNER)),
        ],
        out_specs=pl.BlockSpec((L, D_INNER), lambda b, c: (b * nc + c, 0)),
        out_shape=jax.ShapeDtypeStruct((bsz * seq, D_INNER), BF16),
        scratch_shapes=[
            pltpu.VMEM((L + 8, XBC_DIM), F32),
            pltpu.VMEM((SSM_GROUPS, D_STATE, (SSM_HEADS // SSM_GROUPS) * SSM_HEAD_DIM), F32),
            pltpu.VMEM((L, D_INNER), F32),
        ],
        compiler_params=_params("parallel", "arbitrary"),
        name="ssd",
    )(proj, proj, small, dt_t, conv_w, conv_b.reshape(1, XBC_DIM),
      dtb, dt_bias.reshape(SSM_HEADS, 1), alog, a_log.reshape(SSM_HEADS, 1),
      dskip_x, norm_w.reshape(1, D_INNER), jnp.asarray(expand, dtype=BF16))


def _gelu_tanh(x):
    return 0.5 * x * (1.0 + jnp.tanh(np.sqrt(2.0 / np.pi) * (x + 0.044715 * (x * x * x))))


def _compress_kernel(cur_ref, nxt_ref, pos_ref, w1_ref, w2_ref, o_ref):
    half = CMP_STRIDE * NSA_HEAD_DIM
    lo = (cur_ref[0, 0].astype(F32) + pos_ref[0:1, :]).astype(BF16)
    hi = (nxt_ref[0, 0].astype(F32) + pos_ref[1:2, :]).astype(BF16)
    hid = (jnp.dot(lo, w1_ref[0:half, :], preferred_element_type=F32)
           + jnp.dot(hi, w1_ref[half:2 * half, :], preferred_element_type=F32))
    o_ref[0, 0] = jnp.dot(_gelu_tanh(hid).astype(BF16), w2_ref[...],
                          preferred_element_type=F32).astype(o_ref.dtype)


def _compress(kv, pos, w1, w2, bsz, seq):
    nch = seq // CMP_STRIDE
    half = CMP_STRIDE * NSA_HEAD_DIM
    ch = kv.reshape(bsz, nch, CMP_STRIDE, NSA_GROUPS, NSA_HEAD_DIM)
    ch = jnp.transpose(ch, (0, 3, 1, 2, 4)).reshape(bsz, NSA_GROUPS, nch, half)
    nxt = jnp.concatenate([ch[:, :, 1:], jnp.zeros_like(ch[:, :, :1])], axis=2)
    blk = pl.BlockSpec((1, 1, nch, half), lambda b, g: (b, g, 0, 0))
    return pl.pallas_call(
        _compress_kernel,
        grid=(bsz, NSA_GROUPS),
        in_specs=[
            blk, blk,
            pl.BlockSpec((2, half), lambda b, g: (0, 0)),
            pl.BlockSpec((2 * half, CMP_HIDDEN), lambda b, g: (0, 0)),
            pl.BlockSpec((CMP_HIDDEN, NSA_HEAD_DIM), lambda b, g: (0, 0)),
        ],
        out_specs=pl.BlockSpec((1, 1, nch, NSA_HEAD_DIM), lambda b, g: (b, g, 0, 0)),
        out_shape=jax.ShapeDtypeStruct((bsz, NSA_GROUPS, nch, NSA_HEAD_DIM), BF16),
        compiler_params=_params("parallel", "parallel"),
        name="compress",
    )(ch, nxt, pos.reshape(2, half), w1.astype(BF16), w2.astype(BF16))


NSA_TQ = 512
NSA_KC = 512
NSA_SUB = 256
MASK_VALUE = -1e30
SOFTMAX_M0 = -1e20


def _nsa_kernel(q_ref, kc_ref, vc_ref, k_ref, v_ref, onehot_ref, gate_ref, overlap_ref, gexp_ref, o_ref):
    R, TQ, KC, DH, SUB = NSA_REP, NSA_TQ, NSA_KC, NSA_HEAD_DIM, NSA_SUB
    assert 2 * DH == LANES and TQ == KC and TQ % SUB == 0 and WINDOW % SUB == 0
    i = pl.program_id(2)
    q0 = i * TQ
    nt = (((1,), (1,)), ((), ()))
    tcol = q0 + lax.broadcasted_iota(jnp.int32, (TQ, 1), 0)
    qblk = q_ref[...]
    scale = DH ** -0.5
    qf = [qblk[:, r * DH:(r + 1) * DH].astype(F32) * scale for r in range(R)]
    qh = [q.astype(BF16) for q in qf]
    v_width = LANES
    sel_half = lax.broadcasted_iota(jnp.int32, (1, LANES), 1) < DH
    one = jnp.ones((), BF16)

    def flash_step(m_i, acc, q, k, v_ext, mask_bias=None):
        s = lax.dot_general(q, k, nt, preferred_element_type=F32)
        if mask_bias is not None:
            heads = s.shape[0] // mask_bias.shape[0]
            s = (s.reshape((heads,) + mask_bias.shape) + mask_bias[None]).reshape(s.shape)
        m_new = jnp.maximum(m_i, jnp.max(s, axis=-1, keepdims=True))
        p = jnp.exp((s - m_new).astype(BF16))
        acc = acc * jnp.exp(m_i - m_new) + jnp.dot(p, v_ext, preferred_element_type=F32)
        return m_new, acc

    def spread(x, mat):
        hi = x.astype(BF16)
        lo = (x - hi.astype(F32)).astype(BF16)
        return jnp.dot(hi, mat, preferred_element_type=F32) + jnp.dot(lo, mat, preferred_element_type=F32)

    kc = kc_ref[0, 0]
    vc = vc_ref[0, 0]
    ncmp = kc.shape[0]
    cmp_end = lax.broadcasted_iota(jnp.int32, (1, ncmp), 1) * CMP_STRIDE + (CMP_BLOCK - 1)
    cmask = cmp_end <= tcol
    o_cmp = []
    p4 = None
    for r in range(R):
        lg = jnp.where(cmask, lax.dot_general(qh[r], kc, nt, preferred_element_type=F32), -jnp.inf)
        mx = jnp.max(lg, axis=-1, keepdims=True)
        mx = jnp.where(mx > -jnp.inf, mx, 0.0)
        e = jnp.where(cmask, jnp.exp(lg - mx), 0.0)
        p = e / jnp.maximum(jnp.sum(e, axis=-1, keepdims=True), 1e-30)
        o_cmp.append(jnp.dot(p.astype(BF16), vc, preferred_element_type=F32))
        p4 = p if p4 is None else p4 + p

    imp = sum(lax.dot_general(overlap_ref[...], part, nt, preferred_element_type=F32)
              for part in _split3(p4))
    nsel = imp.shape[0]
    trow = q0 + lax.broadcasted_iota(jnp.int32, (1, TQ), 1)
    cur = trow // SEL_BLOCK
    jj = lax.broadcasted_iota(jnp.int32, (nsel, 1), 0)
    forced = (jj == 0) | (jj == cur) | (jj == cur - 1)
    val = jnp.where(forced, jnp.inf, jnp.where(jj <= cur, imp, -jnp.inf))
    rank = jnp.zeros((nsel, TQ), jnp.int32)
    for b in range(nsel):
        vb = val[b:b + 1, :]
        beats = (vb > val) | ((vb == val) & (jj > b))
        rank = rank + beats.astype(jnp.int32)
    bias = jnp.transpose(jnp.where(rank < SEL_TOPN, 0.0, MASK_VALUE))
    pad = jnp.zeros((TQ, LANES - DH - nsel), F32)
    qa = [jnp.concatenate([qf[r], bias, pad], axis=1).astype(BF16) for r in range(R)]
    qw = [jnp.concatenate([jnp.zeros((TQ, DH), F32), qf[r]], axis=1).astype(BF16) for r in range(R)]

    qa_all = jnp.concatenate(qa, axis=0)

    def sel_kv(k0, n):
        k = jnp.where(sel_half, k_ref[pl.ds(k0, n), :], onehot_ref[pl.ds(k0, n), :])
        v = jnp.where(sel_half, v_ref[pl.ds(k0, n), :], one)
        return k, v

    def sel_step(kb, carry):
        k, v = sel_kv(pl.multiple_of(kb * KC, KC), KC)
        return flash_step(carry[0], carry[1], qa_all, k, v)

    n_full = q0 // KC
    m_all, acc_all = lax.fori_loop(0, n_full, sel_step, (jnp.full((R * TQ, 1), SOFTMAX_M0, F32),
                                                         jnp.zeros((R * TQ, v_width), F32)))
    carry = tuple((m_all[r * TQ:(r + 1) * TQ], acc_all[r * TQ:(r + 1) * TQ]) for r in range(R))

    d0 = pl.multiple_of(q0, TQ)
    stack = lambda parts: jnp.concatenate(parts, axis=0)
    sel_parts, win_parts = [], []
    for j in range(TQ // SUB):
        rows = slice(j * SUB, (j + 1) * SUB)
        tsub = tcol[rows]
        width = (j + 1) * SUB
        bias_d = jnp.where(d0 + lax.broadcasted_iota(jnp.int32, (1, width), 1) <= tsub, 0.0, MASK_VALUE)
        k_d, v_d = sel_kv(d0, width)
        w0 = pl.multiple_of(jnp.maximum(q0 + j * SUB - WINDOW, 0), SUB)
        kpos_w = w0 + lax.broadcasted_iota(jnp.int32, (1, WINDOW + SUB), 1)
        bias_w = jnp.where((kpos_w <= tsub) & (kpos_w > tsub - WINDOW), 0.0, MASK_VALUE)
        k_w = k_ref[pl.ds(w0, WINDOW + SUB), :]
        v_w = jnp.where(sel_half, one, v_ref[pl.ds(w0, WINDOW + SUB), :])
        sel_parts.append(flash_step(stack([carry[r][0][rows] for r in range(R)]),
                                    stack([carry[r][1][rows] for r in range(R)]),
                                    stack([qa[r][rows] for r in range(R)]), k_d, v_d, bias_d)[1])
        win_parts.append(flash_step(jnp.full((R * SUB, 1), SOFTMAX_M0, F32), jnp.zeros((R * SUB, v_width), F32),
                                    stack([qw[r][rows] for r in range(R)]), k_w, v_w, bias_w)[1])
    unstack = lambda parts, r: stack([p[r * SUB:(r + 1) * SUB] for p in parts])
    acc_s = [unstack(sel_parts, r) for r in range(R)]
    acc_w = [unstack(win_parts, r) for r in range(R)]

    gx = spread(_sigmoid(gate_ref[...]), gexp_ref[0])

    def normalised(accs, lo):
        return jnp.concatenate([(acc / jnp.maximum(pltpu.roll(acc, DH, 1), 1e-30))[:, lo:lo + DH] for acc in accs],
                               axis=1)

    w = R * DH
    out = (gx[:, 0:w] * jnp.concatenate(o_cmp, axis=1) + gx[:, w:2 * w] * normalised(acc_s, 0)
           + gx[:, 2 * w:3 * w] * normalised(acc_w, DH))
    o_ref[...] = out.astype(o_ref.dtype)


def _nsa(proj, small, kc, vc, bsz, seq):
    G, R, DH, TQ = NSA_GROUPS, NSA_REP, NSA_HEAD_DIM, NSA_TQ
    n_cmp_pad = seq // CMP_STRIDE
    n_sel = seq // SEL_BLOCK
    nq = seq // TQ

    assert NSA_KC % TQ == 0 and seq % NSA_KC == 0 and TQ % WINDOW == 0 and 2 * DH == LANES

    assert n_sel <= LANES - DH
    block_onehot = np.zeros((seq, LANES), np.float32)
    block_onehot[np.arange(seq), DH + np.arange(seq) // SEL_BLOCK] = 1.0

    c_start = np.arange(n_cmp_pad) * CMP_STRIDE
    s_start = np.arange(n_sel) * SEL_BLOCK
    overlap = ((c_start[:, None] < s_start[None, :] + SEL_BLOCK)
               & (c_start[:, None] + CMP_BLOCK > s_start[None, :])).astype(np.float32)
    overlap[(seq - CMP_BLOCK) // CMP_STRIDE + 1:] = 0.0
    col = np.arange(3 * R * DH)
    gate_lane = SSM_HEADS + 3 * ((col % (R * DH)) // DH) + col // (R * DH)
    gexp = np.stack([(np.arange(LANES)[:, None] == (gate_lane + 3 * R * g)[None, :]) for g in range(G)])

    qblk = pl.BlockSpec((TQ, R * DH), lambda b, g, i: (b * nq + i, P_Q // (R * DH) + g))
    oblk = pl.BlockSpec((TQ, R * DH), lambda b, g, i: (b * nq + i, g))
    cblk = pl.BlockSpec((1, 1, n_cmp_pad, DH), lambda b, g, i: (b, g, 0, 0))
    return pl.pallas_call(
        _nsa_kernel,
        grid=(bsz, G, nq),
        in_specs=[
            qblk, cblk, cblk,
            pl.BlockSpec((seq, LANES), lambda b, g, i: (b, P_KSW // LANES + g)),
            pl.BlockSpec((seq, LANES), lambda b, g, i: (b, P_VSW // LANES + g)),
            pl.BlockSpec((seq, LANES), lambda b, g, i: (0, 0)),
            pl.BlockSpec((TQ, LANES), lambda b, g, i: (b * nq + i, 0)),
            pl.BlockSpec((n_sel, n_cmp_pad), lambda b, g, i: (0, 0)),
            pl.BlockSpec((1, LANES, 3 * R * DH), lambda b, g, i: (g, 0, 0)),
        ],
        out_specs=oblk,
        out_shape=jax.ShapeDtypeStruct((bsz * seq, G * R * DH), BF16),
        compiler_params=_params("parallel", "parallel", "arbitrary"),
        name="nsa_attention",
    )(proj, kc, vc, proj, proj, jnp.asarray(block_onehot, dtype=BF16), small,
      jnp.asarray(overlap.T, dtype=BF16), jnp.asarray(gexp, dtype=BF16))


ROW_TILES = D_MODEL // LANES


def _store_token_major(ref, val):
    n = val.shape[0]
    for c in range(ROW_TILES):
        ref[pl.ds(c, n, stride=ROW_TILES), :] = val[:, c * LANES:(c + 1) * LANES]


def _load_token_major(ref, n):
    return jnp.concatenate([ref[pl.ds(c, n, stride=ROW_TILES), :] for c in range(ROW_TILES)], axis=1)


def _merge_kernel(ys_ref, yn_ref, gs_ref, gn_ref, x_ref, wbs_ref, wbn_ref, wo_ref, fw_ref,
                  rwh_ref, rwl_ref, rb_ref, tri_ref, x1_ref, h_ref, ti_ref, tw_ref, slot_ref, cnt_ref, base_ref):
    @pl.when(pl.program_id(0) == 0)
    def _():
        base_ref[...] = jnp.zeros(base_ref.shape, F32)

    a = jnp.dot(ys_ref[...], wbs_ref[...], preferred_element_type=F32)
    b = jnp.dot(yn_ref[...], wbn_ref[...], preferred_element_type=F32)
    merged = _sigmoid(gs_ref[...].astype(F32)) * a + _sigmoid(gn_ref[...].astype(F32)) * b
    x1 = x_ref[...] + jnp.dot(merged.astype(BF16), wo_ref[...], preferred_element_type=F32)
    x1_ref[...] = x1
    var = jnp.mean(x1 * x1, axis=-1, keepdims=True)
    h = x1 * lax.rsqrt(var + NORM_EPS) * fw_ref[...]
    _store_token_major(h_ref, h)

    nt = (((1,), (1,)), ((), ()))
    h_hi = h.astype(BF16)
    h_lo = (h - h_hi.astype(F32)).astype(BF16)
    logits = (lax.dot_general(rwh_ref[...], h_hi, nt, preferred_element_type=F32)
              + (lax.dot_general(rwl_ref[...], h_hi, nt, preferred_element_type=F32)
                 + lax.dot_general(rwh_ref[...], h_lo, nt, preferred_element_type=F32))) + rb_ref[...]
    ne = logits.shape[0]
    jj = lax.broadcasted_iota(jnp.int32, (ne, 1), 0)
    rank = jnp.zeros(logits.shape, jnp.int32)
    for e in range(ne):
        ce = logits[e:e + 1, :]
        beats = (ce > logits) | ((ce == logits) & (jj > e))
        rank = rank + beats.astype(jnp.int32)
    sel = rank < TOP_K
    mx = jnp.max(logits, axis=0, keepdims=True)
    p = jnp.where(sel, jnp.exp(logits - mx), 0.0)
    p = p / jnp.sum(p, axis=0, keepdims=True)

    self01 = sel.astype(F32)
    before = jnp.dot(self01.astype(BF16), tri_ref[...], preferred_element_type=F32) + base_ref[...]
    for k in range(TOP_K):
        hit = rank == k
        ti_ref[k:k + 1, :] = jnp.sum(jnp.where(hit, jj, 0), axis=0, keepdims=True)
        tw_ref[k:k + 1, :] = jnp.sum(jnp.where(hit, p, 0.0), axis=0, keepdims=True)
        slot_ref[k:k + 1, :] = jnp.sum(jnp.where(hit, before, 0.0), axis=0, keepdims=True).astype(jnp.int32)
    base_ref[...] = base_ref[...] + jnp.sum(self01, axis=1, keepdims=True)
    cnt_ref[...] = base_ref[...]


def _merge_route(y_ssm, y_nsa, proj, x2, wbs, wbn, wo, ffn_w, router_w, router_b, tm):
    t, d = x2.shape
    row = lambda col: pl.BlockSpec((tm, d), lambda i: (i, col))
    const = lambda shape: pl.BlockSpec(shape, lambda i: (0,) * len(shape))
    kblk = pl.BlockSpec((TOP_K, tm), lambda i: (0, i))
    tri = np.triu(np.ones((tm, tm), np.float32), 1)
    rw_t = router_w.T
    rw_hi = rw_t.astype(BF16)
    rw_lo = (rw_t - rw_hi.astype(F32)).astype(BF16)
    return pl.pallas_call(
        _merge_kernel,
        grid=(t // tm,),
        in_specs=[
            row(0), row(0), row(F_MS // d), row(F_MN // d), row(0),
            const((d, d)), const((d, d)), const((d, d)), const((1, d)),
            const((N_EXPERTS, d)), const((N_EXPERTS, d)), const((N_EXPERTS, 1)), const((tm, tm)),
        ],
        out_specs=[row(0), pl.BlockSpec((tm * ROW_TILES, LANES), lambda i: (i, 0)),
                   kblk, kblk, kblk, const((N_EXPERTS, 1))],
        out_shape=[jax.ShapeDtypeStruct((t, d), F32), jax.ShapeDtypeStruct((t * ROW_TILES, LANES), F32),
                   jax.ShapeDtypeStruct((TOP_K, t), jnp.int32), jax.ShapeDtypeStruct((TOP_K, t), F32),
                   jax.ShapeDtypeStruct((TOP_K, t), jnp.int32), jax.ShapeDtypeStruct((N_EXPERTS, 1), F32)],
        scratch_shapes=[pltpu.VMEM((N_EXPERTS, 1), F32)],
        compiler_params=_params("arbitrary"),
        name="merge_route",
    )(y_ssm, y_nsa, proj, proj, x2, wbs.astype(BF16), wbn.astype(BF16), wo.astype(BF16),
      ffn_w.reshape(1, d), rw_hi, rw_lo, router_b.reshape(N_EXPERTS, 1), jnp.asarray(tri, dtype=BF16))


MOE_TM = 256
MOE_RING = 3
GU_CHUNK = 2 * LANES


def _moe_kernel(te_ref, nu_ref, tok0_ref, tok1_ref, tokn_ref, dst_ref, h_hbm, wgu_ref, bgu_ref, wd_ref, bd_ref,
                perm_ref, y_hbm, xbuf, ybuf, wgu_s, wd_s, gsem, ssem, *, spare_row):
    tm, rt = MOE_TM, ROW_TILES
    i = pl.program_id(0)
    nu = nu_ref[0]
    slot = i % MOE_RING
    prev = (i + MOE_RING - 1) % MOE_RING
    new_expert = (i == 0) | (te_ref[i] != te_ref[jnp.maximum(i - 1, 0)])

    @pl.when(new_expert & (i < nu))
    def _():
        wd_s[...] = wd_ref[0].astype(BF16)
        for c in range(wgu_s.shape[1] // GU_CHUNK):
            cols = slice(c * GU_CHUNK, (c + 1) * GU_CHUNK)
            wgu_s[:, cols] = jnp.dot(wgu_ref[0, :, cols].astype(BF16), perm_ref[...],
                                     preferred_element_type=F32).astype(BF16)

    def gather_copy(src_row, r, sl):
        return pltpu.make_async_copy(h_hbm.at[pl.ds(src_row, rt), :], xbuf.at[sl, pl.ds(r * rt, rt), :],
                                     gsem.at[sl])

    def scatter_copy(dst_row, r, sl):
        return pltpu.make_async_copy(ybuf.at[sl, pl.ds(r * rt, rt), :], y_hbm.at[pl.ds(dst_row, rt), :],
                                     ssem.at[sl])

    def start_gather(tok_ref, sl):
        for r in range(tm):
            gather_copy(pl.multiple_of(tok_ref[0, 0, r], rt), r, sl).start()

    def start_scatter(sl):
        for r in range(tm):
            scatter_copy(pl.multiple_of(dst_ref[0, 0, r], rt), r, sl).start()

    def wait_gather(sl):
        pltpu.make_async_copy(h_hbm.at[pl.ds(0, tm * rt), :], xbuf.at[sl], gsem.at[sl]).wait()

    def wait_scatter(sl):
        pltpu.make_async_copy(ybuf.at[sl], y_hbm.at[pl.ds(0, tm * rt), :], ssem.at[sl]).wait()

    @pl.when(i == 0)
    def _():
        ybuf[...] = jnp.zeros(ybuf.shape, ybuf.dtype)
        start_gather(tok0_ref, 0)
        start_gather(tok1_ref, 1)
        for sl in range(2):
            for r in range(tm):
                scatter_copy((spare_row + sl * tm + r) * rt, r, sl).start()

    @pl.when(i < nu)
    def _():
        wait_gather(slot)
        wait_scatter(slot)
        x = _load_token_major(xbuf.at[slot], tm).astype(BF16)
        start_gather(tokn_ref, prev)
        start_scatter(prev)
        h1 = jnp.dot(x, wgu_s[...], preferred_element_type=F32) + bgu_ref[0]
        nchunk = h1.shape[1] // GU_CHUNK
        glu = jnp.concatenate([h1[:, c * GU_CHUNK:c * GU_CHUNK + LANES] for c in range(nchunk)], axis=1)
        lin = jnp.concatenate([h1[:, c * GU_CHUNK + LANES:(c + 1) * GU_CHUNK] for c in range(nchunk)], axis=1)
        glu = jnp.minimum(glu, SWIGLU_LIMIT)
        lin = jnp.clip(lin, -SWIGLU_LIMIT, SWIGLU_LIMIT)
        act = glu * _sigmoid(SWIGLU_ALPHA * glu) * (lin + 1.0)
        y = jnp.dot(act.astype(BF16), wd_s[...], preferred_element_type=F32) + bd_ref[0]
        _store_token_major(ybuf.at[slot], y)

    @pl.when(i == nu)
    def _():
        nxt = (i + 1) % MOE_RING
        wait_gather(slot)
        wait_gather(nxt)
        wait_scatter(slot)
        wait_scatter(nxt)
        start_scatter(prev)
        wait_scatter(prev)


def _moe_experts(h_rows, tok_tiles, dst_tiles, tile_expert, n_used, wgu, bgu, wd, bd, n_out_rows, spare_row):
    d = wd.shape[1]
    tm, rt = MOE_TM, ROW_TILES
    n_tiles = tok_tiles.shape[0]
    perm = np.zeros((GU_CHUNK, GU_CHUNK), np.float32)
    perm[2 * np.arange(LANES), np.arange(LANES)] = 1.0
    perm[2 * np.arange(LANES) + 1, LANES + np.arange(LANES)] = 1.0
    smem = lambda f: pl.BlockSpec((1, 1, tm), f, memory_space=pltpu.SMEM)
    grid_spec = pltpu.PrefetchScalarGridSpec(
        num_scalar_prefetch=2,
        grid=(n_tiles,),
        in_specs=[
            smem(lambda i, te, nu: (0, 0, 0)),
            smem(lambda i, te, nu: (1, 0, 0)),
            smem(lambda i, te, nu: (jnp.minimum(i + MOE_RING - 1, n_tiles - 1), 0, 0)),
            smem(lambda i, te, nu: (i, 0, 0)),
            pl.BlockSpec(memory_space=pl.ANY),
            pl.BlockSpec((1, d, 2 * d), lambda i, te, nu: (te[i], 0, 0)),
            pl.BlockSpec((1, 1, 2 * d), lambda i, te, nu: (te[i], 0, 0)),
            pl.BlockSpec((1, d, d), lambda i, te, nu: (te[i], 0, 0)),
            pl.BlockSpec((1, 1, d), lambda i, te, nu: (te[i], 0, 0)),
            pl.BlockSpec((GU_CHUNK, GU_CHUNK), lambda i, te, nu: (0, 0)),
        ],
        out_specs=pl.BlockSpec(memory_space=pl.ANY),
        scratch_shapes=[
            pltpu.VMEM((MOE_RING, tm * rt, LANES), F32),
            pltpu.VMEM((MOE_RING, tm * rt, LANES), F32),
            pltpu.VMEM((d, 2 * d), BF16),
            pltpu.VMEM((d, d), BF16),
            pltpu.SemaphoreType.DMA((MOE_RING,)),
            pltpu.SemaphoreType.DMA((MOE_RING,)),
        ],
    )
    return pl.pallas_call(
        functools.partial(_moe_kernel, spare_row=spare_row),
        grid_spec=grid_spec,
        out_shape=jax.ShapeDtypeStruct((n_out_rows * rt, LANES), F32),
        compiler_params=_params("arbitrary"),
        name="moe_experts",
    )(tile_expert, n_used, tok_tiles, tok_tiles, tok_tiles, dst_tiles, h_rows, wgu, bgu, wd, bd,
      jnp.asarray(perm, dtype=BF16))


def _final_kernel(x1_ref, y0_ref, y1_ref, y2_ref, y3_ref, tw_ref, nw_ref, o_ref):
    tm = x1_ref.shape[0]
    tw = tw_ref[...]
    yk = [_load_token_major(r, tm) for r in (y0_ref, y1_ref, y2_ref, y3_ref)]
    moe = ((tw[:, 0:1] * yk[0] + tw[:, 1:2] * yk[1]) + (tw[:, 2:3] * yk[2] + tw[:, 3:4] * yk[3]))
    x = x1_ref[...] + moe
    var = jnp.mean(x * x, axis=-1, keepdims=True)
    o_ref[...] = x * lax.rsqrt(var + NORM_EPS) * nw_ref[...]


def _final_norm(x1, y_rows, top_w, norm_w, tm):
    t, d = x1.shape
    nt = t // tm
    yblk = lambda k: pl.BlockSpec((tm * ROW_TILES, LANES), lambda i: (k * nt + i, 0))
    return pl.pallas_call(
        _final_kernel,
        grid=(nt,),
        in_specs=[
            pl.BlockSpec((tm, d), lambda i: (i, 0)),
            yblk(0), yblk(1), yblk(2), yblk(3),
            pl.BlockSpec((tm, TOP_K), lambda i: (i, 0)),
            pl.BlockSpec((1, d), lambda i: (0, 0)),
        ],
        out_specs=pl.BlockSpec((tm, d), lambda i: (i, 0)),
        out_shape=jax.ShapeDtypeStruct((t, d), F32),
        compiler_params=_params("parallel"),
        name="final_norm",
    )(x1, y_rows, y_rows, y_rows, y_rows, top_w, norm_w.reshape(1, d))


def _dispatch_tables(top_i, slot, counts, n_rows):
    t = top_i.shape[1]
    n = t * TOP_K
    tm, rt = MOE_TM, ROW_TILES
    padded = ((counts + tm - 1) // tm) * tm
    pend = jnp.cumsum(padded)
    pstart = pend - padded
    onehot = top_i[:, :, None] == jnp.arange(N_EXPERTS, dtype=jnp.int32)
    dest = slot + jnp.sum(jnp.where(onehot, pstart, 0), axis=-1)
    row_pair = jnp.full((n_rows,), -1, jnp.int32).at[dest.reshape(-1)].set(jnp.arange(n, dtype=jnp.int32))
    live = row_pair >= 0
    tok_tiles = (jnp.where(live, row_pair % t, 0) * rt).reshape(n_rows // tm, 1, tm)
    rows = jnp.arange(n_rows, dtype=jnp.int32)
    out_row = jnp.where(live, row_pair, n + ((rows // tm) % 2) * tm + rows % tm)
    spare = n + 2 * tm + jnp.arange(tm, dtype=jnp.int32)
    dst_tiles = (jnp.concatenate([spare, out_row]) * rt).reshape(n_rows // tm + 1, 1, tm)
    tile_start = jnp.arange(n_rows // tm, dtype=jnp.int32) * tm
    tile_expert = jnp.minimum(jnp.sum(tile_start[:, None] >= pend[None, :], axis=-1), N_EXPERTS - 1)
    n_used = (pend[-1] // tm).reshape(1)
    return tok_tiles, dst_tiles, tile_expert.astype(jnp.int32), n_used.astype(jnp.int32)


def _in_proj_weights(w_in):
    o = np.cumsum([0, D_INNER, XBC_DIM, SSM_HEADS, 1024, 256, 256, 256, 256, 256, 256, 3 * NSA_HEADS, 2 * D_MODEL])
    z, xbc, dt, q, kv, gate, mg = (w_in[:, o[0]:o[1]], w_in[:, o[1]:o[2]], w_in[:, o[2]:o[3]],
                                   w_in[:, o[3]:o[4]], w_in[:, o[4]:o[10]], w_in[:, o[10]:o[11]],
                                   w_in[:, o[11]:o[12]])
    small_pad = jnp.zeros((D_MODEL, LANES - SSM_HEADS - 3 * NSA_HEADS), w_in.dtype)
    kvw = NSA_KV_WIDTH
    k_c, v_c, k_s, v_s, k_w, v_w = [kv[:, n * kvw:(n + 1) * kvw] for n in range(6)]

    def pair_by_group(a, b):
        shape = (D_MODEL, NSA_GROUPS, 1, NSA_HEAD_DIM)
        return jnp.concatenate([a.reshape(shape), b.reshape(shape)], axis=2).reshape(D_MODEL, 2 * kvw)

    w_p = jnp.concatenate([z, mg[:, :D_MODEL], xbc, mg[:, D_MODEL:], q, k_c, v_c,
                           pair_by_group(k_s, k_w), pair_by_group(v_s, v_w)], axis=1)
    w_small = jnp.concatenate([dt, gate, small_pad], axis=1)
    return w_p.astype(BF16), w_small.astype(BF16)


def kernel(x, mix_norm_w, w_in, conv_w, conv_b, dt_bias, a_log, d_skip, ssm_norm_w, cmp_pos_k, cmp_w1_k, cmp_w2_k, cmp_pos_v, cmp_w1_v, cmp_w2_v, w_branch_ssm, w_branch_nsa, w_out, ffn_norm_w, router_w, router_b, w_gate_up, b_gate_up, w_down, b_down, final_norm_w):
    bsz, seq, d = x.shape
    t = bsz * seq
    x2 = x.reshape(t, d)
    depth = w_in.shape[0]
    assert depth == 1, "single-layer block"
    for l in range(depth):
        w_p, w_small = _in_proj_weights(w_in[l])
        proj, small = _norm_matmul(x2, mix_norm_w[l], w_p, w_small, 2048, 512)

        dt_t = jnp.transpose(small[:, :SSM_HEADS].reshape(bsz, seq, SSM_HEADS), (0, 2, 1))
        y_ssm = _ssd(proj, small, dt_t, conv_w[l], conv_b[l], dt_bias[l], a_log[l], d_skip[l], ssm_norm_w[l],
                     bsz, seq)

        kc = _compress(proj[:, P_KC:P_KC + NSA_KV_WIDTH], cmp_pos_k[l], cmp_w1_k[l], cmp_w2_k[l], bsz, seq)
        vc = _compress(proj[:, P_VC:P_VC + NSA_KV_WIDTH], cmp_pos_v[l], cmp_w1_v[l], cmp_w2_v[l], bsz, seq)
        y_nsa = _nsa(proj, small, kc, vc, bsz, seq)

        x1, h, top_i, top_w, slot, counts = _merge_route(
            y_ssm, y_nsa, proj, x2, w_branch_ssm[l], w_branch_nsa[l], w_out[l], ffn_norm_w[l],
            router_w[l], router_b[l], 512)

        n_rows = t * TOP_K + N_EXPERTS * MOE_TM
        tok_tiles, dst_tiles, tile_expert, n_used = _dispatch_tables(
            top_i, slot, counts[:, 0].astype(jnp.int32), n_rows)
        nchunk = 2 * d // GU_CHUNK
        bgu = jnp.transpose(b_gate_up[l].reshape(N_EXPERTS, nchunk, LANES, 2), (0, 1, 3, 2))
        y_rows = _moe_experts(h, tok_tiles, dst_tiles, tile_expert, n_used, w_gate_up[l],
                              bgu.reshape(N_EXPERTS, 1, 2 * d), w_down[l], b_down[l][:, None, :],
                              n_out_rows=t * TOP_K + 3 * MOE_TM, spare_row=t * TOP_K)
    out = _final_norm(x1, y_rows, top_w.T, final_norm_w, MOE_TM)
    return out.reshape(bsz, seq, d)
```

```python
import functools

import numpy as np
import jax
import jax.numpy as jnp
from jax import lax
from jax.experimental import pallas as pl
from jax.experimental.pallas import tpu as pltpu

F32 = jnp.float32
BF16 = jnp.bfloat16

D_MODEL = 1024
D_INNER = 1024
SSM_HEAD_DIM = 64
SSM_HEADS = 16
SSM_GROUPS = 4
D_STATE = 128
CONV_K = 4
XBC_DIM = D_INNER + 2 * SSM_GROUPS * D_STATE
SSD_CHUNK = 128
NSA_HEAD_DIM = 64
NSA_HEADS = 16
NSA_GROUPS = 4
NSA_REP = 4
NSA_KV_WIDTH = 256
CMP_BLOCK = 32
CMP_STRIDE = 16
CMP_HIDDEN = 128
SEL_BLOCK = 64
SEL_TOPN = 8
WINDOW = 256
N_EXPERTS = 32
TOP_K = 4
SWIGLU_ALPHA = 1.702
SWIGLU_LIMIT = 7.0
NORM_EPS = 1e-5

LANES = 128
VMEM_LIMIT = 56 * 1024 * 1024

F_Z, F_MS, F_XBC, F_MN, P_Q, P_KC, P_VC, P_KSW, P_VSW = 0, 1024, 2048, 4096, 5120, 6144, 6400, 6656, 7168


def _sigmoid(x):
    return 0.5 * jnp.tanh(0.5 * x) + 0.5


def _silu(x):
    return x * _sigmoid(x)


def _softplus(x):
    return jnp.maximum(x, 0.0) + jnp.log1p(jnp.exp(-jnp.abs(x)))


def _split3(x):
    hi = x.astype(BF16)
    r1 = x - hi.astype(F32)
    mid = r1.astype(BF16)
    lo = (r1 - mid.astype(F32)).astype(BF16)
    return hi, mid, lo


def _params(*sem):
    return pltpu.CompilerParams(dimension_semantics=sem, vmem_limit_bytes=VMEM_LIMIT)


def _norm_matmul_kernel(x_ref, nw_ref, w_ref, ws_ref, o_ref, small_ref, h_ref):
    @pl.when(pl.program_id(1) == 0)
    def _():
        x = x_ref[...]
        var = jnp.mean(x * x, axis=-1, keepdims=True)
        h_ref[...] = (x * lax.rsqrt(var + NORM_EPS) * nw_ref[...]).astype(BF16)
        small_ref[...] = jnp.dot(h_ref[...], ws_ref[...], preferred_element_type=F32)

    o_ref[...] = jnp.dot(h_ref[...], w_ref[...], preferred_element_type=F32).astype(o_ref.dtype)


def _norm_matmul(x2, norm_w, w, w_small, tm, tn):
    t, d = x2.shape
    n = w.shape[1]
    ns = w_small.shape[1]
    return pl.pallas_call(
        _norm_matmul_kernel,
        grid=(t // tm, n // tn),
        in_specs=[
            pl.BlockSpec((tm, d), lambda i, j: (i, 0)),
            pl.BlockSpec((1, d), lambda i, j: (0, 0)),
            pl.BlockSpec((d, tn), lambda i, j: (0, j)),
            pl.BlockSpec((d, ns), lambda i, j: (0, 0)),
        ],
        out_specs=[pl.BlockSpec((tm, tn), lambda i, j: (i, j)), pl.BlockSpec((tm, ns), lambda i, j: (i, 0))],
        out_shape=[jax.ShapeDtypeStruct((t, n), BF16), jax.ShapeDtypeStruct((t, ns), F32)],
        scratch_shapes=[pltpu.VMEM((tm, d), BF16)],
        compiler_params=_params("parallel", "arbitrary"),
        name="norm_matmul",
    )(x2, norm_w.reshape(1, d), w, w_small)


def _ssd_kernel(z_ref, xbc_ref, sm_ref, dtt_ref, convw_ref, convb_ref, dtb_ref, dtbt_ref,
                alog_ref, alogt_ref, dskip_ref, nw_ref, expand_ref, y_ref,
                xbuf, state, ydiag):
    L = SSD_CHUNK
    c = pl.program_id(1)

    @pl.when(c == 0)
    def _():
        xbuf[0:8, :] = jnp.zeros((8, XBC_DIM), F32)
        state[...] = jnp.zeros(state.shape, F32)

    xbuf[8:8 + L, :] = xbc_ref[...].astype(F32)
    acc = convb_ref[...] + convw_ref[0:1, :] * xbuf[5:5 + L, :]
    for k in range(1, CONV_K):
        acc = acc + convw_ref[k:k + 1, :] * xbuf[5 + k:5 + k + L, :]
    xbuf[0:8, :] = xbuf[L:L + 8, :]
    xbc = _silu(acc)
    xs = xbc[:, :D_INNER]

    lane = lax.broadcasted_iota(jnp.int32, (1, LANES), 1)
    a_row = jnp.where(lane < SSM_HEADS, -jnp.exp(alog_ref[...]), 0.0)
    dt = _softplus(sm_ref[...] + dtb_ref[...])
    a_dt = dt * a_row
    dt_t = _softplus(dtt_ref[0] + dtbt_ref[...])
    a_dt_t = dt_t * (-jnp.exp(alogt_ref[...]))

    row = lax.broadcasted_iota(jnp.int32, (L, L), 0)
    col = lax.broadcasted_iota(jnp.int32, (L, L), 1)
    lower = row >= col
    tri = lower.astype(BF16)
    tri_t = (row <= col).astype(BF16)
    cs = sum(jnp.dot(tri, part, preferred_element_type=F32) for part in _split3(a_dt))
    cs_t = sum(jnp.dot(part, tri_t, preferred_element_type=F32) for part in _split3(a_dt_t))
    cs_last = cs[L - 1:L, :]

    stacked = jnp.concatenate([dt, jnp.exp(cs_last - cs), jnp.exp(cs)], axis=0)
    wide = sum(jnp.dot(part, expand_ref[...], preferred_element_type=F32) for part in _split3(stacked))
    dt_x = wide[0:L]
    dte_x = wide[L:2 * L]
    ecs_x = wide[2 * L:3 * L]

    xdt = xs * dt_x
    xw = (xdt * dte_x).astype(BF16)
    xdt_b = xdt.astype(BF16)

    hpg = SSM_HEADS // SSM_GROUPS
    gw = hpg * SSM_HEAD_DIM
    y_off_parts = []
    for g in range(SSM_GROUPS):
        b_g = xbc[:, D_INNER + g * D_STATE:D_INNER + (g + 1) * D_STATE].astype(BF16)
        c_g = xbc[:, D_INNER + SSM_GROUPS * D_STATE + g * D_STATE:
                  D_INNER + SSM_GROUPS * D_STATE + (g + 1) * D_STATE].astype(BF16)
        cb = lax.dot_general(c_g, b_g, (((1,), (1,)), ((), ())), preferred_element_type=F32)
        for hh in range(hpg):
            h = g * hpg + hh
            seg = cs[:, h:h + 1] - cs_t[h:h + 1, :]
            decay = jnp.exp(jnp.where(lower, seg, -jnp.inf))
            m = (cb * decay).astype(BF16)
            ydiag[:, h * SSM_HEAD_DIM:(h + 1) * SSM_HEAD_DIM] = jnp.dot(
                m, xdt_b[:, h * SSM_HEAD_DIM:(h + 1) * SSM_HEAD_DIM], preferred_element_type=F32)
        st_prev = state[g]
        y_off_parts.append(jnp.dot(c_g, st_prev.astype(BF16), preferred_element_type=F32))
        st_new = lax.dot_general(b_g, xw[:, g * gw:(g + 1) * gw], (((0,), (0,)), ((), ())),
                                 preferred_element_type=F32)
        state[g] = st_prev * ecs_x[L - 1:L, g * gw:(g + 1) * gw] + st_new
    y_off = jnp.concatenate(y_off_parts, axis=1) * ecs_x

    y = (ydiag[...] + y_off + dskip_ref[...] * xs) * _silu(z_ref[...].astype(F32))
    for g in range(SSM_GROUPS):
        yg = y[:, g * gw:(g + 1) * gw]
        yg = yg * lax.rsqrt(jnp.mean(yg * yg, axis=-1, keepdims=True) + NORM_EPS)
        y_ref[:, g * gw:(g + 1) * gw] = (yg * nw_ref[:, g * gw:(g + 1) * gw]).astype(y_ref.dtype)


def _ssd(proj, small, dt_t, conv_w, conv_b, dt_bias, a_log, d_skip, norm_w, bsz, seq):
    L = SSD_CHUNK
    nc = seq // L
    pad = LANES - SSM_HEADS
    dtb = jnp.pad(dt_bias, (0, pad)).reshape(1, LANES)
    alog = jnp.pad(a_log, (0, pad)).reshape(1, LANES)
    dskip_x = jnp.repeat(d_skip, SSM_HEAD_DIM).reshape(1, D_INNER)
    expand = (np.arange(LANES)[:, None] == (np.arange(D_INNER)[None, :] // SSM_HEAD_DIM)).astype(np.float32)
    const = lambda shape: pl.BlockSpec(shape, lambda b, c: (0,) * len(shape))
    return pl.pallas_call(
        _ssd_kernel,
        grid=(bsz, nc),
        in_specs=[
            pl.BlockSpec((L, D_INNER), lambda b, c: (b * nc + c, F_Z // D_INNER)),
            pl.BlockSpec((L, XBC_DIM), lambda b, c: (b * nc + c, F_XBC // XBC_DIM)),
            pl.BlockSpec((L, LANES), lambda b, c: (b * nc + c, 0)),
            pl.BlockSpec((1, SSM_HEADS, L), lambda b, c: (b, 0, c)),
            const((CONV_K, XBC_DIM)), const((1, XBC_DIM)),
            const((1, LANES)), const((SSM_HEADS, 1)),
            const((1, LANES)), const((SSM_HEADS, 1)),
            const((1, D_INNER)), const((1, D_INNER)),
            const((LANES, D_INNER)),
        ],
        out_specs=pl.BlockSpec((L, D_INNER), lambda b, c: (b * nc + c, 0)),
        out_shape=jax.ShapeDtypeStruct((bsz * seq, D_INNER), BF16),
        scratch_shapes=[
            pltpu.VMEM((L + 8, XBC_DIM), F32),
            pltpu.VMEM((SSM_GROUPS, D_STATE, (SSM_HEADS // SSM_GROUPS) * SSM_HEAD_DIM), F32),
            pltpu.VMEM((L, D_INNER), F32),
        ],
        compiler_params=_params("parallel", "arbitrary"),
        name="ssd",
    )(proj, proj, small, dt_t, conv_w, conv_b.reshape(1, XBC_DIM),
      dtb, dt_bias.reshape(SSM_HEADS, 1), alog, a_log.reshape(SSM_HEADS, 1),
      dskip_x, norm_w.reshape(1, D_INNER), jnp.asarray(expand, dtype=BF16))


def _gelu_tanh(x):
    return 0.5 * x * (1.0 + jnp.tanh(np.sqrt(2.0 / np.pi) * (x + 0.044715 * (x * x * x))))


def _compress_kernel(cur_ref, nxt_ref, pos_ref, w1_ref, w2_ref, o_ref):
    half = CMP_STRIDE * NSA_HEAD_DIM
    lo = (cur_ref[0, 0].astype(F32) + pos_ref[0:1, :]).astype(BF16)
    hi = (nxt_ref[0, 0].astype(F32) + pos_ref[1:2, :]).astype(BF16)
    hid = (jnp.dot(lo, w1_ref[0:half, :], preferred_element_type=F32)
           + jnp.dot(hi, w1_ref[half:2 * half, :], preferred_element_type=F32))
    o_ref[0, 0] = jnp.dot(_gelu_tanh(hid).astype(BF16), w2_ref[...],
                          preferred_element_type=F32).astype(o_ref.dtype)


def _compress(kv, pos, w1, w2, bsz, seq):
    nch = seq // CMP_STRIDE
    half = CMP_STRIDE * NSA_HEAD_DIM
    ch = kv.reshape(bsz, nch, CMP_STRIDE, NSA_GROUPS, NSA_HEAD_DIM)
    ch = jnp.transpose(ch, (0, 3, 1, 2, 4)).reshape(bsz, NSA_GROUPS, nch, half)
    nxt = jnp.concatenate([ch[:, :, 1:], jnp.zeros_like(ch[:, :, :1])], axis=2)
    blk = pl.BlockSpec((1, 1, nch, half), lambda b, g: (b, g, 0, 0))
    return pl.pallas_call(
        _compress_kernel,
        grid=(bsz, NSA_GROUPS),
        in_specs=[
            blk, blk,
            pl.BlockSpec((2, half), lambda b, g: (0, 0)),
            pl.BlockSpec((2 * half, CMP_HIDDEN), lambda b, g: (0, 0)),
            pl.BlockSpec((CMP_HIDDEN, NSA_HEAD_DIM), lambda b, g: (0, 0)),
        ],
        out_specs=pl.BlockSpec((1, 1, nch, NSA_HEAD_DIM), lambda b, g: (b, g, 0, 0)),
        out_shape=jax.ShapeDtypeStruct((bsz, NSA_GROUPS, nch, NSA_HEAD_DIM), BF16),
        compiler_params=_params("parallel", "parallel"),
        name="compress",
    )(ch, nxt, pos.reshape(2, half), w1.astype(BF16), w2.astype(BF16))


NSA_TQ = 512
NSA_KC = 512
NSA_SUB = 256
MASK_VALUE = -1e30
SOFTMAX_M0 = -1e20


def _nsa_kernel(q_ref, kc_ref, vc_ref, k_ref, v_ref, onehot_ref, gate_ref, overlap_ref, gexp_ref, o_ref):
    R, TQ, KC, DH, SUB = NSA_REP, NSA_TQ, NSA_KC, NSA_HEAD_DIM, NSA_SUB
    assert 2 * DH == LANES and TQ == KC and TQ % SUB == 0 and WINDOW % SUB == 0
    i = pl.program_id(2)
    q0 = i * TQ
    nt = (((1,), (1,)), ((), ()))
    tcol = q0 + lax.broadcasted_iota(jnp.int32, (TQ, 1), 0)
    qblk = q_ref[...]
    scale = DH ** -0.5
    qf = [qblk[:, r * DH:(r + 1) * DH].astype(F32) * scale for r in range(R)]
    qh = [q.astype(BF16) for q in qf]
    v_width = LANES
    sel_half = lax.broadcasted_iota(jnp.int32, (1, LANES), 1) < DH
    one = jnp.ones((), BF16)

    def flash_step(m_i, acc, q, k, v_ext, mask_bias=None):
        s = lax.dot_general(q, k, nt, preferred_element_type=F32)
        if mask_bias is not None:
            heads = s.shape[0] // mask_bias.shape[0]
            s = (s.reshape((heads,) + mask_bias.shape) + mask_bias[None]).reshape(s.shape)
        m_new = jnp.maximum(m_i, jnp.max(s, axis=-1, keepdims=True))
        p = jnp.exp((s - m_new).astype(BF16))
        acc = acc * jnp.exp(m_i - m_new) + jnp.dot(p, v_ext, preferred_element_type=F32)
        return m_new, acc

    def spread(x, mat):
        hi = x.astype(BF16)
        lo = (x - hi.astype(F32)).astype(BF16)
        return jnp.dot(hi, mat, preferred_element_type=F32) + jnp.dot(lo, mat, preferred_element_type=F32)

    kc = kc_ref[0, 0]
    vc = vc_ref[0, 0]
    ncmp = kc.shape[0]
    cmp_end = lax.broadcasted_iota(jnp.int32, (1, ncmp), 1) * CMP_STRIDE + (CMP_BLOCK - 1)
    cmask = cmp_end <= tcol
    lg = lax.dot_general(jnp.concatenate(qh, axis=0), kc, nt, preferred_element_type=F32).reshape(R, TQ, ncmp)
    lg = jnp.where(cmask[None], lg, -jnp.inf)
    mx = jnp.max(lg, axis=-1, keepdims=True)
    mx = jnp.where(mx > -jnp.inf, mx, 0.0)
    e = jnp.where(cmask[None], jnp.exp(lg - mx), 0.0)
    p = e / jnp.maximum(jnp.sum(e, axis=-1, keepdims=True), 1e-30)
    o_all = jnp.dot(p.reshape(R * TQ, ncmp).astype(BF16), vc, preferred_element_type=F32)
    o_cmp = [o_all[r * TQ:(r + 1) * TQ] for r in range(R)]
    p4 = (p[0] + p[1]) + (p[2] + p[3])

    imp = sum(lax.dot_general(overlap_ref[...], part, nt, preferred_element_type=F32)
              for part in _split3(p4))
    nsel = imp.shape[0]
    trow = q0 + lax.broadcasted_iota(jnp.int32, (1, TQ), 1)
    cur = trow // SEL_BLOCK
    jj = lax.broadcasted_iota(jnp.int32, (nsel, 1), 0)
    forced = (jj == 0) | (jj == cur) | (jj == cur - 1)
    val = jnp.where(forced, jnp.inf, jnp.where(jj <= cur, imp, -jnp.inf))
    rank = jnp.zeros((nsel, TQ), jnp.int32)
    for b in range(nsel):
        vb = val[b:b + 1, :]
        beats = (vb > val) | ((vb == val) & (jj > b))
        rank = rank + beats.astype(jnp.int32)
    bias = jnp.transpose(jnp.where(rank < SEL_TOPN, 0.0, MASK_VALUE))
    pad = jnp.zeros((TQ, LANES - DH - nsel), F32)
    qa = [jnp.concatenate([qf[r], bias, pad], axis=1).astype(BF16) for r in range(R)]
    qw = [jnp.concatenate([jnp.zeros((TQ, DH), F32), qf[r]], axis=1).astype(BF16) for r in range(R)]

    qa_all = jnp.concatenate(qa, axis=0)

    def sel_kv(k0, n):
        k = jnp.where(sel_half, k_ref[pl.ds(k0, n), :], onehot_ref[pl.ds(k0, n), :])
        v = jnp.where(sel_half, v_ref[pl.ds(k0, n), :], one)
        return k, v

    def sel_step(kb, carry):
        k, v = sel_kv(pl.multiple_of(kb * KC, KC), KC)
        return flash_step(carry[0], carry[1], qa_all, k, v)

    n_full = q0 // KC
    m_all, acc_all = lax.fori_loop(0, n_full, sel_step, (jnp.full((R * TQ, 1), SOFTMAX_M0, F32),
                                                         jnp.zeros((R * TQ, v_width), F32)))
    carry = tuple((m_all[r * TQ:(r + 1) * TQ], acc_all[r * TQ:(r + 1) * TQ]) for r in range(R))

    d0 = pl.multiple_of(q0, TQ)
    stack = lambda parts: jnp.concatenate(parts, axis=0)
    sel_parts, win_parts = [], []
    for j in range(TQ // SUB):
        rows = slice(j * SUB, (j + 1) * SUB)
        tsub = tcol[rows]
        width = (j + 1) * SUB
        bias_d = jnp.where(d0 + lax.broadcasted_iota(jnp.int32, (1, width), 1) <= tsub, 0.0, MASK_VALUE)
        k_d, v_d = sel_kv(d0, width)
        w0 = pl.multiple_of(jnp.maximum(q0 + j * SUB - WINDOW, 0), SUB)
        kpos_w = w0 + lax.broadcasted_iota(jnp.int32, (1, WINDOW + SUB), 1)
        bias_w = jnp.where((kpos_w <= tsub) & (kpos_w > tsub - WINDOW), 0.0, MASK_VALUE)
        k_w = k_ref[pl.ds(w0, WINDOW + SUB), :]
        v_w = jnp.where(sel_half, one, v_ref[pl.ds(w0, WINDOW + SUB), :])
        sel_parts.append(flash_step(stack([carry[r][0][rows] for r in range(R)]),
                                    stack([carry[r][1][rows] for r in range(R)]),
                                    stack([qa[r][rows] for r in range(R)]), k_d, v_d, bias_d)[1])
        win_parts.append(flash_step(jnp.full((R * SUB, 1), SOFTMAX_M0, F32), jnp.zeros((R * SUB, v_width), F32),
                                    stack([qw[r][rows] for r in range(R)]), k_w, v_w, bias_w)[1])
    unstack = lambda parts, r: stack([p[r * SUB:(r + 1) * SUB] for p in parts])
    acc_s = [unstack(sel_parts, r) for r in range(R)]
    acc_w = [unstack(win_parts, r) for r in range(R)]

    gx = spread(_sigmoid(gate_ref[...]), gexp_ref[0])

    def normalised(accs, lo):
        return jnp.concatenate([(acc / jnp.maximum(pltpu.roll(acc, DH, 1), 1e-30))[:, lo:lo + DH] for acc in accs],
                               axis=1)

    w = R * DH
    out = (gx[:, 0:w] * jnp.concatenate(o_cmp, axis=1) + gx[:, w:2 * w] * normalised(acc_s, 0)
           + gx[:, 2 * w:3 * w] * normalised(acc_w, DH))
    o_ref[...] = out.astype(o_ref.dtype)


def _nsa(proj, small, kc, vc, bsz, seq):
    G, R, DH, TQ = NSA_GROUPS, NSA_REP, NSA_HEAD_DIM, NSA_TQ
    n_cmp_pad = seq // CMP_STRIDE
    n_sel = seq // SEL_BLOCK
    nq = seq // TQ

    assert NSA_KC % TQ == 0 and seq % NSA_KC == 0 and TQ % WINDOW == 0 and 2 * DH == LANES

    assert n_sel <= LANES - DH
    block_onehot = np.zeros((seq, LANES), np.float32)
    block_onehot[np.arange(seq), DH + np.arange(seq) // SEL_BLOCK] = 1.0

    c_start = np.arange(n_cmp_pad) * CMP_STRIDE
    s_start = np.arange(n_sel) * SEL_BLOCK
    overlap = ((c_start[:, None] < s_start[None, :] + SEL_BLOCK)
               & (c_start[:, None] + CMP_BLOCK > s_start[None, :])).astype(np.float32)
    overlap[(seq - CMP_BLOCK) // CMP_STRIDE + 1:] = 0.0
    col = np.arange(3 * R * DH)
    gate_lane = SSM_HEADS + 3 * ((col % (R * DH)) // DH) + col // (R * DH)
    gexp = np.stack([(np.arange(LANES)[:, None] == (gate_lane + 3 * R * g)[None, :]) for g in range(G)])

    qblk = pl.BlockSpec((TQ, R * DH), lambda b, g, i: (b * nq + i, P_Q // (R * DH) + g))
    oblk = pl.BlockSpec((TQ, R * DH), lambda b, g, i: (b * nq + i, g))
    cblk = pl.BlockSpec((1, 1, n_cmp_pad, DH), lambda b, g, i: (b, g, 0, 0))
    return pl.pallas_call(
        _nsa_kernel,
        grid=(bsz, G, nq),
        in_specs=[
            qblk, cblk, cblk,
            pl.BlockSpec((seq, LANES), lambda b, g, i: (b, P_KSW // LANES + g)),
            pl.BlockSpec((seq, LANES), lambda b, g, i: (b, P_VSW // LANES + g)),
            pl.BlockSpec((seq, LANES), lambda b, g, i: (0, 0)),
            pl.BlockSpec((TQ, LANES), lambda b, g, i: (b * nq + i, 0)),
            pl.BlockSpec((n_sel, n_cmp_pad), lambda b, g, i: (0, 0)),
            pl.BlockSpec((1, LANES, 3 * R * DH), lambda b, g, i: (g, 0, 0)),
        ],
        out_specs=oblk,
        out_shape=jax.ShapeDtypeStruct((bsz * seq, G * R * DH), BF16),
        compiler_params=_params("parallel", "parallel", "arbitrary"),
        name="nsa_attention",
    )(proj, kc, vc, proj, proj, jnp.asarray(block_onehot, dtype=BF16), small,
      jnp.asarray(overlap.T, dtype=BF16), jnp.asarray(gexp, dtype=BF16))


ROW_TILES = D_MODEL // LANES


def _store_token_major(ref, val):
    n = val.shape[0]
    for c in range(ROW_TILES):
        ref[pl.ds(c, n, stride=ROW_TILES), :] = val[:, c * LANES:(c + 1) * LANES]


def _load_token_major(ref, n):
    return jnp.concatenate([ref[pl.ds(c, n, stride=ROW_TILES), :] for c in range(ROW_TILES)], axis=1)


def _merge_kernel(ys_ref, yn_ref, gs_ref, gn_ref, x_ref, wbs_ref, wbn_ref, wo_ref, fw_ref,
                  rwh_ref, rwl_ref, rb_ref, tri_ref, x1_ref, h_ref, ti_ref, tw_ref, slot_ref, cnt_ref, base_ref):
    @pl.when(pl.program_id(0) == 0)
    def _():
        base_ref[...] = jnp.zeros(base_ref.shape, F32)

    a = jnp.dot(ys_ref[...], wbs_ref[...], preferred_element_type=F32)
    b = jnp.dot(yn_ref[...], wbn_ref[...], preferred_element_type=F32)
    merged = _sigmoid(gs_ref[...].astype(F32)) * a + _sigmoid(gn_ref[...].astype(F32)) * b
    x1 = x_ref[...] + jnp.dot(merged.astype(BF16), wo_ref[...], preferred_element_type=F32)
    x1_ref[...] = x1
    var = jnp.mean(x1 * x1, axis=-1, keepdims=True)
    h = x1 * lax.rsqrt(var + NORM_EPS) * fw_ref[...]
    _store_token_major(h_ref, h)

    nt = (((1,), (1,)), ((), ()))
    h_hi = h.astype(BF16)
    h_lo = (h - h_hi.astype(F32)).astype(BF16)
    logits = (lax.dot_general(rwh_ref[...], h_hi, nt, preferred_element_type=F32)
              + (lax.dot_general(rwl_ref[...], h_hi, nt, preferred_element_type=F32)
                 + lax.dot_general(rwh_ref[...], h_lo, nt, preferred_element_type=F32))) + rb_ref[...]
    ne = logits.shape[0]
    jj = lax.broadcasted_iota(jnp.int32, (ne, 1), 0)
    rank = jnp.zeros(logits.shape, jnp.int32)
    for e in range(ne):
        ce = logits[e:e + 1, :]
        beats = (ce > logits) | ((ce == logits) & (jj > e))
        rank = rank + beats.astype(jnp.int32)
    sel = rank < TOP_K
    mx = jnp.max(logits, axis=0, keepdims=True)
    p = jnp.where(sel, jnp.exp(logits - mx), 0.0)
    p = p / jnp.sum(p, axis=0, keepdims=True)

    self01 = sel.astype(F32)
    before = jnp.dot(self01.astype(BF16), tri_ref[...], preferred_element_type=F32) + base_ref[...]
    for k in range(TOP_K):
        hit = rank == k
        ti_ref[k:k + 1, :] = jnp.sum(jnp.where(hit, jj, 0), axis=0, keepdims=True)
        tw_ref[k:k + 1, :] = jnp.sum(jnp.where(hit, p, 0.0), axis=0, keepdims=True)
        slot_ref[k:k + 1, :] = jnp.sum(jnp.where(hit, before, 0.0), axis=0, keepdims=True).astype(jnp.int32)
    base_ref[...] = base_ref[...] + jnp.sum(self01, axis=1, keepdims=True)
    cnt_ref[...] = base_ref[...]


def _merge_route(y_ssm, y_nsa, proj, x2, wbs, wbn, wo, ffn_w, router_w, router_b, tm):
    t, d = x2.shape
    row = lambda col: pl.BlockSpec((tm, d), lambda i: (i, col))
    const = lambda shape: pl.BlockSpec(shape, lambda i: (0,) * len(shape))
    kblk = pl.BlockSpec((TOP_K, tm), lambda i: (0, i))
    tri = np.triu(np.ones((tm, tm), np.float32), 1)
    rw_t = router_w.T
    rw_hi = rw_t.astype(BF16)
    rw_lo = (rw_t - rw_hi.astype(F32)).astype(BF16)
    return pl.pallas_call(
        _merge_kernel,
        grid=(t // tm,),
        in_specs=[
            row(0), row(0), row(F_MS // d), row(F_MN // d), row(0),
            const((d, d)), const((d, d)), const((d, d)), const((1, d)),
            const((N_EXPERTS, d)), const((N_EXPERTS, d)), const((N_EXPERTS, 1)), const((tm, tm)),
        ],
        out_specs=[row(0), pl.BlockSpec((tm * ROW_TILES, LANES), lambda i: (i, 0)),
                   kblk, kblk, kblk, const((N_EXPERTS, 1))],
        out_shape=[jax.ShapeDtypeStruct((t, d), F32), jax.ShapeDtypeStruct((t * ROW_TILES, LANES), F32),
                   jax.ShapeDtypeStruct((TOP_K, t), jnp.int32), jax.ShapeDtypeStruct((TOP_K, t), F32),
                   jax.ShapeDtypeStruct((TOP_K, t), jnp.int32), jax.ShapeDtypeStruct((N_EXPERTS, 1), F32)],
        scratch_shapes=[pltpu.VMEM((N_EXPERTS, 1), F32)],
        compiler_params=_params("arbitrary"),
        name="merge_route",
    )(y_ssm, y_nsa, proj, proj, x2, wbs.astype(BF16), wbn.astype(BF16), wo.astype(BF16),
      ffn_w.reshape(1, d), rw_hi, rw_lo, router_b.reshape(N_EXPERTS, 1), jnp.asarray(tri, dtype=BF16))


MOE_TM = 256
MOE_RING = 3
GU_CHUNK = 2 * LANES


def _moe_kernel(te_ref, nu_ref, tok0_ref, tok1_ref, tokn_ref, dst_ref, h_hbm, wgu_ref, bgu_ref, wd_ref, bd_ref,
                perm_ref, y_hbm, xbuf, ybuf, wgu_s, wd_s, gsem, ssem, *, spare_row):
    tm, rt = MOE_TM, ROW_TILES
    i = pl.program_id(0)
    nu = nu_ref[0]
    slot = i % MOE_RING
    prev = (i + MOE_RING - 1) % MOE_RING
    new_expert = (i == 0) | (te_ref[i] != te_ref[jnp.maximum(i - 1, 0)])

    @pl.when(new_expert & (i < nu))
    def _():
        wd_s[...] = wd_ref[0].astype(BF16)
        for c in range(wgu_s.shape[1] // GU_CHUNK):
            cols = slice(c * GU_CHUNK, (c + 1) * GU_CHUNK)
            wgu_s[:, cols] = jnp.dot(wgu_ref[0, :, cols].astype(BF16), perm_ref[...],
                                     preferred_element_type=F32).astype(BF16)

    def gather_copy(src_row, r, sl):
        return pltpu.make_async_copy(h_hbm.at[pl.ds(src_row, rt), :], xbuf.at[sl, pl.ds(r * rt, rt), :],
                                     gsem.at[sl])

    def scatter_copy(dst_row, r, sl):
        return pltpu.make_async_copy(ybuf.at[sl, pl.ds(r * rt, rt), :], y_hbm.at[pl.ds(dst_row, rt), :],
                                     ssem.at[sl])

    def start_gather(tok_ref, sl):
        for r in range(tm):
            gather_copy(pl.multiple_of(tok_ref[0, 0, r], rt), r, sl).start()

    def start_scatter(sl):
        for r in range(tm):
            scatter_copy(pl.multiple_of(dst_ref[0, 0, r], rt), r, sl).start()

    def wait_gather(sl):
        pltpu.make_async_copy(h_hbm.at[pl.ds(0, tm * rt), :], xbuf.at[sl], gsem.at[sl]).wait()

    def wait_scatter(sl):
        pltpu.make_async_copy(ybuf.at[sl], y_hbm.at[pl.ds(0, tm * rt), :], ssem.at[sl]).wait()

    @pl.when(i == 0)
    def _():
        ybuf[...] = jnp.zeros(ybuf.shape, ybuf.dtype)
        start_gather(tok0_ref, 0)
        start_gather(tok1_ref, 1)
        for sl in range(2):
            for r in range(tm):
                scatter_copy((spare_row + sl * tm + r) * rt, r, sl).start()

    @pl.when(i < nu)
    def _():
        wait_gather(slot)
        wait_scatter(slot)
        x = _load_token_major(xbuf.at[slot], tm).astype(BF16)
        start_gather(tokn_ref, prev)
        start_scatter(prev)
        h1 = jnp.dot(x, wgu_s[...], preferred_element_type=F32) + bgu_ref[0]
        nchunk = h1.shape[1] // GU_CHUNK
        glu = jnp.concatenate([h1[:, c * GU_CHUNK:c * GU_CHUNK + LANES] for c in range(nchunk)], axis=1)
        lin = jnp.concatenate([h1[:, c * GU_CHUNK + LANES:(c + 1) * GU_CHUNK] for c in range(nchunk)], axis=1)
        glu = jnp.minimum(glu, SWIGLU_LIMIT)
        lin = jnp.clip(lin, -SWIGLU_LIMIT, SWIGLU_LIMIT)
        act = glu * _sigmoid(SWIGLU_ALPHA * glu) * (lin + 1.0)
        y = jnp.dot(act.astype(BF16), wd_s[...], preferred_element_type=F32) + bd_ref[0]
        _store_token_major(ybuf.at[slot], y)

    @pl.when(i == nu)
    def _():
        nxt = (i + 1) % MOE_RING
        wait_gather(slot)
        wait_gather(nxt)
        wait_scatter(slot)
        wait_scatter(nxt)
        start_scatter(prev)
        wait_scatter(prev)


def _moe_experts(h_rows, tok_tiles, dst_tiles, tile_expert, n_used, wgu, bgu, wd, bd, n_out_rows, spare_row):
    d = wd.shape[1]
    tm, rt = MOE_TM, ROW_TILES
    n_tiles = tok_tiles.shape[0]
    perm = np.zeros((GU_CHUNK, GU_CHUNK), np.float32)
    perm[2 * np.arange(LANES), np.arange(LANES)] = 1.0
    perm[2 * np.arange(LANES) + 1, LANES + np.arange(LANES)] = 1.0
    smem = lambda f: pl.BlockSpec((1, 1, tm), f, memory_space=pltpu.SMEM)
    grid_spec = pltpu.PrefetchScalarGridSpec(
        num_scalar_prefetch=2,
        grid=(n_tiles,),
        in_specs=[
            smem(lambda i, te, nu: (0, 0, 0)),
            smem(lambda i, te, nu: (1, 0, 0)),
            smem(lambda i, te, nu: (jnp.minimum(i + MOE_RING - 1, n_tiles - 1), 0, 0)),
            smem(lambda i, te, nu: (i, 0, 0)),
            pl.BlockSpec(memory_space=pl.ANY),
            pl.BlockSpec((1, d, 2 * d), lambda i, te, nu: (te[i], 0, 0)),
            pl.BlockSpec((1, 1, 2 * d), lambda i, te, nu: (te[i], 0, 0)),
            pl.BlockSpec((1, d, d), lambda i, te, nu: (te[i], 0, 0)),
            pl.BlockSpec((1, 1, d), lambda i, te, nu: (te[i], 0, 0)),
            pl.BlockSpec((GU_CHUNK, GU_CHUNK), lambda i, te, nu: (0, 0)),
        ],
        out_specs=pl.BlockSpec(memory_space=pl.ANY),
        scratch_shapes=[
            pltpu.VMEM((MOE_RING, tm * rt, LANES), F32),
            pltpu.VMEM((MOE_RING, tm * rt, LANES), F32),
            pltpu.VMEM((d, 2 * d), BF16),
            pltpu.VMEM((d, d), BF16),
            pltpu.SemaphoreType.DMA((MOE_RING,)),
            pltpu.SemaphoreType.DMA((MOE_RING,)),
        ],
    )
    return pl.pallas_call(
        functools.partial(_moe_kernel, spare_row=spare_row),
        grid_spec=grid_spec,
        out_shape=jax.ShapeDtypeStruct((n_out_rows * rt, LANES), F32),
        compiler_params=_params("arbitrary"),
        name="moe_experts",
    )(tile_expert, n_used, tok_tiles, tok_tiles, tok_tiles, dst_tiles, h_rows, wgu, bgu, wd, bd,
      jnp.asarray(perm, dtype=BF16))


def _final_kernel(x1_ref, y0_ref, y1_ref, y2_ref, y3_ref, tw_ref, nw_ref, o_ref):
    tm = x1_ref.shape[0]
    tw = tw_ref[...]
    yk = [_load_token_major(r, tm) for r in (y0_ref, y1_ref, y2_ref, y3_ref)]
    moe = ((tw[:, 0:1] * yk[0] + tw[:, 1:2] * yk[1]) + (tw[:, 2:3] * yk[2] + tw[:, 3:4] * yk[3]))
    x = x1_ref[...] + moe
    var = jnp.mean(x * x, axis=-1, keepdims=True)
    o_ref[...] = x * lax.rsqrt(var + NORM_EPS) * nw_ref[...]


def _final_norm(x1, y_rows, top_w, norm_w, tm):
    t, d = x1.shape
    nt = t // tm
    yblk = lambda k: pl.BlockSpec((tm * ROW_TILES, LANES), lambda i: (k * nt + i, 0))
    return pl.pallas_call(
        _final_kernel,
        grid=(nt,),
        in_specs=[
            pl.BlockSpec((tm, d), lambda i: (i, 0)),
            yblk(0), yblk(1), yblk(2), yblk(3),
            pl.BlockSpec((tm, TOP_K), lambda i: (i, 0)),
            pl.BlockSpec((1, d), lambda i: (0, 0)),
        ],
        out_specs=pl.BlockSpec((tm, d), lambda i: (i, 0)),
        out_shape=jax.ShapeDtypeStruct((t, d), F32),
        compiler_params=_params("parallel"),
        name="final_norm",
    )(x1, y_rows, y_rows, y_rows, y_rows, top_w, norm_w.reshape(1, d))


def _dispatch_tables(top_i, slot, counts, n_rows):
    t = top_i.shape[1]
    n = t * TOP_K
    tm, rt = MOE_TM, ROW_TILES
    padded = ((counts + tm - 1) // tm) * tm
    pend = jnp.cumsum(padded)
    pstart = pend - padded
    onehot = top_i[:, :, None] == jnp.arange(N_EXPERTS, dtype=jnp.int32)
    dest = slot + jnp.sum(jnp.where(onehot, pstart, 0), axis=-1)
    row_pair = jnp.full((n_rows,), -1, jnp.int32).at[dest.reshape(-1)].set(jnp.arange(n, dtype=jnp.int32))
    live = row_pair >= 0
    tok_tiles = (jnp.where(live, row_pair % t, 0) * rt).reshape(n_rows // tm, 1, tm)
    rows = jnp.arange(n_rows, dtype=jnp.int32)
    out_row = jnp.where(live, row_pair, n + ((rows // tm) % 2) * tm + rows % tm)
    spare = n + 2 * tm + jnp.arange(tm, dtype=jnp.int32)
    dst_tiles = (jnp.concatenate([spare, out_row]) * rt).reshape(n_rows // tm + 1, 1, tm)
    tile_start = jnp.arange(n_rows // tm, dtype=jnp.int32) * tm
    tile_expert = jnp.minimum(jnp.sum(tile_start[:, None] >= pend[None, :], axis=-1), N_EXPERTS - 1)
    n_used = (pend[-1] // tm).reshape(1)
    return tok_tiles, dst_tiles, tile_expert.astype(jnp.int32), n_used.astype(jnp.int32)


def _in_proj_weights(w_in):
    o = np.cumsum([0, D_INNER, XBC_DIM, SSM_HEADS, 1024, 256, 256, 256, 256, 256, 256, 3 * NSA_HEADS, 2 * D_MODEL])
    z, xbc, dt, q, kv, gate, mg = (w_in[:, o[0]:o[1]], w_in[:, o[1]:o[2]], w_in[:, o[2]:o[3]],
                                   w_in[:, o[3]:o[4]], w_in[:, o[4]:o[10]], w_in[:, o[10]:o[11]],
                                   w_in[:, o[11]:o[12]])
    small_pad = jnp.zeros((D_MODEL, LANES - SSM_HEADS - 3 * NSA_HEADS), w_in.dtype)
    kvw = NSA_KV_WIDTH
    k_c, v_c, k_s, v_s, k_w, v_w = [kv[:, n * kvw:(n + 1) * kvw] for n in range(6)]

    def pair_by_group(a, b):
        shape = (D_MODEL, NSA_GROUPS, 1, NSA_HEAD_DIM)
        return jnp.concatenate([a.reshape(shape), b.reshape(shape)], axis=2).reshape(D_MODEL, 2 * kvw)

    w_p = jnp.concatenate([z, mg[:, :D_MODEL], xbc, mg[:, D_MODEL:], q, k_c, v_c,
                           pair_by_group(k_s, k_w), pair_by_group(v_s, v_w)], axis=1)
    w_small = jnp.concatenate([dt, gate, small_pad], axis=1)
    return w_p.astype(BF16), w_small.astype(BF16)


def kernel(x, mix_norm_w, w_in, conv_w, conv_b, dt_bias, a_log, d_skip, ssm_norm_w, cmp_pos_k, cmp_w1_k, cmp_w2_k, cmp_pos_v, cmp_w1_v, cmp_w2_v, w_branch_ssm, w_branch_nsa, w_out, ffn_norm_w, router_w, router_b, w_gate_up, b_gate_up, w_down, b_down, final_norm_w):
    bsz, seq, d = x.shape
    t = bsz * seq
    x2 = x.reshape(t, d)
    depth = w_in.shape[0]
    assert depth == 1, "single-layer block"
    for l in range(depth):
        w_p, w_small = _in_proj_weights(w_in[l])
        proj, small = _norm_matmul(x2, mix_norm_w[l], w_p, w_small, 2048, 512)

        dt_t = jnp.transpose(small[:, :SSM_HEADS].reshape(bsz, seq, SSM_HEADS), (0, 2, 1))
        y_ssm = _ssd(proj, small, dt_t, conv_w[l], conv_b[l], dt_bias[l], a_log[l], d_skip[l], ssm_norm_w[l],
                     bsz, seq)

        kc = _compress(proj[:, P_KC:P_KC + NSA_KV_WIDTH], cmp_pos_k[l], cmp_w1_k[l], cmp_w2_k[l], bsz, seq)
        vc = _compress(proj[:, P_VC:P_VC + NSA_KV_WIDTH], cmp_pos_v[l], cmp_w1_v[l], cmp_w2_v[l], bsz, seq)
        y_nsa = _nsa(proj, small, kc, vc, bsz, seq)

        x1, h, top_i, top_w, slot, counts = _merge_route(
            y_ssm, y_nsa, proj, x2, w_branch_ssm[l], w_branch_nsa[l], w_out[l], ffn_norm_w[l],
            router_w[l], router_b[l], 512)

        n_rows = t * TOP_K + N_EXPERTS * MOE_TM
        tok_tiles, dst_tiles, tile_expert, n_used = _dispatch_tables(
            top_i, slot, counts[:, 0].astype(jnp.int32), n_rows)
        nchunk = 2 * d // GU_CHUNK
        bgu = jnp.transpose(b_gate_up[l].reshape(N_EXPERTS, nchunk, LANES, 2), (0, 1, 3, 2))
        y_rows = _moe_experts(h, tok_tiles, dst_tiles, tile_expert, n_used, w_gate_up[l],
                              bgu.reshape(N_EXPERTS, 1, 2 * d), w_down[l], b_down[l][:, None, :],
                              n_out_rows=t * TOP_K + 3 * MOE_TM, spare_row=t * TOP_K)
    out = _final_norm(x1, y_rows, top_w.T, final_norm_w, MOE_TM)
    return out.reshape(bsz, seq, d)
```

```python
import functools

import numpy as np
import jax
import jax.numpy as jnp
from jax import lax
from jax.experimental import pallas as pl
from jax.experimental.pallas import tpu as pltpu

F32 = jnp.float32
BF16 = jnp.bfloat16

D_MODEL = 1024
D_INNER = 1024
SSM_HEAD_DIM = 64
SSM_HEADS = 16
SSM_GROUPS = 4
D_STATE = 128
CONV_K = 4
XBC_DIM = D_INNER + 2 * SSM_GROUPS * D_STATE
SSD_CHUNK = 128
NSA_HEAD_DIM = 64
NSA_HEADS = 16
NSA_GROUPS = 4
NSA_REP = 4
NSA_KV_WIDTH = 256
CMP_BLOCK = 32
CMP_STRIDE = 16
CMP_HIDDEN = 128
SEL_BLOCK = 64
SEL_TOPN = 8
WINDOW = 256
N_EXPERTS = 32
TOP_K = 4
SWIGLU_ALPHA = 1.702
SWIGLU_LIMIT = 7.0
NORM_EPS = 1e-5

LANES = 128
VMEM_LIMIT = 56 * 1024 * 1024

F_Z, F_MS, F_XBC, F_MN, P_Q, P_KC, P_VC, P_KSW, P_VSW = 0, 1024, 2048, 4096, 5120, 6144, 6400, 6656, 7168


def _sigmoid(x):
    return 0.5 * jnp.tanh(0.5 * x) + 0.5


def _silu(x):
    return x * _sigmoid(x)


def _softplus(x):
    return jnp.maximum(x, 0.0) + jnp.log1p(jnp.exp(-jnp.abs(x)))


def _split3(x):
    hi = x.astype(BF16)
    r1 = x - hi.astype(F32)
    mid = r1.astype(BF16)
    lo = (r1 - mid.astype(F32)).astype(BF16)
    return hi, mid, lo


def _params(*sem):
    return pltpu.CompilerParams(dimension_semantics=sem, vmem_limit_bytes=VMEM_LIMIT)


def _norm_matmul_kernel(x_ref, nw_ref, w_ref, ws_ref, o_ref, small_ref, h_ref):
    @pl.when(pl.program_id(1) == 0)
    def _():
        x = x_ref[...]
        var = jnp.mean(x * x, axis=-1, keepdims=True)
        h_ref[...] = (x * lax.rsqrt(var + NORM_EPS) * nw_ref[...]).astype(BF16)
        small_ref[...] = jnp.dot(h_ref[...], ws_ref[...], preferred_element_type=F32)

    o_ref[...] = jnp.dot(h_ref[...], w_ref[...], preferred_element_type=F32).astype(o_ref.dtype)


def _norm_matmul(x2, norm_w, w, w_small, tm, tn):
    t, d = x2.shape
    n = w.shape[1]
    ns = w_small.shape[1]
    return pl.pallas_call(
        _norm_matmul_kernel,
        grid=(t // tm, n // tn),
        in_specs=[
            pl.BlockSpec((tm, d), lambda i, j: (i, 0)),
            pl.BlockSpec((1, d), lambda i, j: (0, 0)),
            pl.BlockSpec((d, tn), lambda i, j: (0, j)),
            pl.BlockSpec((d, ns), lambda i, j: (0, 0)),
        ],
        out_specs=[pl.BlockSpec((tm, tn), lambda i, j: (i, j)), pl.BlockSpec((tm, ns), lambda i, j: (i, 0))],
        out_shape=[jax.ShapeDtypeStruct((t, n), BF16), jax.ShapeDtypeStruct((t, ns), F32)],
        scratch_shapes=[pltpu.VMEM((tm, d), BF16)],
        compiler_params=_params("parallel", "arbitrary"),
        name="norm_matmul",
    )(x2, norm_w.reshape(1, d), w, w_small)


def _ssd_kernel(z_ref, xbc_ref, sm_ref, dtt_ref, convw_ref, convb_ref, dtb_ref, dtbt_ref,
                alog_ref, alogt_ref, dskip_ref, nw_ref, expand_ref, y_ref,
                xbuf, state, ydiag):
    L = SSD_CHUNK
    c = pl.program_id(1)

    @pl.when(c == 0)
    def _():
        xbuf[0:8, :] = jnp.zeros((8, XBC_DIM), F32)
        state[...] = jnp.zeros(state.shape, F32)

    xbuf[8:8 + L, :] = xbc_ref[...].astype(F32)
    acc = convb_ref[...] + convw_ref[0:1, :] * xbuf[5:5 + L, :]
    for k in range(1, CONV_K):
        acc = acc + convw_ref[k:k + 1, :] * xbuf[5 + k:5 + k + L, :]
    xbuf[0:8, :] = xbuf[L:L + 8, :]
    xbc = _silu(acc)
    xs = xbc[:, :D_INNER]

    lane = lax.broadcasted_iota(jnp.int32, (1, LANES), 1)
    a_row = jnp.where(lane < SSM_HEADS, -jnp.exp(alog_ref[...]), 0.0)
    dt = _softplus(sm_ref[...] + dtb_ref[...])
    a_dt = dt * a_row
    dt_t = _softplus(dtt_ref[0] + dtbt_ref[...])
    a_dt_t = dt_t * (-jnp.exp(alogt_ref[...]))

    row = lax.broadcasted_iota(jnp.int32, (L, L), 0)
    col = lax.broadcasted_iota(jnp.int32, (L, L), 1)
    lower = row >= col
    tri = lower.astype(BF16)
    tri_t = (row <= col).astype(BF16)
    cs = sum(jnp.dot(tri, part, preferred_element_type=F32) for part in _split3(a_dt))
    cs_t = sum(jnp.dot(part, tri_t, preferred_element_type=F32) for part in _split3(a_dt_t))
    cs_last = cs[L - 1:L, :]

    stacked = jnp.concatenate([dt, jnp.exp(cs_last - cs), jnp.exp(cs)], axis=0)
    wide = sum(jnp.dot(part, expand_ref[...], preferred_element_type=F32) for part in _split3(stacked))
    dt_x = wide[0:L]
    dte_x = wide[L:2 * L]
    ecs_x = wide[2 * L:3 * L]

    xdt = xs * dt_x
    xw = (xdt * dte_x).astype(BF16)
    xdt_b = xdt.astype(BF16)

    hpg = SSM_HEADS // SSM_GROUPS
    gw = hpg * SSM_HEAD_DIM
    y_off_parts = []
    for g in range(SSM_GROUPS):
        b_g = xbc[:, D_INNER + g * D_STATE:D_INNER + (g + 1) * D_STATE].astype(BF16)
        c_g = xbc[:, D_INNER + SSM_GROUPS * D_STATE + g * D_STATE:
                  D_INNER + SSM_GROUPS * D_STATE + (g + 1) * D_STATE].astype(BF16)
        cb = lax.dot_general(c_g, b_g, (((1,), (1,)), ((), ())), preferred_element_type=F32)
        for hh in range(hpg):
            h = g * hpg + hh
            seg = cs[:, h:h + 1] - cs_t[h:h + 1, :]
            decay = jnp.exp(jnp.where(lower, seg, -jnp.inf))
            m = (cb * decay).astype(BF16)
            ydiag[:, h * SSM_HEAD_DIM:(h + 1) * SSM_HEAD_DIM] = jnp.dot(
                m, xdt_b[:, h * SSM_HEAD_DIM:(h + 1) * SSM_HEAD_DIM], preferred_element_type=F32)
        st_prev = state[g]
        y_off_parts.append(jnp.dot(c_g, st_prev.astype(BF16), preferred_element_type=F32))
        st_new = lax.dot_general(b_g, xw[:, g * gw:(g + 1) * gw], (((0,), (0,)), ((), ())),
                                 preferred_element_type=F32)
        state[g] = st_prev * ecs_x[L - 1:L, g * gw:(g + 1) * gw] + st_new
    y_off = jnp.concatenate(y_off_parts, axis=1) * ecs_x

    y = (ydiag[...] + y_off + dskip_ref[...] * xs) * _silu(z_ref[...].astype(F32))
    for g in range(SSM_GROUPS):
        yg = y[:, g * gw:(g + 1) * gw]
        yg = yg * lax.rsqrt(jnp.mean(yg * yg, axis=-1, keepdims=True) + NORM_EPS)
        y_ref[:, g * gw:(g + 1) * gw] = (yg * nw_ref[:, g * gw:(g + 1) * gw]).astype(y_ref.dtype)


def _ssd(proj, small, dt_t, conv_w, conv_b, dt_bias, a_log, d_skip, norm_w, bsz, seq):
    L = SSD_CHUNK
    nc = seq // L
    pad = LANES - SSM_HEADS
    dtb = jnp.pad(dt_bias, (0, pad)).reshape(1, LANES)
    alog = jnp.pad(a_log, (0, pad)).reshape(1, LANES)
    dskip_x = jnp.repeat(d_skip, SSM_HEAD_DIM).reshape(1, D_INNER)
    expand = (np.arange(LANES)[:, None] == (np.arange(D_INNER)[None, :] // SSM_HEAD_DIM)).astype(np.float32)
    const = lambda shape: pl.BlockSpec(shape, lambda b, c: (0,) * len(shape))
    return pl.pallas_call(
        _ssd_kernel,
        grid=(bsz, nc),
        in_specs=[
            pl.BlockSpec((L, D_INNER), lambda b, c: (b * nc + c, F_Z // D_INNER)),
            pl.BlockSpec((L, XBC_DIM), lambda b, c: (b * nc + c, F_XBC // XBC_DIM)),
            pl.BlockSpec((L, LANES), lambda b, c: (b * nc + c, 0)),
            pl.BlockSpec((1, SSM_HEADS, L), lambda b, c: (b, 0, c)),
            const((CONV_K, XBC_DIM)), const((1, XBC_DIM)),
            const((1, LANES)), const((SSM_HEADS, 1)),
            const((1, LANES)), const((SSM_HEADS, 1)),
            const((1, D_INNER)), const((1, D_INNER)),
            const((LANES, D_INNER)),
        ],
        out_specs=pl.BlockSpec((L, D_INNER), lambda b, c: (b * nc + c, 0)),
        out_shape=jax.ShapeDtypeStruct((bsz * seq, D_INNER), BF16),
        scratch_shapes=[
            pltpu.VMEM((L + 8, XBC_DIM), F32),
            pltpu.VMEM((SSM_GROUPS, D_STATE, (SSM_HEADS // SSM_GROUPS) * SSM_HEAD_DIM), F32),
            pltpu.VMEM((L, D_INNER), F32),
        ],
        compiler_params=_params("parallel", "arbitrary"),
        name="ssd",
    )(proj, proj, small, dt_t, conv_w, conv_b.reshape(1, XBC_DIM),
      dtb, dt_bias.reshape(SSM_HEADS, 1), alog, a_log.reshape(SSM_HEADS, 1),
      dskip_x, norm_w.reshape(1, D_INNER), jnp.asarray(expand, dtype=BF16))


def _gelu_tanh(x):
    return 0.5 * x * (1.0 + jnp.tanh(np.sqrt(2.0 / np.pi) * (x + 0.044715 * (x * x * x))))


def _compress_kernel(cur_ref, nxt_ref, pos_ref, w1_ref, w2_ref, o_ref):
    half = CMP_STRIDE * NSA_HEAD_DIM
    lo = (cur_ref[0, 0].astype(F32) + pos_ref[0:1, :]).astype(BF16)
    hi = (nxt_ref[0, 0].astype(F32) + pos_ref[1:2, :]).astype(BF16)
    hid = (jnp.dot(lo, w1_ref[0:half, :], preferred_element_type=F32)
           + jnp.dot(hi, w1_ref[half:2 * half, :], preferred_element_type=F32))
    o_ref[0, 0] = jnp.dot(_gelu_tanh(hid).astype(BF16), w2_ref[...],
                          preferred_element_type=F32).astype(o_ref.dtype)


def _compress(kv, pos, w1, w2, bsz, seq):
    nch = seq // CMP_STRIDE
    half = CMP_STRIDE * NSA_HEAD_DIM
    ch = kv.reshape(bsz, nch, CMP_STRIDE, NSA_GROUPS, NSA_HEAD_DIM)
    ch = jnp.transpose(ch, (0, 3, 1, 2, 4)).reshape(bsz, NSA_GROUPS, nch, half)
    nxt = jnp.concatenate([ch[:, :, 1:], jnp.zeros_like(ch[:, :, :1])], axis=2)
    blk = pl.BlockSpec((1, 1, nch, half), lambda b, g: (b, g, 0, 0))
    return pl.pallas_call(
        _compress_kernel,
        grid=(bsz, NSA_GROUPS),
        in_specs=[
            blk, blk,
            pl.BlockSpec((2, half), lambda b, g: (0, 0)),
            pl.BlockSpec((2 * half, CMP_HIDDEN), lambda b, g: (0, 0)),
            pl.BlockSpec((CMP_HIDDEN, NSA_HEAD_DIM), lambda b, g: (0, 0)),
        ],
        out_specs=pl.BlockSpec((1, 1, nch, NSA_HEAD_DIM), lambda b, g: (b, g, 0, 0)),
        out_shape=jax.ShapeDtypeStruct((bsz, NSA_GROUPS, nch, NSA_HEAD_DIM), BF16),
        compiler_params=_params("parallel", "parallel"),
        name="compress",
    )(ch, nxt, pos.reshape(2, half), w1.astype(BF16), w2.astype(BF16))


NSA_TQ = 512
NSA_KC = 512
NSA_SUB = 256
MASK_VALUE = -1e30
SOFTMAX_M0 = -1e20


def _nsa_kernel(q_ref, kc_ref, vc_ref, k_ref, v_ref, onehot_ref, gate_ref, overlap_ref, gexp_ref, o_ref):
    R, TQ, KC, DH, SUB = NSA_REP, NSA_TQ, NSA_KC, NSA_HEAD_DIM, NSA_SUB
    assert 2 * DH == LANES and TQ == KC and TQ % SUB == 0 and WINDOW % SUB == 0
    i = pl.program_id(2)
    q0 = i * TQ
    nt = (((1,), (1,)), ((), ()))
    tcol = q0 + lax.broadcasted_iota(jnp.int32, (TQ, 1), 0)
    qblk = q_ref[...]
    scale = DH ** -0.5
    qf = [qblk[:, r * DH:(r + 1) * DH].astype(F32) * scale for r in range(R)]
    qh = [q.astype(BF16) for q in qf]
    v_width = LANES
    sel_half = lax.broadcasted_iota(jnp.int32, (1, LANES), 1) < DH
    one = jnp.ones((), BF16)

    def flash_step(m_i, acc, q, k, v_ext, mask_bias=None):
        s = lax.dot_general(q, k, nt, preferred_element_type=F32)
        if mask_bias is not None:
            heads = s.shape[0] // mask_bias.shape[0]
            s = (s.reshape((heads,) + mask_bias.shape) + mask_bias[None]).reshape(s.shape)
        m_new = jnp.maximum(m_i, jnp.max(s, axis=-1, keepdims=True))
        p = jnp.exp((s - m_new).astype(BF16))
        acc = acc * jnp.exp(m_i - m_new) + jnp.dot(p, v_ext, preferred_element_type=F32)
        return m_new, acc

    def spread(x, mat):
        hi = x.astype(BF16)
        lo = (x - hi.astype(F32)).astype(BF16)
        return jnp.dot(hi, mat, preferred_element_type=F32) + jnp.dot(lo, mat, preferred_element_type=F32)

    kc = kc_ref[0, 0]
    vc = vc_ref[0, 0]
    ncmp = kc.shape[0]
    cmp_end = lax.broadcasted_iota(jnp.int32, (1, ncmp), 1) * CMP_STRIDE + (CMP_BLOCK - 1)
    cmask = cmp_end <= tcol
    lg = lax.dot_general(jnp.concatenate(qh, axis=0), kc, nt, preferred_element_type=F32).reshape(R, TQ, ncmp)
    lg = jnp.where(cmask[None], lg, -jnp.inf)
    mx = jnp.max(lg, axis=-1, keepdims=True)
    mx = jnp.where(mx > -jnp.inf, mx, 0.0)
    e = jnp.where(cmask[None], jnp.exp(lg - mx), 0.0)
    p = e / jnp.maximum(jnp.sum(e, axis=-1, keepdims=True), 1e-30)
    o_all = jnp.dot(p.reshape(R * TQ, ncmp).astype(BF16), vc, preferred_element_type=F32)
    o_cmp = [o_all[r * TQ:(r + 1) * TQ] for r in range(R)]
    p4 = (p[0] + p[1]) + (p[2] + p[3])

    imp = sum(lax.dot_general(overlap_ref[...], part, nt, preferred_element_type=F32)
              for part in _split3(p4))
    nsel = imp.shape[0]
    trow = q0 + lax.broadcasted_iota(jnp.int32, (1, TQ), 1)
    cur = trow // SEL_BLOCK
    jj = lax.broadcasted_iota(jnp.int32, (nsel, 1), 0)
    forced = (jj == 0) | (jj == cur) | (jj == cur - 1)
    val = jnp.where(forced, jnp.inf, jnp.where(jj <= cur, imp, -jnp.inf))
    rank = jnp.zeros((nsel, TQ), jnp.int32)
    for b in range(nsel):
        vb = val[b:b + 1, :]
        beats = (vb > val) | ((vb == val) & (jj > b))
        rank = rank + beats.astype(jnp.int32)
    bias = jnp.transpose(jnp.where(rank < SEL_TOPN, 0.0, MASK_VALUE))
    pad = jnp.zeros((TQ, LANES - DH - nsel), F32)
    qa = [jnp.concatenate([qf[r], bias, pad], axis=1).astype(BF16) for r in range(R)]
    qw = [jnp.concatenate([jnp.zeros((TQ, DH), F32), qf[r]], axis=1).astype(BF16) for r in range(R)]

    qa_all = jnp.concatenate(qa, axis=0)

    def sel_kv(k0, n):
        k = jnp.where(sel_half, k_ref[pl.ds(k0, n), :], onehot_ref[pl.ds(k0, n), :])
        v = jnp.where(sel_half, v_ref[pl.ds(k0, n), :], one)
        return k, v

    def sel_step(kb, carry):
        k, v = sel_kv(pl.multiple_of(kb * KC, KC), KC)
        return flash_step(carry[0], carry[1], qa_all, k, v)

    n_full = q0 // KC
    m_all, acc_all = lax.fori_loop(0, n_full, sel_step, (jnp.full((R * TQ, 1), SOFTMAX_M0, F32),
                                                         jnp.zeros((R * TQ, v_width), F32)))
    carry = tuple((m_all[r * TQ:(r + 1) * TQ], acc_all[r * TQ:(r + 1) * TQ]) for r in range(R))

    d0 = pl.multiple_of(q0, TQ)
    stack = lambda parts: jnp.concatenate(parts, axis=0)
    sel_parts, win_parts = [], []
    for j in range(TQ // SUB):
        rows = slice(j * SUB, (j + 1) * SUB)
        tsub = tcol[rows]
        width = (j + 1) * SUB
        bias_d = jnp.where(d0 + lax.broadcasted_iota(jnp.int32, (1, width), 1) <= tsub, 0.0, MASK_VALUE)
        k_d, v_d = sel_kv(d0, width)
        w0 = pl.multiple_of(jnp.maximum(q0 + j * SUB - WINDOW, 0), SUB)
        kpos_w = w0 + lax.broadcasted_iota(jnp.int32, (1, WINDOW + SUB), 1)
        bias_w = jnp.where((kpos_w <= tsub) & (kpos_w > tsub - WINDOW), 0.0, MASK_VALUE)
        k_w = k_ref[pl.ds(w0, WINDOW + SUB), :]
        v_w = jnp.where(sel_half, one, v_ref[pl.ds(w0, WINDOW + SUB), :])
        sel_parts.append(flash_step(stack([carry[r][0][rows] for r in range(R)]),
                                    stack([carry[r][1][rows] for r in range(R)]),
                                    stack([qa[r][rows] for r in range(R)]), k_d, v_d, bias_d)[1])
        win_parts.append(flash_step(jnp.full((R * SUB, 1), SOFTMAX_M0, F32), jnp.zeros((R * SUB, v_width), F32),
                                    stack([qw[r][rows] for r in range(R)]), k_w, v_w, bias_w)[1])
    unstack = lambda parts, r: stack([p[r * SUB:(r + 1) * SUB] for p in parts])
    acc_s = [unstack(sel_parts, r) for r in range(R)]
    acc_w = [unstack(win_parts, r) for r in range(R)]

    gx = spread(_sigmoid(gate_ref[...]), gexp_ref[0])

    def normalised(accs, lo):
        return jnp.concatenate([(acc / jnp.maximum(pltpu.roll(acc, DH, 1), 1e-30))[:, lo:lo + DH] for acc in accs],
                               axis=1)

    w = R * DH
    out = (gx[:, 0:w] * jnp.concatenate(o_cmp, axis=1) + gx[:, w:2 * w] * normalised(acc_s, 0)
           + gx[:, 2 * w:3 * w] * normalised(acc_w, DH))
    o_ref[...] = out.astype(o_ref.dtype)


def _nsa(proj, small, kc, vc, bsz, seq):
    G, R, DH, TQ = NSA_GROUPS, NSA_REP, NSA_HEAD_DIM, NSA_TQ
    n_cmp_pad = seq // CMP_STRIDE
    n_sel = seq // SEL_BLOCK
    nq = seq // TQ

    assert NSA_KC % TQ == 0 and seq % NSA_KC == 0 and TQ % WINDOW == 0 and 2 * DH == LANES

    assert n_sel <= LANES - DH
    block_onehot = np.zeros((seq, LANES), np.float32)
    block_onehot[np.arange(seq), DH + np.arange(seq) // SEL_BLOCK] = 1.0

    c_start = np.arange(n_cmp_pad) * CMP_STRIDE
    s_start = np.arange(n_sel) * SEL_BLOCK
    overlap = ((c_start[:, None] < s_start[None, :] + SEL_BLOCK)
               & (c_start[:, None] + CMP_BLOCK > s_start[None, :])).astype(np.float32)
    overlap[(seq - CMP_BLOCK) // CMP_STRIDE + 1:] = 0.0
    col = np.arange(3 * R * DH)
    gate_lane = SSM_HEADS + 3 * ((col % (R * DH)) // DH) + col // (R * DH)
    gexp = np.stack([(np.arange(LANES)[:, None] == (gate_lane + 3 * R * g)[None, :]) for g in range(G)])

    qblk = pl.BlockSpec((TQ, R * DH), lambda b, g, i: (b * nq + i, P_Q // (R * DH) + g))
    oblk = pl.BlockSpec((TQ, R * DH), lambda b, g, i: (b * nq + i, g))
    cblk = pl.BlockSpec((1, 1, n_cmp_pad, DH), lambda b, g, i: (b, g, 0, 0))
    return pl.pallas_call(
        _nsa_kernel,
        grid=(bsz, G, nq),
        in_specs=[
            qblk, cblk, cblk,
            pl.BlockSpec((seq, LANES), lambda b, g, i: (b, P_KSW // LANES + g)),
            pl.BlockSpec((seq, LANES), lambda b, g, i: (b, P_VSW // LANES + g)),
            pl.BlockSpec((seq, LANES), lambda b, g, i: (0, 0)),
            pl.BlockSpec((TQ, LANES), lambda b, g, i: (b * nq + i, 0)),
            pl.BlockSpec((n_sel, n_cmp_pad), lambda b, g, i: (0, 0)),
            pl.BlockSpec((1, LANES, 3 * R * DH), lambda b, g, i: (g, 0, 0)),
        ],
        out_specs=oblk,
        out_shape=jax.ShapeDtypeStruct((bsz * seq, G * R * DH), BF16),
        compiler_params=_params("parallel", "parallel", "arbitrary"),
        name="nsa_attention",
    )(proj, kc, vc, proj, proj, jnp.asarray(block_onehot, dtype=BF16), small,
      jnp.asarray(overlap.T, dtype=BF16), jnp.asarray(gexp, dtype=BF16))


ROW_TILES = D_MODEL // LANES


def _store_token_major(ref, val):
    n = val.shape[0]
    for c in range(ROW_TILES):
        ref[pl.ds(c, n, stride=ROW_TILES), :] = val[:, c * LANES:(c + 1) * LANES]


def _load_token_major(ref, n):
    return jnp.concatenate([ref[pl.ds(c, n, stride=ROW_TILES), :] for c in range(ROW_TILES)], axis=1)


def _merge_kernel(ys_ref, yn_ref, gs_ref, gn_ref, x_ref, wbs_ref, wbn_ref, wo_ref, fw_ref,
                  rwh_ref, rwl_ref, rb_ref, tri_ref, x1_ref, h_ref, ti_ref, tw_ref, slot_ref, cnt_ref, base_ref):
    @pl.when(pl.program_id(0) == 0)
    def _():
        base_ref[...] = jnp.zeros(base_ref.shape, F32)

    a = jnp.dot(ys_ref[...], wbs_ref[...], preferred_element_type=F32)
    b = jnp.dot(yn_ref[...], wbn_ref[...], preferred_element_type=F32)
    merged = _sigmoid(gs_ref[...].astype(F32)) * a + _sigmoid(gn_ref[...].astype(F32)) * b
    x1 = x_ref[...] + jnp.dot(merged.astype(BF16), wo_ref[...], preferred_element_type=F32)
    x1_ref[...] = x1
    var = jnp.mean(x1 * x1, axis=-1, keepdims=True)
    h = x1 * lax.rsqrt(var + NORM_EPS) * fw_ref[...]
    _store_token_major(h_ref, h)

    nt = (((1,), (1,)), ((), ()))
    h_hi = h.astype(BF16)
    h_lo = (h - h_hi.astype(F32)).astype(BF16)
    logits = (lax.dot_general(rwh_ref[...], h_hi, nt, preferred_element_type=F32)
              + (lax.dot_general(rwl_ref[...], h_hi, nt, preferred_element_type=F32)
                 + lax.dot_general(rwh_ref[...], h_lo, nt, preferred_element_type=F32))) + rb_ref[...]
    ne = logits.shape[0]
    jj = lax.broadcasted_iota(jnp.int32, (ne, 1), 0)
    rank = jnp.zeros(logits.shape, jnp.int32)
    for e in range(ne):
        ce = logits[e:e + 1, :]
        beats = (ce > logits) | ((ce == logits) & (jj > e))
        rank = rank + beats.astype(jnp.int32)
    sel = rank < TOP_K
    mx = jnp.max(logits, axis=0, keepdims=True)
    p = jnp.where(sel, jnp.exp(logits - mx), 0.0)
    p = p / jnp.sum(p, axis=0, keepdims=True)

    self01 = sel.astype(F32)
    before = jnp.dot(self01.astype(BF16), tri_ref[...], preferred_element_type=F32) + base_ref[...]
    for k in range(TOP_K):
        hit = rank == k
        ti_ref[k:k + 1, :] = jnp.sum(jnp.where(hit, jj, 0), axis=0, keepdims=True)
        tw_ref[k:k + 1, :] = jnp.sum(jnp.where(hit, p, 0.0), axis=0, keepdims=True)
        slot_ref[k:k + 1, :] = jnp.sum(jnp.where(hit, before, 0.0), axis=0, keepdims=True).astype(jnp.int32)
    base_ref[...] = base_ref[...] + jnp.sum(self01, axis=1, keepdims=True)
    cnt_ref[...] = base_ref[...]


def _merge_route(y_ssm, y_nsa, proj, x2, wbs, wbn, wo, ffn_w, router_w, router_b, tm):
    t, d = x2.shape
    row = lambda col: pl.BlockSpec((tm, d), lambda i: (i, col))
    const = lambda shape: pl.BlockSpec(shape, lambda i: (0,) * len(shape))
    kblk = pl.BlockSpec((TOP_K, tm), lambda i: (0, i))
    tri = np.triu(np.ones((tm, tm), np.float32), 1)
    rw_t = router_w.T
    rw_hi = rw_t.astype(BF16)
    rw_lo = (rw_t - rw_hi.astype(F32)).astype(BF16)
    return pl.pallas_call(
        _merge_kernel,
        grid=(t // tm,),
        in_specs=[
            row(0), row(0), row(F_MS // d), row(F_MN // d), row(0),
            const((d, d)), const((d, d)), const((d, d)), const((1, d)),
            const((N_EXPERTS, d)), const((N_EXPERTS, d)), const((N_EXPERTS, 1)), const((tm, tm)),
        ],
        out_specs=[row(0), pl.BlockSpec((tm * ROW_TILES, LANES), lambda i: (i, 0)),
                   kblk, kblk, kblk, const((N_EXPERTS, 1))],
        out_shape=[jax.ShapeDtypeStruct((t, d), F32), jax.ShapeDtypeStruct((t * ROW_TILES, LANES), F32),
                   jax.ShapeDtypeStruct((TOP_K, t), jnp.int32), jax.ShapeDtypeStruct((TOP_K, t), F32),
                   jax.ShapeDtypeStruct((TOP_K, t), jnp.int32), jax.ShapeDtypeStruct((N_EXPERTS, 1), F32)],
        scratch_shapes=[pltpu.VMEM((N_EXPERTS, 1), F32)],
        compiler_params=_params("arbitrary"),
        name="merge_route",
    )(y_ssm, y_nsa, proj, proj, x2, wbs.astype(BF16), wbn.astype(BF16), wo.astype(BF16),
      ffn_w.reshape(1, d), rw_hi, rw_lo, router_b.reshape(N_EXPERTS, 1), jnp.asarray(tri, dtype=BF16))


MOE_TM = 256
MOE_RING = 3
GU_CHUNK = 2 * LANES


def _moe_kernel(te_ref, nu_ref, tok0_ref, tok1_ref, tokn_ref, dst_ref, h_hbm, wgu_ref, bgu_ref, wd_ref, bd_ref,
                perm_ref, y_hbm, xbuf, ybuf, wgu_s, wd_s, gsem, ssem, *, spare_row):
    tm, rt = MOE_TM, ROW_TILES
    i = pl.program_id(0)
    nu = nu_ref[0]
    slot = i % MOE_RING
    prev = (i + MOE_RING - 1) % MOE_RING
    new_expert = (i == 0) | (te_ref[i] != te_ref[jnp.maximum(i - 1, 0)])

    @pl.when(new_expert & (i < nu))
    def _():
        wd_s[...] = wd_ref[0].astype(BF16)
        for c in range(wgu_s.shape[1] // GU_CHUNK):
            cols = slice(c * GU_CHUNK, (c + 1) * GU_CHUNK)
            wgu_s[:, cols] = jnp.dot(wgu_ref[0, :, cols].astype(BF16), perm_ref[...],
                                     preferred_element_type=F32).astype(BF16)

    def gather_copy(src_row, r, sl):
        return pltpu.make_async_copy(h_hbm.at[pl.ds(src_row, rt), :], xbuf.at[sl, pl.ds(r * rt, rt), :],
                                     gsem.at[sl])

    def scatter_copy(dst_row, r, sl):
        return pltpu.make_async_copy(ybuf.at[sl, pl.ds(r * rt, rt), :], y_hbm.at[pl.ds(dst_row, rt), :],
                                     ssem.at[sl])

    def start_gather(tok_ref, sl):
        for r in range(tm):
            gather_copy(pl.multiple_of(tok_ref[0, 0, r], rt), r, sl).start()

    def start_scatter(sl):
        for r in range(tm):
            scatter_copy(pl.multiple_of(dst_ref[0, 0, r], rt), r, sl).start()

    def wait_gather(sl):
        pltpu.make_async_copy(h_hbm.at[pl.ds(0, tm * rt), :], xbuf.at[sl], gsem.at[sl]).wait()

    def wait_scatter(sl):
        pltpu.make_async_copy(ybuf.at[sl], y_hbm.at[pl.ds(0, tm * rt), :], ssem.at[sl]).wait()

    @pl.when(i == 0)
    def _():
        ybuf[...] = jnp.zeros(ybuf.shape, ybuf.dtype)
        start_gather(tok0_ref, 0)
        start_gather(tok1_ref, 1)
        for sl in range(2):
            for r in range(tm):
                scatter_copy((spare_row + sl * tm + r) * rt, r, sl).start()

    @pl.when(i < nu)
    def _():
        wait_gather(slot)
        wait_scatter(slot)
        x = _load_token_major(xbuf.at[slot], tm).astype(BF16)
        start_gather(tokn_ref, prev)
        start_scatter(prev)
        h1 = jnp.dot(x, wgu_s[...], preferred_element_type=F32) + bgu_ref[0]
        nchunk = h1.shape[1] // GU_CHUNK
        glu = jnp.concatenate([h1[:, c * GU_CHUNK:c * GU_CHUNK + LANES] for c in range(nchunk)], axis=1)
        lin = jnp.concatenate([h1[:, c * GU_CHUNK + LANES:(c + 1) * GU_CHUNK] for c in range(nchunk)], axis=1)
        glu = jnp.minimum(glu, SWIGLU_LIMIT)
        lin = jnp.clip(lin, -SWIGLU_LIMIT, SWIGLU_LIMIT)
        act = glu * _sigmoid(SWIGLU_ALPHA * glu) * (lin + 1.0)
        y = jnp.dot(act.astype(BF16), wd_s[...], preferred_element_type=F32) + bd_ref[0]
        _store_token_major(ybuf.at[slot], y)

    @pl.when(i == nu)
    def _():
        nxt = (i + 1) % MOE_RING
        wait_gather(slot)
        wait_gather(nxt)
        wait_scatter(slot)
        wait_scatter(nxt)
        start_scatter(prev)
        wait_scatter(prev)


def _moe_experts(h_rows, tok_tiles, dst_tiles, tile_expert, n_used, wgu, bgu, wd, bd, n_out_rows, spare_row):
    d = wd.shape[1]
    tm, rt = MOE_TM, ROW_TILES
    n_tiles = tok_tiles.shape[0]
    perm = np.zeros((GU_CHUNK, GU_CHUNK), np.float32)
    perm[2 * np.arange(LANES), np.arange(LANES)] = 1.0
    perm[2 * np.arange(LANES) + 1, LANES + np.arange(LANES)] = 1.0
    smem = lambda f: pl.BlockSpec((1, 1, tm), f, memory_space=pltpu.SMEM)
    grid_spec = pltpu.PrefetchScalarGridSpec(
        num_scalar_prefetch=2,
        grid=(n_tiles,),
        in_specs=[
            smem(lambda i, te, nu: (0, 0, 0)),
            smem(lambda i, te, nu: (1, 0, 0)),
            smem(lambda i, te, nu: (jnp.minimum(i + MOE_RING - 1, n_tiles - 1), 0, 0)),
            smem(lambda i, te, nu: (i, 0, 0)),
            pl.BlockSpec(memory_space=pl.ANY),
            pl.BlockSpec((1, d, 2 * d), lambda i, te, nu: (te[i], 0, 0)),
            pl.BlockSpec((1, 1, 2 * d), lambda i, te, nu: (te[i], 0, 0)),
            pl.BlockSpec((1, d, d), lambda i, te, nu: (te[i], 0, 0)),
            pl.BlockSpec((1, 1, d), lambda i, te, nu: (te[i], 0, 0)),
            pl.BlockSpec((GU_CHUNK, GU_CHUNK), lambda i, te, nu: (0, 0)),
        ],
        out_specs=pl.BlockSpec(memory_space=pl.ANY),
        scratch_shapes=[
            pltpu.VMEM((MOE_RING, tm * rt, LANES), F32),
            pltpu.VMEM((MOE_RING, tm * rt, LANES), F32),
            pltpu.VMEM((d, 2 * d), BF16),
            pltpu.VMEM((d, d), BF16),
            pltpu.SemaphoreType.DMA((MOE_RING,)),
            pltpu.SemaphoreType.DMA((MOE_RING,)),
        ],
    )
    return pl.pallas_call(
        functools.partial(_moe_kernel, spare_row=spare_row),
        grid_spec=grid_spec,
        out_shape=jax.ShapeDtypeStruct((n_out_rows * rt, LANES), F32),
        compiler_params=_params("arbitrary"),
        name="moe_experts",
    )(tile_expert, n_used, tok_tiles, tok_tiles, tok_tiles, dst_tiles, h_rows, wgu, bgu, wd, bd,
      jnp.asarray(perm, dtype=BF16))


def _final_kernel(x1_ref, y0_ref, y1_ref, y2_ref, y3_ref, tw_ref, nw_ref, o_ref):
    tm = x1_ref.shape[0]
    tw = tw_ref[...]
    yk = [_load_token_major(r, tm) for r in (y0_ref, y1_ref, y2_ref, y3_ref)]
    moe = ((tw[:, 0:1] * yk[0] + tw[:, 1:2] * yk[1]) + (tw[:, 2:3] * yk[2] + tw[:, 3:4] * yk[3]))
    x = x1_ref[...] + moe
    var = jnp.mean(x * x, axis=-1, keepdims=True)
    o_ref[...] = x * lax.rsqrt(var + NORM_EPS) * nw_ref[...]


def _final_norm(x1, y_rows, top_w, norm_w, tm):
    t, d = x1.shape
    nt = t // tm
    yblk = lambda k: pl.BlockSpec((tm * ROW_TILES, LANES), lambda i: (k * nt + i, 0))
    return pl.pallas_call(
        _final_kernel,
        grid=(nt,),
        in_specs=[
            pl.BlockSpec((tm, d), lambda i: (i, 0)),
            yblk(0), yblk(1), yblk(2), yblk(3),
            pl.BlockSpec((tm, TOP_K), lambda i: (i, 0)),
            pl.BlockSpec((1, d), lambda i: (0, 0)),
        ],
        out_specs=pl.BlockSpec((tm, d), lambda i: (i, 0)),
        out_shape=jax.ShapeDtypeStruct((t, d), F32),
        compiler_params=_params("parallel"),
        name="final_norm",
    )(x1, y_rows, y_rows, y_rows, y_rows, top_w, norm_w.reshape(1, d))


def _dispatch_tables(top_i, slot, counts, n_rows):
    t = top_i.shape[1]
    n = t * TOP_K
    tm, rt = MOE_TM, ROW_TILES
    padded = ((counts + tm - 1) // tm) * tm
    pend = jnp.cumsum(padded)
    pstart = pend - padded
    onehot = top_i[:, :, None] == jnp.arange(N_EXPERTS, dtype=jnp.int32)
    dest = slot + jnp.sum(jnp.where(onehot, pstart, 0), axis=-1)
    row_pair = jnp.full((n_rows,), -1, jnp.int32).at[dest.reshape(-1)].set(jnp.arange(n, dtype=jnp.int32))
    live = row_pair >= 0
    tok_tiles = (jnp.where(live, row_pair % t, 0) * rt).reshape(n_rows // tm, 1, tm)
    rows = jnp.arange(n_rows, dtype=jnp.int32)
    out_row = jnp.where(live, row_pair, n + ((rows // tm) % 2) * tm + rows % tm)
    spare = n + 2 * tm + jnp.arange(tm, dtype=jnp.int32)
    dst_tiles = (jnp.concatenate([spare, out_row]) * rt).reshape(n_rows // tm + 1, 1, tm)
    tile_start = jnp.arange(n_rows // tm, dtype=jnp.int32) * tm
    tile_expert = jnp.minimum(jnp.sum(tile_start[:, None] >= pend[None, :], axis=-1), N_EXPERTS - 1)
    n_used = (pend[-1] // tm).reshape(1)
    return tok_tiles, dst_tiles, tile_expert.astype(jnp.int32), n_used.astype(jnp.int32)


def _in_proj_weights(w_in):
    o = np.cumsum([0, D_INNER, XBC_DIM, SSM_HEADS, 1024, 256, 256, 256, 256, 256, 256, 3 * NSA_HEADS, 2 * D_MODEL])
    z, xbc, dt, q, kv, gate, mg = (w_in[:, o[0]:o[1]], w_in[:, o[1]:o[2]], w_in[:, o[2]:o[3]],
                                   w_in[:, o[3]:o[4]], w_in[:, o[4]:o[10]], w_in[:, o[10]:o[11]],
                                   w_in[:, o[11]:o[12]])
    small_pad = jnp.zeros((D_MODEL, LANES - SSM_HEADS - 3 * NSA_HEADS), w_in.dtype)
    kvw = NSA_KV_WIDTH
    k_c, v_c, k_s, v_s, k_w, v_w = [kv[:, n * kvw:(n + 1) * kvw] for n in range(6)]

    def pair_by_group(a, b):
        shape = (D_MODEL, NSA_GROUPS, 1, NSA_HEAD_DIM)
        return jnp.concatenate([a.reshape(shape), b.reshape(shape)], axis=2).reshape(D_MODEL, 2 * kvw)

    w_p = jnp.concatenate([z, mg[:, :D_MODEL], xbc, mg[:, D_MODEL:], q, k_c, v_c,
                           pair_by_group(k_s, k_w), pair_by_group(v_s, v_w)], axis=1)
    w_small = jnp.concatenate([dt, gate, small_pad], axis=1)
    return w_p.astype(BF16), w_small.astype(BF16)


def kernel(x, mix_norm_w, w_in, conv_w, conv_b, dt_bias, a_log, d_skip, ssm_norm_w, cmp_pos_k, cmp_w1_k, cmp_w2_k, cmp_pos_v, cmp_w1_v, cmp_w2_v, w_branch_ssm, w_branch_nsa, w_out, ffn_norm_w, router_w, router_b, w_gate_up, b_gate_up, w_down, b_down, final_norm_w):
    bsz, seq, d = x.shape
    t = bsz * seq
    x2 = x.reshape(t, d)
    depth = w_in.shape[0]
    assert depth == 1, "single-layer block"
    for l in range(depth):
        w_p, w_small = _in_proj_weights(w_in[l])
        proj, small = _norm_matmul(x2, mix_norm_w[l], w_p, w_small, 2048, 1536)

        dt_t = jnp.transpose(small[:, :SSM_HEADS].reshape(bsz, seq, SSM_HEADS), (0, 2, 1))
        y_ssm = _ssd(proj, small, dt_t, conv_w[l], conv_b[l], dt_bias[l], a_log[l], d_skip[l], ssm_norm_w[l],
                     bsz, seq)

        kc = _compress(proj[:, P_KC:P_KC + NSA_KV_WIDTH], cmp_pos_k[l], cmp_w1_k[l], cmp_w2_k[l], bsz, seq)
        vc = _compress(proj[:, P_VC:P_VC + NSA_KV_WIDTH], cmp_pos_v[l], cmp_w1_v[l], cmp_w2_v[l], bsz, seq)
        y_nsa = _nsa(proj, small, kc, vc, bsz, seq)

        x1, h, top_i, top_w, slot, counts = _merge_route(
            y_ssm, y_nsa, proj, x2, w_branch_ssm[l], w_branch_nsa[l], w_out[l], ffn_norm_w[l],
            router_w[l], router_b[l], 512)

        n_rows = t * TOP_K + N_EXPERTS * MOE_TM
        tok_tiles, dst_tiles, tile_expert, n_used = _dispatch_tables(
            top_i, slot, counts[:, 0].astype(jnp.int32), n_rows)
        nchunk = 2 * d // GU_CHUNK
        bgu = jnp.transpose(b_gate_up[l].reshape(N_EXPERTS, nchunk, LANES, 2), (0, 1, 3, 2))
        y_rows = _moe_experts(h, tok_tiles, dst_tiles, tile_expert, n_used, w_gate_up[l],
                              bgu.reshape(N_EXPERTS, 1, 2 * d), w_down[l], b_down[l][:, None, :],
                              n_out_rows=t * TOP_K + 3 * MOE_TM, spare_row=t * TOP_K)
    out = _final_norm(x1, y_rows, top_w.T, final_norm_w, MOE_TM)
    return out.reshape(bsz, seq, d)
```
